```python
import jax, jax.numpy as jnp
from jax import lax
import numpy as np

D_MODEL = 4096
BATCH = 2
SEQ = 8192
DEPTH = 2

N_A_LAYERS = DEPTH // 2
N_B_LAYERS = DEPTH - N_A_LAYERS
N_DENSE = (DEPTH + 1) // 2
N_MOE = DEPTH // 2
CONV_WIDTH = 3
HEAD_DIM = 128
N_HEADS = D_MODEL // HEAD_DIM
D_FF = 11008
N_EXPERTS = 8
TOP_K = 2
D_EXPERT = D_MODEL
PLE_DIM = 256
Q_BLOCK = 128
EPS = 1e-6

kernel_name = 'yoco_shortconv_stickbreaking_moe_trunk'


def rmsnorm(x, g):
    x32 = x.astype(jnp.float32)
    y = x32 * lax.rsqrt(jnp.mean(x32 * x32, axis=-1, keepdims=True) + EPS)
    return (y * g.astype(jnp.float32)).astype(x.dtype)


def to_heads(t):
    b, s, _ = t.shape
    return t.reshape(b, s, N_HEADS, HEAD_DIM).transpose(0, 2, 1, 3)


def from_heads(o):
    b, h, s, dh = o.shape
    return o.transpose(0, 2, 1, 3).reshape(b, s, h * dh)


def short_conv_mixer(h, w_in, conv_w, w_out):
    s = h.shape[1]
    bcx = h @ w_in
    b_gate, c_gate, u = jnp.split(bcx, 3, axis=-1)
    u = c_gate * u
    u_pad = jnp.pad(u, ((0, 0), (CONV_WIDTH - 1, 0), (0, 0)))
    y = conv_w[0] * u_pad[:, 0:s]
    for tap in range(1, CONV_WIDTH):
        y = y + conv_w[tap] * u_pad[:, tap:tap + s]
    return (b_gate * y) @ w_out


def stick_breaking_attention(q, k, v):
    b, h, s, dh = q.shape
    n_blocks = s // Q_BLOCK
    qb = q.reshape(b, h, n_blocks, Q_BLOCK, dh).transpose(2, 0, 1, 3, 4)
    starts = jnp.arange(n_blocks, dtype=jnp.int32) * Q_BLOCK
    key_pos = jnp.arange(s, dtype=jnp.int32)
    scale = HEAD_DIM ** -0.5

    def one_block(args):
        q_blk, start = args
        z = jnp.einsum('bhqd,bhkd->bhqk', q_blk, k, preferred_element_type=jnp.float32) * scale
        q_pos = start + jnp.arange(Q_BLOCK, dtype=jnp.int32)
        strict = key_pos[None, :] < q_pos[:, None]
        log_beta = jax.nn.log_sigmoid(z)
        log_keep = jnp.where(strict, log_beta - z, 0.0)
        between = lax.cumsum(log_keep, axis=3, reverse=True) - log_keep
        w = jnp.where(strict, jnp.exp(log_beta + between), 0.0)
        return jnp.einsum('bhqk,bhkd->bhqd', w.astype(v.dtype), v)

    out = lax.map(one_block, (qb, starts))
    return out.transpose(1, 2, 0, 3, 4).reshape(b, h, s, dh)


def swiglu(h, w_gu, w_down):
    g, u = jnp.split(h @ w_gu, 2, axis=-1)
    return (jax.nn.silu(g) * u) @ w_down


def moe_swiglu(h, w_router, b_router, w_gu, w_down):
    b, s, d = h.shape
    t = h.reshape(b * s, d)
    logits = jnp.dot(t, w_router, preferred_element_type=jnp.float32) + b_router.astype(jnp.float32)
    top_logits, top_idx = lax.top_k(logits, TOP_K)
    top_w = jax.nn.softmax(top_logits, axis=-1)
    gates = jnp.sum(jax.nn.one_hot(top_idx, N_EXPERTS, dtype=jnp.float32) * top_w[..., None], axis=1)
    gates = gates.astype(h.dtype)
    y = gates[:, 0:1] * swiglu(t, w_gu[0], w_down[0])
    for e in range(1, N_EXPERTS):
        y = y + gates[:, e:e + 1] * swiglu(t, w_gu[e], w_down[e])
    return y.reshape(b, s, d)


def setup_inputs(seed: int = 0) -> dict:
    key = jax.random.key(seed)
    ks = iter(jax.random.split(key, 32))
    f32 = jnp.float32
    out_scale = (2.0 * DEPTH) ** -0.5

    def w(shape, fan_in, extra=1.0):
        return jax.random.normal(next(ks), shape, f32) * (fan_in ** -0.5) * extra

    def gain(shape):
        return 1.0 + 0.02 * jax.random.normal(next(ks), shape, f32)

    d = D_MODEL
    return {
        'x': jax.random.normal(next(ks), (BATCH, SEQ, d), f32),
        'p': jax.random.normal(next(ks), (DEPTH, BATCH, SEQ, PLE_DIM), f32),
        'a_norm': gain((N_A_LAYERS, d)),
        'a_w_in': w((N_A_LAYERS, d, 3 * d), d),
        'a_conv_w': w((N_A_LAYERS, CONV_WIDTH, d), CONV_WIDTH),
        'a_w_out': w((N_A_LAYERS, d, d), d, out_scale),
        'kv_norm': gain((d,)),
        'w_kv': w((d, 2 * N_HEADS * HEAD_DIM), d),
        'b_norm': gain((N_B_LAYERS, d)),
        'b_w_q': w((N_B_LAYERS, d, N_HEADS * HEAD_DIM), d),
        'b_w_o': w((N_B_LAYERS, N_HEADS * HEAD_DIM, d), N_HEADS * HEAD_DIM, out_scale),
        'ffn_norm': gain((DEPTH, d)),
        'dense_w_gu': w((N_DENSE, d, 2 * D_FF), d),
        'dense_w_down': w((N_DENSE, D_FF, d), D_FF, out_scale),
        'moe_w_router': w((N_MOE, d, N_EXPERTS), d),
        'moe_b_router': 0.01 * jax.random.normal(next(ks), (N_MOE, N_EXPERTS), f32),
        'moe_w_gu': w((N_MOE, N_EXPERTS, d, 2 * D_EXPERT), d),
        'moe_w_down': w((N_MOE, N_EXPERTS, D_EXPERT, d), D_EXPERT, out_scale),
        'ple_norm': gain((DEPTH, d)),
        'ple_w_up': w((DEPTH, PLE_DIM, d), PLE_DIM, out_scale),
        'ple_w_gate': w((DEPTH, d, d), d),
        'final_norm': gain((d,)),
    }


def reference(x, p, a_norm, a_w_in, a_conv_w, a_w_out, kv_norm, w_kv, b_norm, b_w_q, b_w_o,
              ffn_norm, dense_w_gu, dense_w_down, moe_w_router, moe_b_router, moe_w_gu, moe_w_down,
              ple_norm, ple_w_up, ple_w_gate, final_norm):
    h = x
    k = None
    v = None
    for i in range(DEPTH):
        if i < N_A_LAYERS:
            h = h + short_conv_mixer(rmsnorm(h, a_norm[i]), a_w_in[i], a_conv_w[i], a_w_out[i])
        else:
            j = i - N_A_LAYERS
            if j == 0:
                k_flat, v_flat = jnp.split(rmsnorm(h, kv_norm) @ w_kv, 2, axis=-1)
                k = to_heads(k_flat)
                v = to_heads(v_flat)
            q = to_heads(rmsnorm(h, b_norm[j]) @ b_w_q[j])
            h = h + from_heads(stick_breaking_attention(q, k, v)) @ b_w_o[j]
        hn = rmsnorm(h, ffn_norm[i])
        if i % 2 == 0:
            h = h + swiglu(hn, dense_w_gu[i // 2], dense_w_down[i // 2])
        else:
            m = i // 2
            h = h + moe_swiglu(hn, moe_w_router[m], moe_b_router[m], moe_w_gu[m], moe_w_down[m])
        gate = jax.nn.sigmoid(rmsnorm(h, ple_norm[i]) @ ple_w_gate[i])
        h = h + (p[i] @ ple_w_up[i]) * gate
    return rmsnorm(h, final_norm)
```

```python
import functools

import jax
import jax.numpy as jnp
from jax import lax
from jax.experimental import pallas as pl
from jax.experimental.pallas import tpu as pltpu

F32 = jnp.float32
BF16 = jnp.bfloat16

EPS = 1e-6
HEAD_DIM = 128
CONV_WIDTH = 3
TOP_K = 2
LANES = 128
BF16_SUBLANES = 16
VMEM_BUDGET_V7X = 60000 * 1024
VMEM_SLACK = 8 * 1024 * 1024
EXP_ZERO_BELOW = -110.0


def _params(semantics, block_bytes):
    limit = min(VMEM_BUDGET_V7X, int(block_bytes) + VMEM_SLACK)
    return pltpu.CompilerParams(dimension_semantics=semantics, vmem_limit_bytes=limit)


def _blk(dim, pref):
    b = min(dim, pref)
    while dim % b:
        b //= 2
    return b


def _rmsnorm_kernel(x_ref, g_ref, *o_refs):
    x = x_ref[...]
    y = x * lax.rsqrt(jnp.mean(x * x, axis=-1, keepdims=True) + EPS)
    for n, o_ref in enumerate(o_refs):
        o_ref[...] = (y * g_ref[n:n + 1, :]).astype(o_ref.dtype)


def rmsnorm(x, gains, out_dtype):
    t, d = x.shape
    g = gains.shape[0]
    bt = _blk(t, 256)
    out_bytes = jnp.dtype(out_dtype).itemsize
    outs = pl.pallas_call(
        _rmsnorm_kernel,
        grid=(t // bt,),
        in_specs=[pl.BlockSpec((bt, d), lambda i: (i, 0)),
                  pl.BlockSpec((g, d), lambda i: (0, 0))],
        out_specs=[pl.BlockSpec((bt, d), lambda i: (i, 0))] * g,
        out_shape=[jax.ShapeDtypeStruct((t, d), out_dtype)] * g,
        compiler_params=_params(("parallel",), 2 * bt * d * (4 + g * out_bytes)),
        name="rmsnorm",
    )(x, gains)
    return outs


def _mm_kernel(*refs, nk, has_resid, scale):
    if has_resid:
        x_ref, w_ref, r_ref, o_ref = refs[:4]
    else:
        x_ref, w_ref, o_ref = refs[:3]
        r_ref = None

    def finish(acc):
        if scale is not None:
            acc = acc * scale
        if r_ref is not None:
            acc = r_ref[...] + acc
        o_ref[...] = acc.astype(o_ref.dtype)

    part = jnp.dot(x_ref[...], w_ref[...], preferred_element_type=F32)
    if nk == 1:
        finish(part)
    else:
        acc_ref = refs[-1]
        k = pl.program_id(2)

        @pl.when(k == 0)
        def _():
            acc_ref[...] = part

        @pl.when(k > 0)
        def _():
            acc_ref[...] += part

        @pl.when(k == nk - 1)
        def _():
            finish(acc_ref[...])


def matmul(x, w, *, out_dtype, resid=None, scale=None, bm=1024, bn=512, bk=None):
    m, kdim = x.shape
    n = w.shape[1]
    bm, bn = _blk(m, bm), _blk(n, bn)
    bk = kdim if bk is None else bk
    nk = kdim // bk
    in_specs = [pl.BlockSpec((bm, bk), lambda i, j, k: (i, k)),
                pl.BlockSpec((bk, bn), lambda i, j, k: (k, j))]
    args = [x, w]
    vmem = 2 * (bm * bk + bk * bn) * 2 + 2 * bm * bn * jnp.dtype(out_dtype).itemsize + bm * bn * 4
    if resid is not None:
        in_specs.append(pl.BlockSpec((bm, bn), lambda i, j, k: (i, j)))
        args.append(resid)
        vmem += 2 * bm * bn * 4
    scratch = [pltpu.VMEM((bm, bn), F32)] if nk > 1 else []
    vmem += bm * bn * 4 if nk > 1 else 0
    return pl.pallas_call(
        functools.partial(_mm_kernel, nk=nk, has_resid=resid is not None, scale=scale),
        grid=(m // bm, n // bn, nk),
        in_specs=in_specs,
        out_specs=pl.BlockSpec((bm, bn), lambda i, j, k: (i, j)),
        out_shape=jax.ShapeDtypeStruct((m, n), out_dtype),
        scratch_shapes=scratch,
        compiler_params=_params(("parallel", "parallel", "arbitrary"), vmem),
        name="matmul",
    )(*args)


def _glu_kernel(x_ref, wg_ref, wu_ref, o_ref):
    x = x_ref[...]
    g = jnp.dot(x, wg_ref[...], preferred_element_type=F32)
    u = jnp.dot(x, wu_ref[...], preferred_element_type=F32)
    o_ref[...] = (g * jax.nn.sigmoid(g) * u).astype(o_ref.dtype)


def glu(x, w_gu, *, bm=1024, bn=256):
    m, kdim = x.shape
    f = w_gu.shape[1] // 2
    bm, bn = _blk(m, bm), _blk(f, bn)
    nf = f // bn
    vmem = 2 * (bm * kdim + 2 * kdim * bn + bm * bn) * 2 + 3 * bm * bn * 4
    return pl.pallas_call(
        _glu_kernel,
        grid=(m // bm, nf),
        in_specs=[pl.BlockSpec((bm, kdim), lambda i, j: (i, 0)),
                  pl.BlockSpec((kdim, bn), lambda i, j: (0, j)),
                  pl.BlockSpec((kdim, bn), lambda i, j: (0, j + nf))],
        out_specs=pl.BlockSpec((bm, bn), lambda i, j: (i, j)),
        out_shape=jax.ShapeDtypeStruct((m, f), BF16),
        compiler_params=_params(("parallel", "parallel"), vmem),
        name="glu",
    )(x, w_gu, w_gu)


def _conv_kernel(b_ref, c_ref, u_ref, ch_ref, uh_ref, w_ref, o_ref, *, blocks_per_seq):
    ts = c_ref.shape[0]
    cu = c_ref[...].astype(F32) * u_ref[...].astype(F32)
    seq_start = (pl.program_id(0) % blocks_per_seq) == 0
    halo = ch_ref[...].astype(F32) * uh_ref[...].astype(F32)
    halo = jnp.where(seq_start, 0.0, halo)
    row = lax.broadcasted_iota(jnp.int32, cu.shape, 0)
    h1 = halo[BF16_SUBLANES - 1:BF16_SUBLANES, :]
    h2 = halo[BF16_SUBLANES - 2:BF16_SUBLANES - 1, :]
    prev1 = jnp.where(row == 0, h1, pltpu.roll(cu, 1, 0))
    prev2 = jnp.where(row == 0, h2, jnp.where(row == 1, h1, pltpu.roll(cu, 2, 0)))
    y = w_ref[0:1, :] * prev2 + w_ref[1:2, :] * prev1 + w_ref[2:3, :] * cu
    o_ref[...] = (b_ref[...].astype(F32) * y).astype(o_ref.dtype)


def short_conv(bcx, conv_w, seq):
    t, d3 = bcx.shape
    d = d3 // 3
    ts = _blk(seq, 512)
    dblk = _blk(d, 1024)
    nd = d // dblk
    hb = ts // BF16_SUBLANES

    def halo_map(off):
        return lambda i, j: (jnp.maximum(i * hb - 1, 0), j + off)

    vmem = 2 * (4 * ts * dblk * 2 + 2 * BF16_SUBLANES * dblk * 2) + 6 * ts * dblk * 4
    return pl.pallas_call(
        functools.partial(_conv_kernel, blocks_per_seq=seq // ts),
        grid=(t // ts, nd),
        in_specs=[pl.BlockSpec((ts, dblk), lambda i, j: (i, j)),
                  pl.BlockSpec((ts, dblk), lambda i, j: (i, j + nd)),
                  pl.BlockSpec((ts, dblk), lambda i, j: (i, j + 2 * nd)),
                  pl.BlockSpec((BF16_SUBLANES, dblk), halo_map(nd)),
                  pl.BlockSpec((BF16_SUBLANES, dblk), halo_map(2 * nd)),
                  pl.BlockSpec((CONV_WIDTH, dblk), lambda i, j: (0, j))],
        out_specs=pl.BlockSpec((ts, dblk), lambda i, j: (i, j)),
        out_shape=jax.ShapeDtypeStruct((t, d), BF16),
        compiler_params=_params(("parallel", "parallel"), vmem),
        name="short_conv",
    )(bcx, bcx, bcx, bcx, bcx, conv_w)


def _ple_kernel(hn_ref, wg_ref, p_ref, wu_ref, h_ref, o_ref):
    gate = jax.nn.sigmoid(jnp.dot(hn_ref[...], wg_ref[...], preferred_element_type=F32))
    up = jnp.dot(p_ref[...].astype(BF16), wu_ref[...], preferred_element_type=F32)
    o_ref[...] = h_ref[...] + up * gate


def ple(hn, w_gate, p, w_up, h, *, bm=1024, bn=512):
    m, d = hn.shape
    pd = p.shape[1]
    n = w_gate.shape[1]
    bm, bn = _blk(m, bm), _blk(n, bn)
    vmem = 2 * (bm * d * 2 + d * bn * 2 + bm * pd * 4 + pd * bn * 2 + 2 * bm * bn * 4) + 3 * bm * bn * 4
    return pl.pallas_call(
        _ple_kernel,
        grid=(m // bm, n // bn),
        in_specs=[pl.BlockSpec((bm, d), lambda i, j: (i, 0)),
                  pl.BlockSpec((d, bn), lambda i, j: (0, j)),
                  pl.BlockSpec((bm, pd), lambda i, j: (i, 0)),
                  pl.BlockSpec((pd, bn), lambda i, j: (0, j)),
                  pl.BlockSpec((bm, bn), lambda i, j: (i, j))],
        out_specs=pl.BlockSpec((bm, bn), lambda i, j: (i, j)),
        out_shape=jax.ShapeDtypeStruct((m, n), F32),
        compiler_params=_params(("parallel", "parallel"), vmem),
        name="ple",
    )(hn, w_gate, p, w_up, h)


def _attn_kernel(q_ref, k_ref, v_ref, tri_ref, o_ref, acc_ref, c_ref, *, tq, tk):
    q = q_ref[...]
    q_start = pl.program_id(2) * tq
    row = q_start + lax.broadcasted_iota(jnp.int32, (tq, tk), 0)
    col = lax.broadcasted_iota(jnp.int32, (tq, tk), 1)
    acc_ref[...] = jnp.zeros_like(acc_ref)
    c_ref[...] = jnp.zeros_like(c_ref)

    def body(carry):
        kb, _ = carry
        k_start = pl.multiple_of(kb * tk, tk)
        kblk = k_ref[pl.ds(k_start, tk), :]
        vblk = v_ref[pl.ds(k_start, tk), :]
        z = lax.dot_general(q, kblk, (((1,), (1,)), ((), ())), preferred_element_type=F32)
        strict = (k_start + col) < row
        softplus = jnp.maximum(z, 0.0) + jnp.log(1.0 + jnp.exp(-jnp.abs(z)))
        log_keep = jnp.where(strict, -softplus, 0.0)
        hi = log_keep.astype(BF16)
        lo = (log_keep - hi.astype(F32)).astype(BF16)
        tri = tri_ref[...]
        suffix = (jnp.dot(hi, tri, preferred_element_type=F32)
                  + jnp.dot(lo, tri, preferred_element_type=F32))
        c = c_ref[...]
        w = jnp.where(strict, jnp.exp(z + log_keep + suffix + c), 0.0)
        acc_ref[...] += jnp.dot(w.astype(BF16), vblk, preferred_element_type=F32)
        c_new = c + suffix[:, 0:1] + log_keep[:, 0:1]
        c_ref[...] = c_new
        alive = (jnp.max(c_new) > EXP_ZERO_BELOW).astype(jnp.int32)
        return kb - 1, alive

    def cond(carry):
        kb, alive = carry
        return jnp.logical_and(kb >= 0, alive > 0)

    last_kb = (q_start + tq) // tk - 1
    lax.while_loop(cond, body, (last_kb, jnp.int32(1)))
    o_ref[...] = acc_ref[...].astype(o_ref.dtype)


def stick_breaking_attention(q, kv, batch, seq, *, tq=512, tk=256):
    t, hd = q.shape
    n_heads = hd // HEAD_DIM
    tq, tk = _blk(seq, tq), _blk(seq, tk)
    tk = min(tk, tq)
    nq = seq // tq
    tri = (lax.broadcasted_iota(jnp.int32, (tk, tk), 0)
           > lax.broadcasted_iota(jnp.int32, (tk, tk), 1)).astype(BF16)
    vmem = 2 * (2 * tq * HEAD_DIM * 2 + 2 * seq * HEAD_DIM * 2 + tk * tk * 2) + 12 * tq * tk * 4
    return pl.pallas_call(
        functools.partial(_attn_kernel, tq=tq, tk=tk),
        grid=(batch, n_heads, nq),
        in_specs=[pl.BlockSpec((tq, HEAD_DIM), lambda b, h, i: (b * nq + i, h)),
                  pl.BlockSpec((seq, HEAD_DIM), lambda b, h, i: (b, h)),
                  pl.BlockSpec((seq, HEAD_DIM), lambda b, h, i: (b, n_heads + h)),
                  pl.BlockSpec((tk, tk), lambda b, h, i: (0, 0))],
        out_specs=pl.BlockSpec((tq, HEAD_DIM), lambda b, h, i: (b * nq + i, h)),
        out_shape=jax.ShapeDtypeStruct((t, hd), BF16),
        scratch_shapes=[pltpu.VMEM((tq, HEAD_DIM), F32), pltpu.VMEM((tq, 1), F32)],
        compiler_params=_params(("parallel", "parallel", "parallel"), vmem),
        name="stick_breaking_attention",
    )(q, kv, kv, tri)


INFO_IDX1, INFO_IDX2, INFO_RANK1, INFO_RANK2, INFO_W1, INFO_W2 = range(6)


def _router_kernel(h_ref, g_ref, wr_ref, br_ref, tri_ref, hn_ref, info_ref, cnt_ref, count_ref):
    @pl.when(pl.program_id(0) == 0)
    def _():
        count_ref[...] = jnp.zeros_like(count_ref)

    x = h_ref[...]
    hn = x * lax.rsqrt(jnp.mean(x * x, axis=-1, keepdims=True) + EPS) * g_ref[...]
    hn_ref[...] = hn
    logits = jnp.dot(hn, wr_ref[...], precision=lax.Precision.HIGHEST,
                     preferred_element_type=F32) + br_ref[...]
    lane = lax.broadcasted_iota(jnp.int32, logits.shape, 1)
    m1 = jnp.max(logits, axis=1, keepdims=True)
    i1 = jnp.min(jnp.where(logits == m1, lane, LANES), axis=1, keepdims=True)
    rest = jnp.where(lane == i1, -jnp.inf, logits)
    m2 = jnp.max(rest, axis=1, keepdims=True)
    i2 = jnp.min(jnp.where(rest == m2, lane, LANES), axis=1, keepdims=True)
    e2 = jnp.exp(m2 - m1)
    w1 = 1.0 / (1.0 + e2)
    w2 = e2 / (1.0 + e2)
    sel1 = lane == i1
    sel2 = lane == i2
    sel = jnp.logical_or(sel1, sel2).astype(F32)
    before = jnp.dot(tri_ref[...], sel.astype(BF16), preferred_element_type=F32) + count_ref[0:1, :]
    rank1 = jnp.sum(jnp.where(sel1, before, 0.0), axis=1, keepdims=True)
    rank2 = jnp.sum(jnp.where(sel2, before, 0.0), axis=1, keepdims=True)
    count_ref[...] = count_ref[...] + jnp.sum(sel, axis=0, keepdims=True)
    cnt_ref[...] = count_ref[...]
    info = jnp.zeros(logits.shape, F32)
    for slot, val in ((INFO_IDX1, i1.astype(F32)), (INFO_IDX2, i2.astype(F32)), (INFO_RANK1, rank1),
                      (INFO_RANK2, rank2), (INFO_W1, w1), (INFO_W2, w2)):
        info = jnp.where(lane == slot, val, info)
    info_ref[...] = info


def route(h, gain, w_router, b_router, *, bt=512):
    t, d = h.shape
    n_exp = w_router.shape[1]
    bt = _blk(t, bt)
    wr = jnp.zeros((d, LANES), F32).at[:, :n_exp].set(w_router)
    br = jnp.full((1, LANES), -1e30, F32).at[0, :n_exp].set(b_router)
    tri = (lax.broadcasted_iota(jnp.int32, (bt, bt), 1)
           < lax.broadcasted_iota(jnp.int32, (bt, bt), 0)).astype(BF16)
    vmem = 2 * (2 * bt * d * 4 + d * LANES * 4 + bt * bt * 2 + bt * LANES * 4) + 4 * bt * d * 4
    hn, info, counts = pl.pallas_call(
        _router_kernel,
        grid=(t // bt,),
        in_specs=[pl.BlockSpec((bt, d), lambda i: (i, 0)),
                  pl.BlockSpec((1, d), lambda i: (0, 0)),
                  pl.BlockSpec((d, LANES), lambda i: (0, 0)),
                  pl.BlockSpec((1, LANES), lambda i: (0, 0)),
                  pl.BlockSpec((bt, bt), lambda i: (0, 0))],
        out_specs=[pl.BlockSpec((bt, d), lambda i: (i, 0)),
                   pl.BlockSpec((bt, LANES), lambda i: (i, 0)),
                   pl.BlockSpec((8, LANES), lambda i: (0, 0))],
        out_shape=[jax.ShapeDtypeStruct((t, d), F32),
                   jax.ShapeDtypeStruct((t, LANES), F32),
                   jax.ShapeDtypeStruct((8, LANES), F32)],
        scratch_shapes=[pltpu.VMEM((8, LANES), F32)],
        compiler_params=_params(("arbitrary",), vmem),
        name="moe_router",
    )(h, gain, wr, br, tri)
    return hn, info, counts[0, :n_exp]


def _scatter_rows_kernel(pos_ref, src_ref, dst_in_ref, dst_ref, sem, *, tc):
    del dst_in_ref
    base = pl.program_id(0) * tc

    def copy(j, slot):
        return pltpu.make_async_copy(src_ref.at[pl.ds(base + j, 1)],
                                     dst_ref.at[pl.ds(pos_ref[slot, j], 1)], sem)

    def issue(j, _):
        copy(j, 0).start()
        copy(j, 1).start()
        return 0

    def drain(j, _):
        copy(j, 0).wait()
        copy(j, 1).wait()
        return 0

    lax.fori_loop(0, tc, issue, 0)
    lax.fori_loop(0, tc, drain, 0)


def scatter_rows(src, pos, dst_init, *, tc=2048):
    t, d = src.shape
    tc = _blk(t, tc)
    return pl.pallas_call(
        functools.partial(_scatter_rows_kernel, tc=tc),
        grid=(t // tc,),
        in_specs=[pl.BlockSpec((TOP_K, tc), lambda i: (0, i), memory_space=pltpu.SMEM),
                  pl.BlockSpec(memory_space=pl.ANY),
                  pl.BlockSpec(memory_space=pl.ANY)],
        out_specs=pl.BlockSpec(memory_space=pl.ANY),
        out_shape=jax.ShapeDtypeStruct(dst_init.shape, dst_init.dtype),
        scratch_shapes=[pltpu.SemaphoreType.DMA(())],
        input_output_aliases={2: 0},
        compiler_params=pltpu.CompilerParams(dimension_semantics=("arbitrary",)),
        name="moe_scatter_rows",
    )(pos, src, dst_init)


def _combine_kernel(pos_ref, y_ref, w_ref, h_ref, o_ref, buf_ref, sem, *, tc):
    def copy(j, slot):
        return pltpu.make_async_copy(y_ref.at[pl.ds(pos_ref[slot, j], 1)],
                                     buf_ref.at[slot, pl.ds(j, 1)], sem)

    def issue(j, _):
        copy(j, 0).start()
        copy(j, 1).start()
        return 0

    def drain(j, _):
        copy(j, 0).wait()
        copy(j, 1).wait()
        return 0

    lax.fori_loop(0, tc, issue, 0)
    lax.fori_loop(0, tc, drain, 0)
    w = w_ref[...]
    o_ref[...] = (h_ref[...] + w[:, INFO_W1:INFO_W1 + 1] * buf_ref[0]
                  + w[:, INFO_W2:INFO_W2 + 1] * buf_ref[1])


def combine(y_sorted, pos, info, h, *, tc=256):
    t, d = h.shape
    tc = _blk(t, tc)
    vmem = 2 * tc * d * 4 + 2 * (2 * tc * d * 4 + tc * LANES * 4) + 2 * tc * d * 4
    return pl.pallas_call(
        functools.partial(_combine_kernel, tc=tc),
        grid=(t // tc,),
        in_specs=[pl.BlockSpec((TOP_K, tc), lambda i: (0, i), memory_space=pltpu.SMEM),
                  pl.BlockSpec(memory_space=pl.ANY),
                  pl.BlockSpec((tc, LANES), lambda i: (i, 0)),
                  pl.BlockSpec((tc, d), lambda i: (i, 0))],
        out_specs=pl.BlockSpec((tc, d), lambda i: (i, 0)),
        out_shape=jax.ShapeDtypeStruct((t, d), F32),
        scratch_shapes=[pltpu.VMEM((TOP_K, tc, d), F32), pltpu.SemaphoreType.DMA(())],
        compiler_params=_params(("arbitrary",), vmem),
        name="moe_combine",
    )(pos, y_sorted, info, h)


def _expert_glu_kernel(be_ref, nvalid_ref, x_ref, wg_ref, wu_ref, o_ref, xb_ref):
    del be_ref
    i, j = pl.program_id(0), pl.program_id(1)

    @pl.when(i < nvalid_ref[0])
    def _():
        @pl.when(j == 0)
        def _():
            xb_ref[...] = x_ref[...].astype(BF16)

        x = xb_ref[...]
        g = jnp.dot(x, wg_ref[...], preferred_element_type=F32)
        u = jnp.dot(x, wu_ref[...], preferred_element_type=F32)
        o_ref[...] = (g * jax.nn.sigmoid(g) * u).astype(o_ref.dtype)

    @pl.when(i >= nvalid_ref[0])
    def _():
        o_ref[...] = jnp.zeros_like(o_ref)


def expert_glu(x_sorted, w_gu, block_expert, n_valid, *, bm, bn=512):
    r, d = x_sorted.shape
    f = w_gu.shape[2] // 2
    bn = _blk(f, bn)
    nf = f // bn
    vmem = 2 * (bm * d * 4 + 2 * d * bn * 2 + bm * bn * 2) + bm * d * 2 + 3 * bm * bn * 4
    return pl.pallas_call(
        _expert_glu_kernel,
        grid_spec=pltpu.PrefetchScalarGridSpec(
            num_scalar_prefetch=2,
            grid=(r // bm, nf),
            in_specs=[pl.BlockSpec((bm, d), lambda i, j, be, nv: (i, 0)),
                      pl.BlockSpec((None, d, bn), lambda i, j, be, nv: (be[i], 0, j)),
                      pl.BlockSpec((None, d, bn), lambda i, j, be, nv: (be[i], 0, j + nf))],
            out_specs=pl.BlockSpec((bm, bn), lambda i, j, be, nv: (i, j)),
            scratch_shapes=[pltpu.VMEM((bm, d), BF16)]),
        out_shape=jax.ShapeDtypeStruct((r, f), BF16),
        compiler_params=_params(("arbitrary", "arbitrary"), vmem),
        name="moe_expert_glu",
    )(block_expert, n_valid, x_sorted, w_gu, w_gu)


def _expert_down_kernel(be_ref, nvalid_ref, x_ref, w_ref, o_ref):
    del be_ref

    @pl.when(pl.program_id(0) < nvalid_ref[0])
    def _():
        o_ref[...] = jnp.dot(x_ref[...], w_ref[...], preferred_element_type=F32)

    @pl.when(pl.program_id(0) >= nvalid_ref[0])
    def _():
        o_ref[...] = jnp.zeros_like(o_ref)


def expert_down(a_sorted, w_down, block_expert, n_valid, *, bm, bn=1024):
    r, f = a_sorted.shape
    d = w_down.shape[2]
    bn = _blk(d, bn)
    vmem = 2 * (bm * f * 2 + f * bn * 2 + bm * bn * 4) + bm * bn * 4
    return pl.pallas_call(
        _expert_down_kernel,
        grid_spec=pltpu.PrefetchScalarGridSpec(
            num_scalar_prefetch=2,
            grid=(r // bm, d // bn),
            in_specs=[pl.BlockSpec((bm, f), lambda i, j, be, nv: (i, 0)),
                      pl.BlockSpec((None, f, bn), lambda i, j, be, nv: (be[i], 0, j))],
            out_specs=pl.BlockSpec((bm, bn), lambda i, j, be, nv: (i, j))),
        out_shape=jax.ShapeDtypeStruct((r, d), F32),
        compiler_params=_params(("arbitrary", "arbitrary"), vmem),
        name="moe_expert_down",
    )(block_expert, n_valid, a_sorted, w_down)


def moe(h, gain, w_router, b_router, w_gu, w_down, *, bm=512):
    t, d = h.shape
    n_exp = w_router.shape[1]
    bm = _blk(t, bm)
    hn, info, counts = route(h, gain, w_router, b_router)
    counts = counts.astype(jnp.int32)
    blocks_per_expert = (counts + bm - 1) // bm
    block_end = jnp.cumsum(blocks_per_expert)
    row_offset = (block_end - blocks_per_expert) * bm
    n_blocks = TOP_K * t // bm + n_exp
    n_valid = block_end[-1:]
    block_ids = jnp.arange(n_blocks, dtype=jnp.int32)
    block_expert = jnp.searchsorted(block_end, jnp.minimum(block_ids, n_valid[0] - 1), side="right")
    block_expert = block_expert.astype(jnp.int32)
    idx = info[:, INFO_IDX1:INFO_IDX2 + 1].astype(jnp.int32)
    rank = info[:, INFO_RANK1:INFO_RANK2 + 1].astype(jnp.int32)
    pos = (row_offset[idx] + rank).T
    x_sorted = scatter_rows(hn, pos, jnp.zeros((n_blocks * bm, d), F32))
    a_sorted = expert_glu(x_sorted, w_gu, block_expert, n_valid, bm=bm)
    y_sorted = expert_down(a_sorted, w_down, block_expert, n_valid, bm=bm)
    return combine(y_sorted, pos, info, h)


def kernel(x, p, a_norm, a_w_in, a_conv_w, a_w_out, kv_norm, w_kv, b_norm, b_w_q, b_w_o, ffn_norm, dense_w_gu, dense_w_down, moe_w_router, moe_b_router, moe_w_gu, moe_w_down, ple_norm, ple_w_up, ple_w_gate, final_norm):
    batch, seq, d = x.shape
    depth = p.shape[0]
    n_a = a_norm.shape[0]
    t = batch * seq
    h = x.reshape(t, d)
    p = p.reshape(depth, t, p.shape[-1])
    bf = lambda w: w.astype(BF16)
    kv = None
    for i in range(depth):
        if i < n_a:
            (hn,) = rmsnorm(h, a_norm[i:i + 1], BF16)
            bcx = matmul(hn, bf(a_w_in[i]), out_dtype=BF16)
            gated = short_conv(bcx, a_conv_w[i], seq)
            h = matmul(gated, bf(a_w_out[i]), out_dtype=F32, resid=h)
        else:
            j = i - n_a
            if j == 0:
                kvn, qn = rmsnorm(h, jnp.stack([kv_norm, b_norm[j]]), BF16)
                kv = matmul(kvn, bf(w_kv), out_dtype=BF16)
            else:
                (qn,) = rmsnorm(h, b_norm[j:j + 1], BF16)
            q = matmul(qn, bf(b_w_q[j]), out_dtype=BF16, scale=HEAD_DIM ** -0.5)
            attn = stick_breaking_attention(q, kv, batch, seq)
            h = matmul(attn, bf(b_w_o[j]), out_dtype=F32, resid=h)
        if i % 2 == 0:
            (hn,) = rmsnorm(h, ffn_norm[i:i + 1], BF16)
            act = glu(hn, bf(dense_w_gu[i // 2]))
            f = act.shape[1]
            h = matmul(act, bf(dense_w_down[i // 2]), out_dtype=F32, resid=h,
                       bk=f // 2 if (f // 2) % LANES == 0 else None)
        else:
            m = i // 2
            h = moe(h, ffn_norm[i:i + 1], moe_w_router[m], moe_b_router[m],
                    bf(moe_w_gu[m]), bf(moe_w_down[m]))
        (hn,) = rmsnorm(h, ple_norm[i:i + 1], BF16)
        h = ple(hn, bf(ple_w_gate[i]), p[i], bf(ple_w_up[i]), h)
    (out,) = rmsnorm(h, final_norm[None, :], F32)
    return out.reshape(batch, seq, d)
```

```python
import functools

import jax
import jax.numpy as jnp
from jax import lax
from jax.experimental import pallas as pl
from jax.experimental.pallas import tpu as pltpu

F32 = jnp.float32
BF16 = jnp.bfloat16

EPS = 1e-6
HEAD_DIM = 128
CONV_WIDTH = 3
TOP_K = 2
LANES = 128
BF16_SUBLANES = 16
VMEM_BUDGET_V7X = 60000 * 1024
VMEM_SLACK = 8 * 1024 * 1024
LOG2E = 1.4426950408889634
EXP_ZERO_BELOW = -110.0
MASKED_SCORE = -1e30


def _params(semantics, block_bytes):
    limit = min(VMEM_BUDGET_V7X, int(block_bytes) + VMEM_SLACK)
    return pltpu.CompilerParams(dimension_semantics=semantics, vmem_limit_bytes=limit)


def _blk(dim, pref):
    b = min(dim, pref)
    while dim % b:
        b //= 2
    return b


def _rmsnorm_kernel(x_ref, g_ref, *o_refs):
    x = x_ref[...]
    y = x * lax.rsqrt(jnp.mean(x * x, axis=-1, keepdims=True) + EPS)
    for n, o_ref in enumerate(o_refs):
        o_ref[...] = (y * g_ref[n:n + 1, :]).astype(o_ref.dtype)


def rmsnorm(x, gains, out_dtype):
    t, d = x.shape
    g = gains.shape[0]
    bt = _blk(t, 256)
    out_bytes = jnp.dtype(out_dtype).itemsize
    outs = pl.pallas_call(
        _rmsnorm_kernel,
        grid=(t // bt,),
        in_specs=[pl.BlockSpec((bt, d), lambda i: (i, 0)),
                  pl.BlockSpec((g, d), lambda i: (0, 0))],
        out_specs=[pl.BlockSpec((bt, d), lambda i: (i, 0))] * g,
        out_shape=[jax.ShapeDtypeStruct((t, d), out_dtype)] * g,
        compiler_params=_params(("parallel",), 2 * bt * d * (4 + g * out_bytes)),
        name="rmsnorm",
    )(x, gains)
    return outs


def _mm_kernel(*refs, nk, has_resid, scale):
    if has_resid:
        x_ref, w_ref, r_ref, o_ref = refs[:4]
    else:
        x_ref, w_ref, o_ref = refs[:3]
        r_ref = None

    def finish(acc):
        if scale is not None:
            acc = acc * scale
        if r_ref is not None:
            acc = r_ref[...] + acc
        o_ref[...] = acc.astype(o_ref.dtype)

    part = jnp.dot(x_ref[...], w_ref[...], preferred_element_type=F32)
    if nk == 1:
        finish(part)
    else:
        acc_ref = refs[-1]
        k = pl.program_id(2)

        @pl.when(k == 0)
        def _():
            acc_ref[...] = part

        @pl.when(k > 0)
        def _():
            acc_ref[...] += part

        @pl.when(k == nk - 1)
        def _():
            finish(acc_ref[...])


def matmul(x, w, *, out_dtype, resid=None, scale=None, bm=1024, bn=512, bk=None):
    m, kdim = x.shape
    n = w.shape[1]
    bm, bn = _blk(m, bm), _blk(n, bn)
    bk = kdim if bk is None else bk
    nk = kdim // bk
    in_specs = [pl.BlockSpec((bm, bk), lambda i, j, k: (i, k)),
                pl.BlockSpec((bk, bn), lambda i, j, k: (k, j))]
    args = [x, w]
    vmem = 2 * (bm * bk + bk * bn) * 2 + 2 * bm * bn * jnp.dtype(out_dtype).itemsize + bm * bn * 4
    if resid is not None:
        in_specs.append(pl.BlockSpec((bm, bn), lambda i, j, k: (i, j)))
        args.append(resid)
        vmem += 2 * bm * bn * 4
    scratch = [pltpu.VMEM((bm, bn), F32)] if nk > 1 else []
    vmem += bm * bn * 4 if nk > 1 else 0
    return pl.pallas_call(
        functools.partial(_mm_kernel, nk=nk, has_resid=resid is not None, scale=scale),
        grid=(m // bm, n // bn, nk),
        in_specs=in_specs,
        out_specs=pl.BlockSpec((bm, bn), lambda i, j, k: (i, j)),
        out_shape=jax.ShapeDtypeStruct((m, n), out_dtype),
        scratch_shapes=scratch,
        compiler_params=_params(("parallel", "parallel", "arbitrary"), vmem),
        name="matmul",
    )(*args)


def _glu_kernel(x_ref, wg_ref, wu_ref, o_ref):
    x = x_ref[...]
    g = jnp.dot(x, wg_ref[...], preferred_element_type=F32)
    u = jnp.dot(x, wu_ref[...], preferred_element_type=F32)
    o_ref[...] = (g * jax.nn.sigmoid(g) * u).astype(o_ref.dtype)


def glu(x, w_gu, *, bm=1024, bn=256):
    m, kdim = x.shape
    f = w_gu.shape[1] // 2
    bm, bn = _blk(m, bm), _blk(f, bn)
    nf = f // bn
    vmem = 2 * (bm * kdim + 2 * kdim * bn + bm * bn) * 2 + 3 * bm * bn * 4
    return pl.pallas_call(
        _glu_kernel,
        grid=(m // bm, nf),
        in_specs=[pl.BlockSpec((bm, kdim), lambda i, j: (i, 0)),
                  pl.BlockSpec((kdim, bn), lambda i, j: (0, j)),
                  pl.BlockSpec((kdim, bn), lambda i, j: (0, j + nf))],
        out_specs=pl.BlockSpec((bm, bn), lambda i, j: (i, j)),
        out_shape=jax.ShapeDtypeStruct((m, f), BF16),
        compiler_params=_params(("parallel", "parallel"), vmem),
        name="glu",
    )(x, w_gu, w_gu)


def _conv_kernel(b_ref, c_ref, u_ref, ch_ref, uh_ref, w_ref, o_ref, *, blocks_per_seq):
    ts = c_ref.shape[0]
    cu = c_ref[...].astype(F32) * u_ref[...].astype(F32)
    seq_start = (pl.program_id(0) % blocks_per_seq) == 0
    halo = ch_ref[...].astype(F32) * uh_ref[...].astype(F32)
    halo = jnp.where(seq_start, 0.0, halo)
    row = lax.broadcasted_iota(jnp.int32, cu.shape, 0)
    h1 = halo[BF16_SUBLANES - 1:BF16_SUBLANES, :]
    h2 = halo[BF16_SUBLANES - 2:BF16_SUBLANES - 1, :]
    prev1 = jnp.where(row == 0, h1, pltpu.roll(cu, 1, 0))
    prev2 = jnp.where(row == 0, h2, jnp.where(row == 1, h1, pltpu.roll(cu, 2, 0)))
    y = w_ref[0:1, :] * prev2 + w_ref[1:2, :] * prev1 + w_ref[2:3, :] * cu
    o_ref[...] = (b_ref[...].astype(F32) * y).astype(o_ref.dtype)


def short_conv(bcx, conv_w, seq):
    t, d3 = bcx.shape
    d = d3 // 3
    ts = _blk(seq, 512)
    dblk = _blk(d, 1024)
    nd = d // dblk
    hb = ts // BF16_SUBLANES

    def halo_map(off):
        return lambda i, j: (jnp.maximum(i * hb - 1, 0), j + off)

    vmem = 2 * (4 * ts * dblk * 2 + 2 * BF16_SUBLANES * dblk * 2) + 6 * ts * dblk * 4
    return pl.pallas_call(
        functools.partial(_conv_kernel, blocks_per_seq=seq // ts),
        grid=(t // ts, nd),
        in_specs=[pl.BlockSpec((ts, dblk), lambda i, j: (i, j)),
                  pl.BlockSpec((ts, dblk), lambda i, j: (i, j + nd)),
                  pl.BlockSpec((ts, dblk), lambda i, j: (i, j + 2 * nd)),
                  pl.BlockSpec((BF16_SUBLANES, dblk), halo_map(nd)),
                  pl.BlockSpec((BF16_SUBLANES, dblk), halo_map(2 * nd)),
                  pl.BlockSpec((CONV_WIDTH, dblk), lambda i, j: (0, j))],
        out_specs=pl.BlockSpec((ts, dblk), lambda i, j: (i, j)),
        out_shape=jax.ShapeDtypeStruct((t, d), BF16),
        compiler_params=_params(("parallel", "parallel"), vmem),
        name="short_conv",
    )(bcx, bcx, bcx, bcx, bcx, conv_w)


def _ple_kernel(hn_ref, wg_ref, p_ref, wu_ref, h_ref, o_ref):
    gate = jax.nn.sigmoid(jnp.dot(hn_ref[...], wg_ref[...], preferred_element_type=F32))
    up = jnp.dot(p_ref[...].astype(BF16), wu_ref[...], preferred_element_type=F32)
    o_ref[...] = h_ref[...] + up * gate


def ple(hn, w_gate, p, w_up, h, *, bm=1024, bn=512):
    m, d = hn.shape
    pd = p.shape[1]
    n = w_gate.shape[1]
    bm, bn = _blk(m, bm), _blk(n, bn)
    vmem = 2 * (bm * d * 2 + d * bn * 2 + bm * pd * 4 + pd * bn * 2 + 2 * bm * bn * 4) + 3 * bm * bn * 4
    return pl.pallas_call(
        _ple_kernel,
        grid=(m // bm, n // bn),
        in_specs=[pl.BlockSpec((bm, d), lambda i, j: (i, 0)),
                  pl.BlockSpec((d, bn), lambda i, j: (0, j)),
                  pl.BlockSpec((bm, pd), lambda i, j: (i, 0)),
                  pl.BlockSpec((pd, bn), lambda i, j: (0, j)),
                  pl.BlockSpec((bm, bn), lambda i, j: (i, j))],
        out_specs=pl.BlockSpec((bm, bn), lambda i, j: (i, j)),
        out_shape=jax.ShapeDtypeStruct((m, n), F32),
        compiler_params=_params(("parallel", "parallel"), vmem),
        name="ple",
    )(hn, w_gate, p, w_up, h)


def _attn_kernel(q_ref, k_ref, v_ref, tri_ref, bias_ref, o_ref, acc_ref, gone_ref, *, tb, seq):
    gone_limit = -EXP_ZERO_BELOW * LOG2E

    def tile(q, k_start, gone, diagonal):
        kblk = k_ref[pl.ds(k_start, tb), :]
        vblk = v_ref[pl.ds(k_start, tb), :]
        z = lax.dot_general(q, kblk, (((1,), (1,)), ((), ())), preferred_element_type=F32)
        if diagonal:
            z = z + bias_ref[...]
        sp = jnp.maximum(z, 0.0) + jnp.log(1.0 + jnp.exp2(-jnp.abs(z))) * LOG2E
        hi = sp.astype(BF16)
        lo = (sp - hi.astype(F32)).astype(BF16)
        tri = tri_ref[...]
        suffix = (jnp.dot(hi, tri, preferred_element_type=F32)
                  + jnp.dot(lo, tri, preferred_element_type=F32))
        log2_w = z - sp - suffix
        if gone is not None:
            log2_w = log2_w - gone
        pv = jnp.dot(jnp.exp2(log2_w).astype(BF16), vblk, preferred_element_type=F32)
        walked = suffix[:, 0:1] + sp[:, 0:1]
        return pv, walked if gone is None else gone + walked

    def query_block(qi, first):
        q_start = pl.multiple_of(qi * tb, tb)
        q = q_ref[pl.ds(q_start, tb), :]
        pv, gone = tile(q, q_start, None, True)
        if not first:
            pv_left, gone = tile(q, pl.multiple_of(q_start - tb, tb), gone, False)
            acc_ref[...] = pv + pv_left
            gone_ref[...] = gone

            def cond(carry):
                kb, alive = carry
                return jnp.logical_and(kb >= 0, alive > 0)

            def body(carry):
                kb, _ = carry
                pv_far, gone_far = tile(q, pl.multiple_of(kb * tb, tb), gone_ref[...], False)
                acc_ref[...] += pv_far
                gone_ref[...] = gone_far
                return kb - 1, (jnp.min(gone_far) < gone_limit).astype(jnp.int32)

            lax.while_loop(cond, body, (qi - 2, (jnp.min(gone) < gone_limit).astype(jnp.int32)))
            pv = acc_ref[...]
        o_ref[pl.ds(q_start, tb), :] = pv.astype(o_ref.dtype)

    query_block(0, True)

    def step(qi, carry):
        query_block(qi, False)
        return carry

    lax.fori_loop(1, seq // tb, step, 0)


def stick_breaking_attention(q, kv, batch, seq, *, tb=256):
    t, hd = q.shape
    n_heads = hd // HEAD_DIM
    tb = _blk(seq, tb)
    rows = lax.broadcasted_iota(jnp.int32, (tb, tb), 0)
    cols = lax.broadcasted_iota(jnp.int32, (tb, tb), 1)
    tri = (rows > cols).astype(BF16)
    bias = jnp.where(cols < rows, 0.0, MASKED_SCORE).astype(F32)
    vmem = 2 * (4 * seq * HEAD_DIM * 2 + tb * tb * 6) + 16 * tb * tb * 4
    head = lambda b, h: (b, h)
    return pl.pallas_call(
        functools.partial(_attn_kernel, tb=tb, seq=seq),
        grid=(batch, n_heads),
        in_specs=[pl.BlockSpec((seq, HEAD_DIM), head),
                  pl.BlockSpec((seq, HEAD_DIM), head),
                  pl.BlockSpec((seq, HEAD_DIM), lambda b, h: (b, n_heads + h)),
                  pl.BlockSpec((tb, tb), lambda b, h: (0, 0)),
                  pl.BlockSpec((tb, tb), lambda b, h: (0, 0))],
        out_specs=pl.BlockSpec((seq, HEAD_DIM), head),
        out_shape=jax.ShapeDtypeStruct((t, hd), BF16),
        scratch_shapes=[pltpu.VMEM((tb, HEAD_DIM), F32), pltpu.VMEM((tb, 1), F32)],
        compiler_params=_params(("parallel", "parallel"), vmem),
        name="stick_breaking_attention",
    )(q, kv, kv, tri, bias)


INFO_IDX1, INFO_IDX2, INFO_RANK1, INFO_RANK2, INFO_W1, INFO_W2 = range(6)


def _router_kernel(h_ref, g_ref, wr_ref, br_ref, tri_ref, hn_ref, info_ref, cnt_ref, count_ref):
    @pl.when(pl.program_id(0) == 0)
    def _():
        count_ref[...] = jnp.zeros_like(count_ref)

    x = h_ref[...]
    hn = x * lax.rsqrt(jnp.mean(x * x, axis=-1, keepdims=True) + EPS) * g_ref[...]
    hn_ref[...] = hn
    logits = jnp.dot(hn, wr_ref[...], precision=lax.Precision.HIGHEST,
                     preferred_element_type=F32) + br_ref[...]
    lane = lax.broadcasted_iota(jnp.int32, logits.shape, 1)
    m1 = jnp.max(logits, axis=1, keepdims=True)
    i1 = jnp.min(jnp.where(logits == m1, lane, LANES), axis=1, keepdims=True)
    rest = jnp.where(lane == i1, -jnp.inf, logits)
    m2 = jnp.max(rest, axis=1, keepdims=True)
    i2 = jnp.min(jnp.where(rest == m2, lane, LANES), axis=1, keepdims=True)
    e2 = jnp.exp(m2 - m1)
    w1 = 1.0 / (1.0 + e2)
    w2 = e2 / (1.0 + e2)
    sel1 = lane == i1
    sel2 = lane == i2
    sel = jnp.logical_or(sel1, sel2).astype(F32)
    before = jnp.dot(tri_ref[...], sel.astype(BF16), preferred_element_type=F32) + count_ref[0:1, :]
    rank1 = jnp.sum(jnp.where(sel1, before, 0.0), axis=1, keepdims=True)
    rank2 = jnp.sum(jnp.where(sel2, before, 0.0), axis=1, keepdims=True)
    count_ref[...] = count_ref[...] + jnp.sum(sel, axis=0, keepdims=True)
    cnt_ref[...] = count_ref[...]
    info = jnp.zeros(logits.shape, F32)
    for slot, val in ((INFO_IDX1, i1.astype(F32)), (INFO_IDX2, i2.astype(F32)), (INFO_RANK1, rank1),
                      (INFO_RANK2, rank2), (INFO_W1, w1), (INFO_W2, w2)):
        info = jnp.where(lane == slot, val, info)
    info_ref[...] = info


def route(h, gain, w_router, b_router, *, bt=512):
    t, d = h.shape
    n_exp = w_router.shape[1]
    bt = _blk(t, bt)
    wr = jnp.zeros((d, LANES), F32).at[:, :n_exp].set(w_router)
    br = jnp.full((1, LANES), -1e30, F32).at[0, :n_exp].set(b_router)
    tri = (lax.broadcasted_iota(jnp.int32, (bt, bt), 1)
           < lax.broadcasted_iota(jnp.int32, (bt, bt), 0)).astype(BF16)
    vmem = 2 * (2 * bt * d * 4 + d * LANES * 4 + bt * bt * 2 + bt * LANES * 4) + 4 * bt * d * 4
    hn, info, counts = pl.pallas_call(
        _router_kernel,
        grid=(t // bt,),
        in_specs=[pl.BlockSpec((bt, d), lambda i: (i, 0)),
                  pl.BlockSpec((1, d), lambda i: (0, 0)),
                  pl.BlockSpec((d, LANES), lambda i: (0, 0)),
                  pl.BlockSpec((1, LANES), lambda i: (0, 0)),
                  pl.BlockSpec((bt, bt), lambda i: (0, 0))],
        out_specs=[pl.BlockSpec((bt, d), lambda i: (i, 0)),
                   pl.BlockSpec((bt, LANES), lambda i: (i, 0)),
                   pl.BlockSpec((8, LANES), lambda i: (0, 0))],
        out_shape=[jax.ShapeDtypeStruct((t, d), F32),
                   jax.ShapeDtypeStruct((t, LANES), F32),
                   jax.ShapeDtypeStruct((8, LANES), F32)],
        scratch_shapes=[pltpu.VMEM((8, LANES), F32)],
        compiler_params=_params(("arbitrary",), vmem),
        name="moe_router",
    )(h, gain, wr, br, tri)
    return hn, info, counts[0, :n_exp]


def _gather_rows_kernel(nvalid_ref, tok_ref, src_ref, o_ref, buf_ref, sem, *, tc, rows_per_block):
    valid = pl.program_id(0) * tc < nvalid_ref[0] * rows_per_block

    def copy(j):
        return pltpu.make_async_copy(src_ref.at[pl.ds(tok_ref[0, j], 1)], buf_ref.at[pl.ds(j, 1)], sem)

    def issue(j, _):
        copy(j).start()
        return 0

    def drain(j, _):
        copy(j).wait()
        return 0

    @pl.when(valid)
    def _():
        lax.fori_loop(0, tc, issue, 0)
        lax.fori_loop(0, tc, drain, 0)
        o_ref[...] = buf_ref[...].astype(o_ref.dtype)

    @pl.when(jnp.logical_not(valid))
    def _():
        o_ref[...] = jnp.zeros_like(o_ref)


def gather_rows(src, tok_of_row, n_valid, rows_per_block, *, tc=512):
    d = src.shape[1]
    r = tok_of_row.shape[0]
    tc = _blk(r, tc)
    vmem = tc * d * 4 + 2 * tc * d * 2 + tc * d * 4
    return pl.pallas_call(
        functools.partial(_gather_rows_kernel, tc=tc, rows_per_block=rows_per_block),
        grid_spec=pltpu.PrefetchScalarGridSpec(
            num_scalar_prefetch=1,
            grid=(r // tc,),
            in_specs=[pl.BlockSpec((1, tc), lambda i, nv: (0, i), memory_space=pltpu.SMEM),
                      pl.BlockSpec(memory_space=pl.ANY)],
            out_specs=pl.BlockSpec((tc, d), lambda i, nv: (i, 0)),
            scratch_shapes=[pltpu.VMEM((tc, d), F32), pltpu.SemaphoreType.DMA(())]),
        out_shape=jax.ShapeDtypeStruct((r, d), BF16),
        compiler_params=_params(("arbitrary",), vmem),
        name="moe_gather_rows",
    )(n_valid, tok_of_row.reshape(1, r), src)


def _combine_kernel(pos_ref, y_ref, w_ref, h_ref, o_ref, buf_ref, sem, *, tc):
    def copy(j, slot):
        return pltpu.make_async_copy(y_ref.at[pl.ds(pos_ref[slot, j], 1)],
                                     buf_ref.at[slot, pl.ds(j, 1)], sem)

    def issue(j, _):
        copy(j, 0).start()
        copy(j, 1).start()
        return 0

    def drain(j, _):
        copy(j, 0).wait()
        copy(j, 1).wait()
        return 0

    lax.fori_loop(0, tc, issue, 0)
    lax.fori_loop(0, tc, drain, 0)
    w = w_ref[...]
    o_ref[...] = (h_ref[...] + w[:, INFO_W1:INFO_W1 + 1] * buf_ref[0]
                  + w[:, INFO_W2:INFO_W2 + 1] * buf_ref[1])


def combine(y_sorted, pos, info, h, *, tc=256):
    t, d = h.shape
    tc = _blk(t, tc)
    vmem = 2 * tc * d * 4 + 2 * (2 * tc * d * 4 + tc * LANES * 4) + 2 * tc * d * 4
    return pl.pallas_call(
        functools.partial(_combine_kernel, tc=tc),
        grid=(t // tc,),
        in_specs=[pl.BlockSpec((TOP_K, tc), lambda i: (0, i), memory_space=pltpu.SMEM),
                  pl.BlockSpec(memory_space=pl.ANY),
                  pl.BlockSpec((tc, LANES), lambda i: (i, 0)),
                  pl.BlockSpec((tc, d), lambda i: (i, 0))],
        out_specs=pl.BlockSpec((tc, d), lambda i: (i, 0)),
        out_shape=jax.ShapeDtypeStruct((t, d), F32),
        scratch_shapes=[pltpu.VMEM((TOP_K, tc, d), F32), pltpu.SemaphoreType.DMA(())],
        compiler_params=_params(("arbitrary",), vmem),
        name="moe_combine",
    )(pos, y_sorted, info, h)


def _expert_glu_kernel(be_ref, nvalid_ref, x_ref, wg_ref, wu_ref, o_ref):
    del be_ref

    @pl.when(pl.program_id(0) < nvalid_ref[0])
    def _():
        x = x_ref[...]
        g = jnp.dot(x, wg_ref[...], preferred_element_type=F32)
        u = jnp.dot(x, wu_ref[...], preferred_element_type=F32)
        o_ref[...] = (g * jax.nn.sigmoid(g) * u).astype(o_ref.dtype)

    @pl.when(pl.program_id(0) >= nvalid_ref[0])
    def _():
        o_ref[...] = jnp.zeros_like(o_ref)


def expert_glu(x_sorted, w_gu, block_expert, n_valid, *, bm, bn=512):
    r, d = x_sorted.shape
    f = w_gu.shape[2] // 2
    bn = _blk(f, bn)
    nf = f // bn
    vmem = 2 * (bm * d * 2 + 2 * d * bn * 2 + bm * bn * 2) + 3 * bm * bn * 4
    return pl.pallas_call(
        _expert_glu_kernel,
        grid_spec=pltpu.PrefetchScalarGridSpec(
            num_scalar_prefetch=2,
            grid=(r // bm, nf),
            in_specs=[pl.BlockSpec((bm, d), lambda i, j, be, nv: (i, 0)),
                      pl.BlockSpec((None, d, bn), lambda i, j, be, nv: (be[i], 0, j)),
                      pl.BlockSpec((None, d, bn), lambda i, j, be, nv: (be[i], 0, j + nf))],
            out_specs=pl.BlockSpec((bm, bn), lambda i, j, be, nv: (i, j))),
        out_shape=jax.ShapeDtypeStruct((r, f), BF16),
        compiler_params=_params(("arbitrary", "arbitrary"), vmem),
        name="moe_expert_glu",
    )(block_expert, n_valid, x_sorted, w_gu, w_gu)


def _expert_down_kernel(be_ref, nvalid_ref, x_ref, w_ref, o_ref):
    del be_ref

    @pl.when(pl.program_id(0) < nvalid_ref[0])
    def _():
        o_ref[...] = jnp.dot(x_ref[...], w_ref[...], preferred_element_type=F32)

    @pl.when(pl.program_id(0) >= nvalid_ref[0])
    def _():
        o_ref[...] = jnp.zeros_like(o_ref)


def expert_down(a_sorted, w_down, block_expert, n_valid, *, bm, bn=1024):
    r, f = a_sorted.shape
    d = w_down.shape[2]
    bn = _blk(d, bn)
    vmem = 2 * (bm * f * 2 + f * bn * 2 + bm * bn * 4) + bm * bn * 4
    return pl.pallas_call(
        _expert_down_kernel,
        grid_spec=pltpu.PrefetchScalarGridSpec(
            num_scalar_prefetch=2,
            grid=(r // bm, d // bn),
            in_specs=[pl.BlockSpec((bm, f), lambda i, j, be, nv: (i, 0)),
                      pl.BlockSpec((None, f, bn), lambda i, j, be, nv: (be[i], 0, j))],
            out_specs=pl.BlockSpec((bm, bn), lambda i, j, be, nv: (i, j))),
        out_shape=jax.ShapeDtypeStruct((r, d), F32),
        compiler_params=_params(("arbitrary", "arbitrary"), vmem),
        name="moe_expert_down",
    )(block_expert, n_valid, a_sorted, w_down)


def moe(h, gain, w_router, b_router, w_gu, w_down, *, bm=512):
    t, d = h.shape
    n_exp = w_router.shape[1]
    bm = _blk(t, bm)
    hn, info, counts = route(h, gain, w_router, b_router)
    counts = counts.astype(jnp.int32)
    blocks_per_expert = (counts + bm - 1) // bm
    block_end = jnp.cumsum(blocks_per_expert)
    row_offset = (block_end - blocks_per_expert) * bm
    n_blocks = TOP_K * t // bm + n_exp
    n_valid = block_end[-1:]
    block_ids = jnp.minimum(jnp.arange(n_blocks, dtype=jnp.int32), n_valid[0] - 1)
    block_expert = jnp.sum((block_ids[:, None] >= block_end[None, :]).astype(jnp.int32), axis=1)
    idx = info[:, INFO_IDX1:INFO_IDX2 + 1].astype(jnp.int32)
    rank = info[:, INFO_RANK1:INFO_RANK2 + 1].astype(jnp.int32)
    pos = (row_offset[idx] + rank).T
    token_ids = jnp.broadcast_to(jnp.arange(t, dtype=jnp.int32), (TOP_K, t))
    tok_of_row = jnp.zeros((n_blocks * bm,), jnp.int32).at[pos.reshape(-1)].set(
        token_ids.reshape(-1), unique_indices=True)
    x_sorted = gather_rows(hn, tok_of_row, n_valid, bm)
    a_sorted = expert_glu(x_sorted, w_gu, block_expert, n_valid, bm=bm)
    y_sorted = expert_down(a_sorted, w_down, block_expert, n_valid, bm=bm)
    return combine(y_sorted, pos, info, h)


def kernel(x, p, a_norm, a_w_in, a_conv_w, a_w_out, kv_norm, w_kv, b_norm, b_w_q, b_w_o, ffn_norm, dense_w_gu, dense_w_down, moe_w_router, moe_b_router, moe_w_gu, moe_w_down, ple_norm, ple_w_up, ple_w_gate, final_norm):
    batch, seq, d = x.shape
    depth = p.shape[0]
    n_a = a_norm.shape[0]
    t = batch * seq
    h = x.reshape(t, d)
    p = p.reshape(depth, t, p.shape[-1])
    bf = lambda w: w.astype(BF16)
    kv = None
    for i in range(depth):
        if i < n_a:
            (hn,) = rmsnorm(h, a_norm[i:i + 1], BF16)
            bcx = matmul(hn, bf(a_w_in[i]), out_dtype=BF16)
            gated = short_conv(bcx, a_conv_w[i], seq)
            h = matmul(gated, bf(a_w_out[i]), out_dtype=F32, resid=h)
        else:
            j = i - n_a
            if j == 0:
                kvn, qn = rmsnorm(h, jnp.stack([kv_norm, b_norm[j]]), BF16)
                kv = matmul(kvn, bf(w_kv), out_dtype=BF16)
            else:
                (qn,) = rmsnorm(h, b_norm[j:j + 1], BF16)
            q = matmul(qn, bf(b_w_q[j]), out_dtype=BF16, scale=HEAD_DIM ** -0.5 * LOG2E)
            attn = stick_breaking_attention(q, kv, batch, seq)
            h = matmul(attn, bf(b_w_o[j]), out_dtype=F32, resid=h)
        if i % 2 == 0:
            (hn,) = rmsnorm(h, ffn_norm[i:i + 1], BF16)
            act = glu(hn, bf(dense_w_gu[i // 2]))
            f = act.shape[1]
            h = matmul(act, bf(dense_w_down[i // 2]), out_dtype=F32, resid=h,
                       bk=f // 2 if (f // 2) % LANES == 0 else None)
        else:
            m = i // 2
            h = moe(h, ffn_norm[i:i + 1], moe_w_router[m], moe_b_router[m],
                    bf(moe_w_gu[m]), bf(moe_w_down[m]))
        (hn,) = rmsnorm(h, ple_norm[i:i + 1], BF16)
        h = ple(hn, bf(ple_w_gate[i]), p[i], bf(ple_w_up[i]), h)
    (out,) = rmsnorm(h, final_norm[None, :], F32)
    return out.reshape(batch, seq, d)
```

```python
import functools

import jax
import jax.numpy as jnp
from jax import lax
from jax.experimental import pallas as pl
from jax.experimental.pallas import tpu as pltpu

F32 = jnp.float32
BF16 = jnp.bfloat16

EPS = 1e-6
HEAD_DIM = 128
CONV_WIDTH = 3
TOP_K = 2
LANES = 128
BF16_SUBLANES = 16
VMEM_BUDGET_V7X = 60000 * 1024
VMEM_SLACK = 8 * 1024 * 1024
LOG2E = 1.4426950408889634
EXP_ZERO_BELOW = -110.0
MASKED_SCORE = -1e30


def _params(semantics, block_bytes):
    limit = min(VMEM_BUDGET_V7X, int(block_bytes) + VMEM_SLACK)
    return pltpu.CompilerParams(dimension_semantics=semantics, vmem_limit_bytes=limit)


def _blk(dim, pref):
    b = min(dim, pref)
    while dim % b:
        b //= 2
    return b


def _rmsnorm_kernel(x_ref, g_ref, *o_refs):
    x = x_ref[...]
    y = x * lax.rsqrt(jnp.mean(x * x, axis=-1, keepdims=True) + EPS)
    for n, o_ref in enumerate(o_refs):
        o_ref[...] = (y * g_ref[n:n + 1, :]).astype(o_ref.dtype)


def rmsnorm(x, gains, out_dtype):
    t, d = x.shape
    g = gains.shape[0]
    bt = _blk(t, 256)
    out_bytes = jnp.dtype(out_dtype).itemsize
    outs = pl.pallas_call(
        _rmsnorm_kernel,
        grid=(t // bt,),
        in_specs=[pl.BlockSpec((bt, d), lambda i: (i, 0)),
                  pl.BlockSpec((g, d), lambda i: (0, 0))],
        out_specs=[pl.BlockSpec((bt, d), lambda i: (i, 0))] * g,
        out_shape=[jax.ShapeDtypeStruct((t, d), out_dtype)] * g,
        compiler_params=_params(("parallel",), 2 * bt * d * (4 + g * out_bytes)),
        name="rmsnorm",
    )(x, gains)
    return outs


def _stream_kernel(x_ref, xb_ref, rstd_ref):
    x = x_ref[...]
    xb_ref[...] = x.astype(xb_ref.dtype)
    rstd_ref[...] = jnp.broadcast_to(lax.rsqrt(jnp.mean(x * x, axis=-1, keepdims=True) + EPS), rstd_ref.shape)


def stream_operands(x):
    t, d = x.shape
    bt = _blk(t, 256)
    return pl.pallas_call(
        _stream_kernel,
        grid=(t // bt,),
        in_specs=[pl.BlockSpec((bt, d), lambda i: (i, 0))],
        out_specs=[pl.BlockSpec((bt, d), lambda i: (i, 0)), pl.BlockSpec((bt, LANES), lambda i: (i, 0))],
        out_shape=[jax.ShapeDtypeStruct((t, d), BF16), jax.ShapeDtypeStruct((t, LANES), F32)],
        compiler_params=_params(("parallel",), 2 * bt * (d * 6 + LANES * 4) + bt * d * 4),
        name="stream_operands",
    )(x)


def _emit_stream(h, j, last_j, width, hb_ref, rstd_ref, ssq_ref):
    hb_ref[...] = h.astype(hb_ref.dtype)
    part = jnp.sum(h * h, axis=1, keepdims=True)

    @pl.when(j == 0)
    def _():
        ssq_ref[...] = part

    @pl.when(j > 0)
    def _():
        ssq_ref[...] += part

    @pl.when(j == last_j)
    def _():
        rstd_ref[...] = jnp.broadcast_to(lax.rsqrt(ssq_ref[...] * (1.0 / width) + EPS), rstd_ref.shape)


def _resid_mm_kernel(x_ref, w_ref, r_ref, o_ref, *rest, nk, nn, width, emit_norm):
    def finish(acc):
        out = r_ref[...] + acc
        o_ref[...] = out
        if emit_norm:
            _emit_stream(out, pl.program_id(1), nn - 1, width, *rest[:3])

    part = jnp.dot(x_ref[...], w_ref[...], preferred_element_type=F32)
    if nk == 1:
        finish(part)
    else:
        acc_ref = rest[-1]
        k = pl.program_id(2)

        @pl.when(k == 0)
        def _():
            acc_ref[...] = part

        @pl.when(k > 0)
        def _():
            acc_ref[...] += part

        @pl.when(k == nk - 1)
        def _():
            finish(acc_ref[...])


def resid_matmul(x, w, resid, *, emit_norm, bm=1024, bn=512, bk=None):
    m, kdim = x.shape
    n = w.shape[1]
    bm, bn = _blk(m, bm), _blk(n, bn)
    bk = kdim if bk is None else bk
    nk = kdim // bk
    nn = n // bn
    tile = pl.BlockSpec((bm, bn), lambda i, j, k: (i, j))
    rows = pl.BlockSpec((bm, LANES), lambda i, j, k: (i, 0))
    vmem = 2 * (bm * bk + bk * bn) * 2 + 4 * bm * bn * 4 + bm * bn * 4
    out_specs = [tile]
    out_shape = [jax.ShapeDtypeStruct((m, n), F32)]
    scratch = []
    if emit_norm:
        out_specs += [tile, rows]
        out_shape += [jax.ShapeDtypeStruct((m, n), BF16), jax.ShapeDtypeStruct((m, LANES), F32)]
        scratch.append(pltpu.VMEM((bm, 1), F32))
        vmem += 2 * bm * bn * 2 + 3 * bm * LANES * 4
    if nk > 1:
        scratch.append(pltpu.VMEM((bm, bn), F32))
        vmem += bm * bn * 4
    outs = pl.pallas_call(
        functools.partial(_resid_mm_kernel, nk=nk, nn=nn, width=n, emit_norm=emit_norm),
        grid=(m // bm, nn, nk),
        in_specs=[pl.BlockSpec((bm, bk), lambda i, j, k: (i, k)),
                  pl.BlockSpec((bk, bn), lambda i, j, k: (k, j)),
                  tile],
        out_specs=out_specs,
        out_shape=out_shape,
        scratch_shapes=scratch,
        compiler_params=_params(("parallel", "arbitrary" if emit_norm else "parallel", "arbitrary"), vmem),
        name="resid_matmul",
    )(x, w, resid)
    return outs if emit_norm else outs[0]


def _normed_kernel(x_ref, rs_ref, g_ref, *refs, n_w, scale):
    w_refs, o_ref, wb_refs = refs[:n_w], refs[n_w], refs[n_w + 1:]

    @pl.when(pl.program_id(1) == 0)
    def _():
        gain = g_ref[...]
        for w_ref, wb_ref in zip(w_refs, wb_refs):
            wb_ref[...] = (w_ref[...] * gain).astype(BF16)

    x = x_ref[...]
    rstd = rs_ref[:, 0:1]
    outs = [jnp.dot(x, wb_ref[...], preferred_element_type=F32) * rstd for wb_ref in wb_refs]
    if n_w == 2:
        g, u = outs
        out = g * jax.nn.sigmoid(g) * u
    else:
        out = outs[0] if scale is None else outs[0] * scale
    o_ref[...] = out.astype(o_ref.dtype)


def normed_matmul(x, rstd, gain, w, *, glu=False, scale=None, bm=1024, bn=512):
    m, kdim = x.shape
    n_w = 2 if glu else 1
    n = w.shape[1] // n_w
    bm, bn = _blk(m, bm), _blk(n, bn)
    nn = n // bn
    w_specs = [pl.BlockSpec((kdim, bn), lambda j, i, off=off: (0, j + off)) for off in range(0, n_w * nn, nn)]
    vmem = (2 * bm * kdim * 2 + n_w * kdim * bn * (2 * 4 + 2) + 2 * bm * bn * 2 + 2 * kdim * LANES * 4
            + 2 * bm * LANES * 4 + (n_w + 1) * bm * bn * 4)
    return pl.pallas_call(
        functools.partial(_normed_kernel, n_w=n_w, scale=scale),
        grid=(nn, m // bm),
        in_specs=[pl.BlockSpec((bm, kdim), lambda j, i: (i, 0)),
                  pl.BlockSpec((bm, LANES), lambda j, i: (i, 0)),
                  pl.BlockSpec((kdim, 1), lambda j, i: (0, 0))] + w_specs,
        out_specs=pl.BlockSpec((bm, bn), lambda j, i: (i, j)),
        out_shape=jax.ShapeDtypeStruct((m, n), BF16),
        scratch_shapes=[pltpu.VMEM((kdim, bn), BF16)] * n_w,
        compiler_params=_params(("parallel", "arbitrary"), vmem),
        name="normed_glu" if glu else "normed_matmul",
    )(x, rstd, gain.reshape(kdim, 1), *([w] * n_w))


def _conv_kernel(b_ref, c_ref, u_ref, ch_ref, uh_ref, w_ref, o_ref, *, blocks_per_seq):
    ts = c_ref.shape[0]
    cu = c_ref[...].astype(F32) * u_ref[...].astype(F32)
    seq_start = (pl.program_id(0) % blocks_per_seq) == 0
    halo = ch_ref[...].astype(F32) * uh_ref[...].astype(F32)
    halo = jnp.where(seq_start, 0.0, halo)
    row = lax.broadcasted_iota(jnp.int32, cu.shape, 0)
    h1 = halo[BF16_SUBLANES - 1:BF16_SUBLANES, :]
    h2 = halo[BF16_SUBLANES - 2:BF16_SUBLANES - 1, :]
    prev1 = jnp.where(row == 0, h1, pltpu.roll(cu, 1, 0))
    prev2 = jnp.where(row == 0, h2, jnp.where(row == 1, h1, pltpu.roll(cu, 2, 0)))
    y = w_ref[0:1, :] * prev2 + w_ref[1:2, :] * prev1 + w_ref[2:3, :] * cu
    o_ref[...] = (b_ref[...].astype(F32) * y).astype(o_ref.dtype)


def short_conv(bcx, conv_w, seq):
    t, d3 = bcx.shape
    d = d3 // 3
    ts = _blk(seq, 512)
    dblk = _blk(d, 1024)
    nd = d // dblk
    hb = ts // BF16_SUBLANES

    def halo_map(off):
        return lambda i, j: (jnp.maximum(i * hb - 1, 0), j + off)

    vmem = 2 * (4 * ts * dblk * 2 + 2 * BF16_SUBLANES * dblk * 2) + 6 * ts * dblk * 4
    return pl.pallas_call(
        functools.partial(_conv_kernel, blocks_per_seq=seq // ts),
        grid=(t // ts, nd),
        in_specs=[pl.BlockSpec((ts, dblk), lambda i, j: (i, j)),
                  pl.BlockSpec((ts, dblk), lambda i, j: (i, j + nd)),
                  pl.BlockSpec((ts, dblk), lambda i, j: (i, j + 2 * nd)),
                  pl.BlockSpec((BF16_SUBLANES, dblk), halo_map(nd)),
                  pl.BlockSpec((BF16_SUBLANES, dblk), halo_map(2 * nd)),
                  pl.BlockSpec((CONV_WIDTH, dblk), lambda i, j: (0, j))],
        out_specs=pl.BlockSpec((ts, dblk), lambda i, j: (i, j)),
        out_shape=jax.ShapeDtypeStruct((t, d), BF16),
        compiler_params=_params(("parallel", "parallel"), vmem),
        name="short_conv",
    )(bcx, bcx, bcx, bcx, bcx, conv_w)


def _ple_kernel(hb_ref, wg_ref, p_ref, wu_ref, h_ref, rs_ref, o_ref, *rest, nn, width):
    gate = jax.nn.sigmoid(jnp.dot(hb_ref[...], wg_ref[...], preferred_element_type=F32) * rs_ref[:, 0:1])
    up = jnp.dot(p_ref[...].astype(BF16), wu_ref[...], preferred_element_type=F32)
    out = h_ref[...] + up * gate
    o_ref[...] = out
    if rest:
        _emit_stream(out, pl.program_id(1), nn - 1, width, *rest)


def ple(hb, rstd, w_gate, p, w_up, h, *, emit_norm, bm=1024, bn=512):
    m, d = hb.shape
    pd = p.shape[1]
    n = w_gate.shape[1]
    bm, bn = _blk(m, bm), _blk(n, bn)
    nn = n // bn
    tile = pl.BlockSpec((bm, bn), lambda i, j: (i, j))
    rows = pl.BlockSpec((bm, LANES), lambda i, j: (i, 0))
    vmem = (2 * (bm * d * 2 + d * bn * 2 + bm * pd * 4 + pd * bn * 2 + 2 * bm * bn * 4 + bm * LANES * 4)
            + 3 * bm * bn * 4)
    out_specs, out_shape, scratch = [tile], [jax.ShapeDtypeStruct((m, n), F32)], []
    if emit_norm:
        out_specs += [tile, rows]
        out_shape += [jax.ShapeDtypeStruct((m, n), BF16), jax.ShapeDtypeStruct((m, LANES), F32)]
        scratch.append(pltpu.VMEM((bm, 1), F32))
        vmem += 2 * bm * bn * 2 + 3 * bm * LANES * 4
    outs = pl.pallas_call(
        functools.partial(_ple_kernel, nn=nn, width=n),
        grid=(m // bm, nn),
        in_specs=[pl.BlockSpec((bm, d), lambda i, j: (i, 0)),
                  pl.BlockSpec((d, bn), lambda i, j: (0, j)),
                  pl.BlockSpec((bm, pd), lambda i, j: (i, 0)),
                  pl.BlockSpec((pd, bn), lambda i, j: (0, j)),
                  tile, rows],
        out_specs=out_specs,
        out_shape=out_shape,
        scratch_shapes=scratch,
        compiler_params=_params(("parallel", "arbitrary" if emit_norm else "parallel"), vmem),
        name="ple",
    )(hb, w_gate, p, w_up, h, rstd)
    return outs if emit_norm else outs[0]


def _attn_kernel(q_ref, k_ref, v_ref, tri_ref, bias_ref, o_ref, acc_ref, gone_ref, *, tb, rs, group, seq):
    gone_limit = -EXP_ZERO_BELOW * LOG2E

    def scores(q, k_start, nk, bias):
        z = lax.dot_general(q, k_ref[pl.ds(k_start, nk), :], (((1,), (1,)), ((), ())),
                            preferred_element_type=F32)
        return z if bias is None else z + bias

    def split(z):
        sp = jnp.maximum(z, 0.0) + jnp.log(1.0 + jnp.exp2(-jnp.abs(z))) * LOG2E
        hi = sp.astype(BF16)
        return sp, hi, (sp - hi.astype(F32)).astype(BF16)

    def suffix_sum(hi, lo, nk):
        tri = tri_ref[0:nk, 0:nk]
        return jnp.dot(hi, tri, preferred_element_type=F32) + jnp.dot(lo, tri, preferred_element_type=F32)

    def weights(z, sp, suffix, gone):
        log2_w = z - sp - suffix
        walked = suffix[:, 0:1] + sp[:, 0:1]
        if gone is not None:
            log2_w, walked = log2_w - gone, walked + gone
        return jnp.exp2(log2_w).astype(BF16), walked

    def values(w, k_start, nk):
        return jnp.dot(w, v_ref[pl.ds(k_start, nk), :], preferred_element_type=F32)

    def tile(q, k_start, nk, gone, bias):
        z = scores(q, k_start, nk, bias)
        sp, hi, lo = split(z)
        w, gone = weights(z, sp, suffix_sum(hi, lo, nk), gone)
        return values(w, k_start, nk), gone

    def query_blocks(qis, first):
        q_starts = [pl.multiple_of(qi * tb, tb) for qi in qis]
        ns = tb // rs
        units = [(b, s, True) for b in range(len(qis)) for s in range(ns)]
        if not first:
            units += [(b, s, False) for b in range(len(qis)) for s in range(ns)]
        live = [None] * len(units)
        gone_of, pv_of = {}, {}

        def run(stage, u):
            b, s, diagonal = units[u]
            rows = slice(s * rs, (s + 1) * rs)
            nk = (s + 1) * rs if diagonal else tb
            k_start = q_starts[b] if diagonal else pl.multiple_of(q_starts[b] - tb, tb)
            if stage == 0:
                q = q_ref[pl.ds(q_starts[b] + s * rs, rs), :]
                live[u] = (scores(q, k_start, nk, bias_ref[rows, 0:nk] if diagonal else None),)
            elif stage == 1:
                (z,) = live[u]
                live[u] = (z,) + split(z)
            elif stage == 2:
                z, sp, hi, lo = live[u]
                live[u] = (z, sp, suffix_sum(hi, lo, nk))
            elif stage == 3:
                z, sp, suffix = live[u]
                w, gone_of[b, s] = weights(z, sp, suffix, gone_of.get((b, s)))
                live[u] = (w,)
            else:
                (w,) = live[u]
                live[u] = None
                pv = values(w, k_start, nk)
                if first:
                    o_ref[pl.ds(q_starts[b] + s * rs, rs), :] = pv.astype(o_ref.dtype)
                elif diagonal:
                    pv_of[b, s] = pv
                else:
                    acc_ref[b, rows, :] = pv_of.pop((b, s)) + pv
                    gone_ref[b, rows, :] = gone_of[b, s]

        n_stages = 5
        for step in range(len(units) + n_stages - 1):
            for stage in range(n_stages):
                if 0 <= step - stage < len(units):
                    run(stage, step - stage)
        if first:
            return

        for b, qi in enumerate(qis):
            def cond(carry):
                kb, alive = carry
                return jnp.logical_and(kb >= 0, alive > 0)

            def body(carry, b=b):
                kb, _ = carry
                pv_far, gone_far = tile(q_ref[pl.ds(q_starts[b], tb), :], pl.multiple_of(kb * tb, tb), tb,
                                        gone_ref[b], None)
                acc_ref[b] += pv_far
                gone_ref[b] = gone_far
                return kb - 1, (jnp.min(gone_far) < gone_limit).astype(jnp.int32)

            lax.while_loop(cond, body, (qi - 2, (jnp.min(gone_ref[b]) < gone_limit).astype(jnp.int32)))
            o_ref[pl.ds(q_starts[b], tb), :] = acc_ref[b].astype(o_ref.dtype)

    nq = seq // tb
    query_blocks([0], True)
    n_iter = (nq - 1) // group
    lead = nq - 1 - n_iter * group
    if lead:
        query_blocks(list(range(1, 1 + lead)), False)

    def step(it, carry):
        query_blocks([1 + lead + it * group + g for g in range(group)], False)
        return carry

    lax.fori_loop(0, n_iter, step, 0)


def stick_breaking_attention(q, kv, batch, seq, *, tb=256, rs=128, group=4):
    t, hd = q.shape
    n_heads = hd // HEAD_DIM
    tb = _blk(seq, tb)
    rows = lax.broadcasted_iota(jnp.int32, (tb, tb), 0)
    cols = lax.broadcasted_iota(jnp.int32, (tb, tb), 1)
    tri = (rows > cols).astype(BF16)
    bias = jnp.where(cols < rows, 0.0, MASKED_SCORE).astype(F32)
    vmem = 2 * (4 * seq * HEAD_DIM * 2 + tb * tb * 6) + 16 * tb * tb * 4
    head = lambda b, h: (b, h)
    return pl.pallas_call(
        functools.partial(_attn_kernel, tb=tb, rs=min(rs, tb), group=group, seq=seq),
        grid=(batch, n_heads),
        in_specs=[pl.BlockSpec((seq, HEAD_DIM), head),
                  pl.BlockSpec((seq, HEAD_DIM), head),
                  pl.BlockSpec((seq, HEAD_DIM), lambda b, h: (b, n_heads + h)),
                  pl.BlockSpec((tb, tb), lambda b, h: (0, 0)),
                  pl.BlockSpec((tb, tb), lambda b, h: (0, 0))],
        out_specs=pl.BlockSpec((seq, HEAD_DIM), head),
        out_shape=jax.ShapeDtypeStruct((t, hd), BF16),
        scratch_shapes=[pltpu.VMEM((group, tb, HEAD_DIM), F32), pltpu.VMEM((group, tb, 1), F32)],
        compiler_params=_params(("parallel", "parallel"), vmem),
        name="stick_breaking_attention",
    )(q, kv, kv, tri, bias)


INFO_IDX1, INFO_IDX2, INFO_RANK1, INFO_RANK2, INFO_W1, INFO_W2 = range(6)


def _router_kernel(h_ref, g_ref, wr_ref, br_ref, tri_ref, hn_ref, info_ref, cnt_ref, count_ref):
    @pl.when(pl.program_id(0) == 0)
    def _():
        count_ref[...] = jnp.zeros_like(count_ref)

    x = h_ref[...]
    hn = x * lax.rsqrt(jnp.mean(x * x, axis=-1, keepdims=True) + EPS) * g_ref[...]
    hn_ref[...] = hn
    logits = jnp.dot(hn, wr_ref[...], precision=lax.Precision.HIGHEST,
                     preferred_element_type=F32) + br_ref[...]
    lane = lax.broadcasted_iota(jnp.int32, logits.shape, 1)
    m1 = jnp.max(logits, axis=1, keepdims=True)
    i1 = jnp.min(jnp.where(logits == m1, lane, LANES), axis=1, keepdims=True)
    rest = jnp.where(lane == i1, -jnp.inf, logits)
    m2 = jnp.max(rest, axis=1, keepdims=True)
    i2 = jnp.min(jnp.where(rest == m2, lane, LANES), axis=1, keepdims=True)
    e2 = jnp.exp(m2 - m1)
    w1 = 1.0 / (1.0 + e2)
    w2 = e2 / (1.0 + e2)
    sel1 = lane == i1
    sel2 = lane == i2
    sel = jnp.logical_or(sel1, sel2).astype(F32)
    before = jnp.dot(tri_ref[...], sel.astype(BF16), preferred_element_type=F32) + count_ref[0:1, :]
    rank1 = jnp.sum(jnp.where(sel1, before, 0.0), axis=1, keepdims=True)
    rank2 = jnp.sum(jnp.where(sel2, before, 0.0), axis=1, keepdims=True)
    count_ref[...] = count_ref[...] + jnp.sum(sel, axis=0, keepdims=True)
    cnt_ref[...] = count_ref[...]
    info = jnp.zeros(logits.shape, F32)
    for slot, val in ((INFO_IDX1, i1.astype(F32)), (INFO_IDX2, i2.astype(F32)), (INFO_RANK1, rank1),
                      (INFO_RANK2, rank2), (INFO_W1, w1), (INFO_W2, w2)):
        info = jnp.where(lane == slot, val, info)
    info_ref[...] = info


def route(h, gain, w_router, b_router, *, bt=512):
    t, d = h.shape
    n_exp = w_router.shape[1]
    bt = _blk(t, bt)
    wr = jnp.zeros((d, LANES), F32).at[:, :n_exp].set(w_router)
    br = jnp.full((1, LANES), -1e30, F32).at[0, :n_exp].set(b_router)
    tri = (lax.broadcasted_iota(jnp.int32, (bt, bt), 1)
           < lax.broadcasted_iota(jnp.int32, (bt, bt), 0)).astype(BF16)
    vmem = 2 * (2 * bt * d * 4 + d * LANES * 4 + bt * bt * 2 + bt * LANES * 4) + 4 * bt * d * 4
    hn, info, counts = pl.pallas_call(
        _router_kernel,
        grid=(t // bt,),
        in_specs=[pl.BlockSpec((bt, d), lambda i: (i, 0)),
                  pl.BlockSpec((1, d), lambda i: (0, 0)),
                  pl.BlockSpec((d, LANES), lambda i: (0, 0)),
                  pl.BlockSpec((1, LANES), lambda i: (0, 0)),
                  pl.BlockSpec((bt, bt), lambda i: (0, 0))],
        out_specs=[pl.BlockSpec((bt, d), lambda i: (i, 0)),
                   pl.BlockSpec((bt, LANES), lambda i: (i, 0)),
                   pl.BlockSpec((8, LANES), lambda i: (0, 0))],
        out_shape=[jax.ShapeDtypeStruct((t, d), F32),
                   jax.ShapeDtypeStruct((t, LANES), F32),
                   jax.ShapeDtypeStruct((8, LANES), F32)],
        scratch_shapes=[pltpu.VMEM((8, LANES), F32)],
        compiler_params=_params(("arbitrary",), vmem),
        name="moe_router",
    )(h, gain, wr, br, tri)
    return hn, info, counts[0, :n_exp]


def _gather_rows_kernel(nvalid_ref, tok_ref, src_ref, o_ref, buf_ref, sem, *, tc, rows_per_block):
    valid = pl.program_id(0) * tc < nvalid_ref[0] * rows_per_block

    def copy(j):
        return pltpu.make_async_copy(src_ref.at[pl.ds(tok_ref[0, j], 1)], buf_ref.at[pl.ds(j, 1)], sem)

    def issue(j, _):
        copy(j).start()
        return 0

    def drain(j, _):
        copy(j).wait()
        return 0

    @pl.when(valid)
    def _():
        lax.fori_loop(0, tc, issue, 0)
        lax.fori_loop(0, tc, drain, 0)
        o_ref[...] = buf_ref[...].astype(o_ref.dtype)

    @pl.when(jnp.logical_not(valid))
    def _():
        o_ref[...] = jnp.zeros_like(o_ref)


def gather_rows(src, tok_of_row, n_valid, rows_per_block, *, tc=512):
    d = src.shape[1]
    r = tok_of_row.shape[0]
    tc = _blk(r, tc)
    vmem = tc * d * 4 + 2 * tc * d * 2 + tc * d * 4
    return pl.pallas_call(
        functools.partial(_gather_rows_kernel, tc=tc, rows_per_block=rows_per_block),
        grid_spec=pltpu.PrefetchScalarGridSpec(
            num_scalar_prefetch=1,
            grid=(r // tc,),
            in_specs=[pl.BlockSpec((1, tc), lambda i, nv: (0, i), memory_space=pltpu.SMEM),
                      pl.BlockSpec(memory_space=pl.ANY)],
            out_specs=pl.BlockSpec((tc, d), lambda i, nv: (i, 0)),
            scratch_shapes=[pltpu.VMEM((tc, d), F32), pltpu.SemaphoreType.DMA(())]),
        out_shape=jax.ShapeDtypeStruct((r, d), BF16),
        compiler_params=_params(("arbitrary",), vmem),
        name="moe_gather_rows",
    )(n_valid, tok_of_row.reshape(1, r), src)


def _combine_kernel(pos_ref, y_ref, w_ref, h_ref, o_ref, hb_ref, rstd_ref, buf_ref, sem, *, tc):
    def copy(j, slot):
        return pltpu.make_async_copy(y_ref.at[pl.ds(pos_ref[slot, j], 1)],
                                     buf_ref.at[slot, pl.ds(j, 1)], sem)

    def issue(j, _):
        copy(j, 0).start()
        copy(j, 1).start()
        return 0

    def drain(j, _):
        copy(j, 0).wait()
        copy(j, 1).wait()
        return 0

    lax.fori_loop(0, tc, issue, 0)
    lax.fori_loop(0, tc, drain, 0)
    w = w_ref[...]
    out = (h_ref[...] + w[:, INFO_W1:INFO_W1 + 1] * buf_ref[0]
           + w[:, INFO_W2:INFO_W2 + 1] * buf_ref[1])
    o_ref[...] = out
    hb_ref[...] = out.astype(hb_ref.dtype)
    rstd_ref[...] = jnp.broadcast_to(lax.rsqrt(jnp.mean(out * out, axis=1, keepdims=True) + EPS),
                                     rstd_ref.shape)


def combine(y_sorted, pos, info, h, *, tc=256):
    t, d = h.shape
    tc = _blk(t, tc)
    vmem = 2 * tc * d * 4 + 2 * (2 * tc * d * 4 + 2 * tc * LANES * 4 + tc * d * 2) + 2 * tc * d * 4
    return pl.pallas_call(
        functools.partial(_combine_kernel, tc=tc),
        grid=(t // tc,),
        in_specs=[pl.BlockSpec((TOP_K, tc), lambda i: (0, i), memory_space=pltpu.SMEM),
                  pl.BlockSpec(memory_space=pl.ANY),
                  pl.BlockSpec((tc, LANES), lambda i: (i, 0)),
                  pl.BlockSpec((tc, d), lambda i: (i, 0))],
        out_specs=[pl.BlockSpec((tc, d), lambda i: (i, 0)),
                   pl.BlockSpec((tc, d), lambda i: (i, 0)),
                   pl.BlockSpec((tc, LANES), lambda i: (i, 0))],
        out_shape=[jax.ShapeDtypeStruct((t, d), F32),
                   jax.ShapeDtypeStruct((t, d), BF16),
                   jax.ShapeDtypeStruct((t, LANES), F32)],
        scratch_shapes=[pltpu.VMEM((TOP_K, tc, d), F32), pltpu.SemaphoreType.DMA(())],
        compiler_params=_params(("arbitrary",), vmem),
        name="moe_combine",
    )(pos, y_sorted, info, h)


def _expert_glu_kernel(be_ref, nvalid_ref, x_ref, wg_ref, wu_ref, o_ref):
    del be_ref

    @pl.when(pl.program_id(0) < nvalid_ref[0])
    def _():
        x = x_ref[...]
        g = jnp.dot(x, wg_ref[...], preferred_element_type=F32)
        u = jnp.dot(x, wu_ref[...], preferred_element_type=F32)
        o_ref[...] = (g * jax.nn.sigmoid(g) * u).astype(o_ref.dtype)

    @pl.when(pl.program_id(0) >= nvalid_ref[0])
    def _():
        o_ref[...] = jnp.zeros_like(o_ref)


def expert_glu(x_sorted, w_gu, block_expert, n_valid, *, bm, bn=512):
    r, d = x_sorted.shape
    f = w_gu.shape[2] // 2
    bn = _blk(f, bn)
    nf = f // bn
    vmem = 2 * (bm * d * 2 + 2 * d * bn * 2 + bm * bn * 2) + 3 * bm * bn * 4
    return pl.pallas_call(
        _expert_glu_kernel,
        grid_spec=pltpu.PrefetchScalarGridSpec(
            num_scalar_prefetch=2,
            grid=(r // bm, nf),
            in_specs=[pl.BlockSpec((bm, d), lambda i, j, be, nv: (i, 0)),
                      pl.BlockSpec((None, d, bn), lambda i, j, be, nv: (be[i], 0, j)),
                      pl.BlockSpec((None, d, bn), lambda i, j, be, nv: (be[i], 0, j + nf))],
            out_specs=pl.BlockSpec((bm, bn), lambda i, j, be, nv: (i, j))),
        out_shape=jax.ShapeDtypeStruct((r, f), BF16),
        compiler_params=_params(("arbitrary", "arbitrary"), vmem),
        name="moe_expert_glu",
    )(block_expert, n_valid, x_sorted, w_gu, w_gu)


def _expert_down_kernel(be_ref, nvalid_ref, x_ref, w_ref, o_ref):
    del be_ref

    @pl.when(pl.program_id(0) < nvalid_ref[0])
    def _():
        o_ref[...] = jnp.dot(x_ref[...], w_ref[...], preferred_element_type=F32)

    @pl.when(pl.program_id(0) >= nvalid_ref[0])
    def _():
        o_ref[...] = jnp.zeros_like(o_ref)


def expert_down(a_sorted, w_down, block_expert, n_valid, *, bm, bn=1024):
    r, f = a_sorted.shape
    d = w_down.shape[2]
    bn = _blk(d, bn)
    vmem = 2 * (bm * f * 2 + f * bn * 2 + bm * bn * 4) + bm * bn * 4
    return pl.pallas_call(
        _expert_down_kernel,
        grid_spec=pltpu.PrefetchScalarGridSpec(
            num_scalar_prefetch=2,
            grid=(r // bm, d // bn),
            in_specs=[pl.BlockSpec((bm, f), lambda i, j, be, nv: (i, 0)),
                      pl.BlockSpec((None, f, bn), lambda i, j, be, nv: (be[i], 0, j))],
            out_specs=pl.BlockSpec((bm, bn), lambda i, j, be, nv: (i, j))),
        out_shape=jax.ShapeDtypeStruct((r, d), F32),
        compiler_params=_params(("arbitrary", "arbitrary"), vmem),
        name="moe_expert_down",
    )(block_expert, n_valid, a_sorted, w_down)


def moe(h, gain, w_router, b_router, w_gu, w_down, *, bm=512):
    t, d = h.shape
    n_exp = w_router.shape[1]
    bm = _blk(t, bm)
    hn, info, counts = route(h, gain, w_router, b_router)
    counts = counts.astype(jnp.int32)
    blocks_per_expert = (counts + bm - 1) // bm
    block_end = jnp.cumsum(blocks_per_expert)
    row_offset = (block_end - blocks_per_expert) * bm
    n_blocks = TOP_K * t // bm + n_exp
    n_valid = block_end[-1:]
    block_ids = jnp.minimum(jnp.arange(n_blocks, dtype=jnp.int32), n_valid[0] - 1)
    block_expert = jnp.sum((block_ids[:, None] >= block_end[None, :]).astype(jnp.int32), axis=1)
    idx = info[:, INFO_IDX1:INFO_IDX2 + 1].astype(jnp.int32)
    rank = info[:, INFO_RANK1:INFO_RANK2 + 1].astype(jnp.int32)
    pos = (row_offset[idx] + rank).T
    token_ids = jnp.broadcast_to(jnp.arange(t, dtype=jnp.int32), (TOP_K, t))
    tok_of_row = jnp.zeros((n_blocks * bm,), jnp.int32).at[pos.reshape(-1)].set(
        token_ids.reshape(-1), unique_indices=True)
    x_sorted = gather_rows(hn, tok_of_row, n_valid, bm)
    a_sorted = expert_glu(x_sorted, w_gu, block_expert, n_valid, bm=bm)
    y_sorted = expert_down(a_sorted, w_down, block_expert, n_valid, bm=bm)
    return combine(y_sorted, pos, info, h)


def kernel(x, p, a_norm, a_w_in, a_conv_w, a_w_out, kv_norm, w_kv, b_norm, b_w_q, b_w_o, ffn_norm, dense_w_gu, dense_w_down, moe_w_router, moe_b_router, moe_w_gu, moe_w_down, ple_norm, ple_w_up, ple_w_gate, final_norm):
    batch, seq, d = x.shape
    depth = p.shape[0]
    n_a = a_norm.shape[0]
    t = batch * seq
    p = p.reshape(depth, t, p.shape[-1])
    bf = lambda w: w.astype(BF16)

    h = x.reshape(t, d)
    hb, rstd = stream_operands(h)
    kv = None
    for i in range(depth):
        if i < n_a:
            bcx = normed_matmul(hb, rstd, a_norm[i], a_w_in[i])
            mixed, w_out = short_conv(bcx, a_conv_w[i], seq), a_w_out[i]
        else:
            j = i - n_a
            if j == 0:
                kv = normed_matmul(hb, rstd, kv_norm, w_kv)
            q = normed_matmul(hb, rstd, b_norm[j], b_w_q[j], scale=HEAD_DIM ** -0.5 * LOG2E)
            mixed, w_out = stick_breaking_attention(q, kv, batch, seq), b_w_o[j]
        if i % 2 == 0:
            h, hb, rstd = resid_matmul(mixed, bf(w_out), h, emit_norm=True)
            act = normed_matmul(hb, rstd, ffn_norm[i], dense_w_gu[i // 2], glu=True, bn=256)
            f = act.shape[1]
            h, hb, rstd = resid_matmul(act, bf(dense_w_down[i // 2]), h, emit_norm=True,
                                       bk=f // 2 if (f // 2) % LANES == 0 else None)
        else:
            m = i // 2
            h = resid_matmul(mixed, bf(w_out), h, emit_norm=False)
            h, hb, rstd = moe(h, ffn_norm[i:i + 1], moe_w_router[m], moe_b_router[m],
                              bf(moe_w_gu[m]), bf(moe_w_down[m]))
        w_gate = bf(ple_w_gate[i] * ple_norm[i][:, None])
        if i + 1 < depth:
            h, hb, rstd = ple(hb, rstd, w_gate, p[i], bf(ple_w_up[i]), h, emit_norm=True)
        else:
            h = ple(hb, rstd, w_gate, p[i], bf(ple_w_up[i]), h, emit_norm=False)
    (out,) = rmsnorm(h, final_norm[None, :], F32)
    return out.reshape(batch, seq, d)
```

```python
import functools

import jax
import jax.numpy as jnp
from jax import lax
from jax.experimental import pallas as pl
from jax.experimental.pallas import tpu as pltpu

F32 = jnp.float32
BF16 = jnp.bfloat16

EPS = 1e-6
HEAD_DIM = 128
CONV_WIDTH = 3
TOP_K = 2
LANES = 128
BF16_SUBLANES = 16
MXU_COLUMNS = 256
VMEM_BUDGET_V7X = 60000 * 1024
VMEM_SLACK = 8 * 1024 * 1024
CAST_SLAB_BYTES = 4 * 1024 * 1024
LOG2E = 1.4426950408889634
EXP_ZERO_BELOW = -110.0
MASKED_SCORE = -1e30


def _params(semantics, block_bytes):
    limit = min(VMEM_BUDGET_V7X, int(block_bytes) + VMEM_SLACK)
    return pltpu.CompilerParams(dimension_semantics=semantics, vmem_limit_bytes=limit)


def _blk(dim, pref):
    b = min(dim, pref)
    while dim % b:
        b //= 2
    return b


def _rmsnorm_kernel(x_ref, g_ref, *o_refs):
    x = x_ref[...]
    y = x * lax.rsqrt(jnp.mean(x * x, axis=-1, keepdims=True) + EPS)
    for n, o_ref in enumerate(o_refs):
        o_ref[...] = (y * g_ref[n:n + 1, :]).astype(o_ref.dtype)


def rmsnorm(x, gains, out_dtype):
    t, d = x.shape
    g = gains.shape[0]
    bt = _blk(t, 256)
    out_bytes = jnp.dtype(out_dtype).itemsize
    outs = pl.pallas_call(
        _rmsnorm_kernel,
        grid=(t // bt,),
        in_specs=[pl.BlockSpec((bt, d), lambda i: (i, 0)),
                  pl.BlockSpec((g, d), lambda i: (0, 0))],
        out_specs=[pl.BlockSpec((bt, d), lambda i: (i, 0))] * g,
        out_shape=[jax.ShapeDtypeStruct((t, d), out_dtype)] * g,
        compiler_params=_params(("parallel",), 2 * bt * d * (4 + g * out_bytes)),
        name="rmsnorm",
    )(x, gains)
    return outs


def _stream_kernel(x_ref, xb_ref, rstd_ref):
    x = x_ref[...]
    xb_ref[...] = x.astype(xb_ref.dtype)
    rstd_ref[...] = jnp.broadcast_to(lax.rsqrt(jnp.mean(x * x, axis=-1, keepdims=True) + EPS), rstd_ref.shape)


def stream_operands(x):
    t, d = x.shape
    bt = _blk(t, 256)
    return pl.pallas_call(
        _stream_kernel,
        grid=(t // bt,),
        in_specs=[pl.BlockSpec((bt, d), lambda i: (i, 0))],
        out_specs=[pl.BlockSpec((bt, d), lambda i: (i, 0)), pl.BlockSpec((bt, LANES), lambda i: (i, 0))],
        out_shape=[jax.ShapeDtypeStruct((t, d), BF16), jax.ShapeDtypeStruct((t, LANES), F32)],
        compiler_params=_params(("parallel",), 2 * bt * (d * 6 + LANES * 4) + bt * d * 4),
        name="stream_operands",
    )(x)


def _emit_stream(h, j, last_j, width, hb_ref, rstd_ref, ssq_ref):
    hb_ref[...] = h.astype(hb_ref.dtype)
    _emit_rstd(jnp.sum(h * h, axis=1, keepdims=True), j, last_j, width, rstd_ref, ssq_ref)


def _emit_rstd(part, j, last_j, width, rstd_ref, ssq_ref):
    @pl.when(j == 0)
    def _():
        ssq_ref[...] = part

    @pl.when(j > 0)
    def _():
        ssq_ref[...] += part

    @pl.when(j == last_j)
    def _():
        rstd_ref[...] = jnp.broadcast_to(lax.rsqrt(ssq_ref[...] * (1.0 / width) + EPS), rstd_ref.shape)


def _resid_mm_kernel(x_ref, w_ref, r_ref, o_ref, *rest, nk, nn, width, emit_norm):
    def finish(acc):
        out = r_ref[...] + acc
        o_ref[...] = out
        if emit_norm:
            _emit_stream(out, pl.program_id(1), nn - 1, width, *rest[:3])

    part = jnp.dot(x_ref[...], w_ref[...], preferred_element_type=F32)
    if nk == 1:
        finish(part)
    else:
        acc_ref = rest[-1]
        k = pl.program_id(2)

        @pl.when(k == 0)
        def _():
            acc_ref[...] = part

        @pl.when(k > 0)
        def _():
            acc_ref[...] += part

        @pl.when(k == nk - 1)
        def _():
            finish(acc_ref[...])


def resid_matmul(x, w, resid, *, emit_norm, bm=1024, bn=512, bk=None):
    m, kdim = x.shape
    n = w.shape[1]
    bm, bn = _blk(m, bm), _blk(n, bn)
    bk = kdim if bk is None else bk
    nk = kdim // bk
    nn = n // bn
    tile = pl.BlockSpec((bm, bn), lambda i, j, k: (i, j))
    rows = pl.BlockSpec((bm, LANES), lambda i, j, k: (i, 0))
    vmem = 2 * (bm * bk + bk * bn) * 2 + 4 * bm * bn * 4 + bm * bn * 4
    out_specs = [tile]
    out_shape = [jax.ShapeDtypeStruct((m, n), F32)]
    scratch = []
    if emit_norm:
        out_specs += [tile, rows]
        out_shape += [jax.ShapeDtypeStruct((m, n), BF16), jax.ShapeDtypeStruct((m, LANES), F32)]
        scratch.append(pltpu.VMEM((bm, 1), F32))
        vmem += 2 * bm * bn * 2 + 3 * bm * LANES * 4
    if nk > 1:
        scratch.append(pltpu.VMEM((bm, bn), F32))
        vmem += bm * bn * 4
    outs = pl.pallas_call(
        functools.partial(_resid_mm_kernel, nk=nk, nn=nn, width=n, emit_norm=emit_norm),
        grid=(m // bm, nn, nk),
        in_specs=[pl.BlockSpec((bm, bk), lambda i, j, k: (i, k)),
                  pl.BlockSpec((bk, bn), lambda i, j, k: (k, j)),
                  tile],
        out_specs=out_specs,
        out_shape=out_shape,
        scratch_shapes=scratch,
        compiler_params=_params(("parallel", "arbitrary" if emit_norm else "parallel", "arbitrary"), vmem),
        name="resid_matmul",
    )(x, w, resid)
    return outs if emit_norm else outs[0]


def _normed_kernel(x_ref, rs_ref, g_ref, *refs, n_w, scale, has_cast):
    w_refs, refs = refs[:n_w], refs[n_w:]
    if has_cast:
        cast_ref, o_ref, cast_o_ref = refs[:3]
        cast_o_ref[...] = cast_ref[...].astype(cast_o_ref.dtype)
        wb_refs = refs[3:]
    else:
        o_ref, wb_refs = refs[0], refs[1:]

    @pl.when(pl.program_id(1) == 0)
    def _():
        gain = g_ref[...]
        for w_ref, wb_ref in zip(w_refs, wb_refs):
            wb_ref[...] = (w_ref[...] * gain).astype(BF16)

    x = x_ref[...]
    rstd = rs_ref[:, 0:1]
    outs = [jnp.dot(x, wb_ref[...], preferred_element_type=F32) * rstd for wb_ref in wb_refs]
    if n_w == 2:
        g, u = outs
        out = g * jax.nn.sigmoid(g) * u
    else:
        out = outs[0] if scale is None else outs[0] * scale
    o_ref[...] = out.astype(o_ref.dtype)


def normed_matmul(x, rstd, gain, w, *, glu=False, scale=None, cast=None, bm=1024, bn=512):
    m, kdim = x.shape
    n_w = 2 if glu else 1
    n = w.shape[1] // n_w
    bm, bn = _blk(m, bm), _blk(n, bn)
    nn, nm = n // bn, m // bm
    w_specs = [pl.BlockSpec((kdim, bn), lambda j, i, off=off: (0, j + off)) for off in range(0, n_w * nn, nn)]
    vmem = (2 * bm * kdim * 2 + n_w * kdim * bn * (2 * 4 + 2) + 2 * bm * bn * 2 + 2 * kdim * LANES * 4
            + 2 * bm * LANES * 4 + (n_w + 1) * bm * bn * 4)
    in_specs = [pl.BlockSpec((bm, kdim), lambda j, i: (i, 0)),
                pl.BlockSpec((bm, LANES), lambda j, i: (i, 0)),
                pl.BlockSpec((kdim, 1), lambda j, i: (0, 0))] + w_specs
    args = [x, rstd, gain.reshape(kdim, 1)] + [w] * n_w
    out_specs = [pl.BlockSpec((bm, bn), lambda j, i: (i, j))]
    out_shape = [jax.ShapeDtypeStruct((m, n), BF16)]
    slab = _cast_slab(cast, nn * nm) if cast is not None else None
    if slab is not None:
        rows, cols = cast.shape
        last = rows // slab - 1
        side = pl.BlockSpec((slab, cols), lambda j, i: (jnp.minimum(j * nm + i, last), 0))
        in_specs.append(side)
        args.append(cast)
        out_specs.append(side)
        out_shape.append(jax.ShapeDtypeStruct((rows, cols), BF16))
        vmem += 2 * slab * cols * (4 + 2)
    outs = pl.pallas_call(
        functools.partial(_normed_kernel, n_w=n_w, scale=scale, has_cast=slab is not None),
        grid=(nn, nm),
        in_specs=in_specs,
        out_specs=out_specs,
        out_shape=out_shape,
        scratch_shapes=[pltpu.VMEM((kdim, bn), BF16)] * n_w,
        compiler_params=_params(("arbitrary", "arbitrary"), vmem),
        name="normed_glu" if glu else "normed_matmul",
    )(*args)
    return (outs[0], outs[1] if slab is not None else None) if cast is not None else outs[0]


def _cast_slab(a, steps):
    rows = a.shape[0]
    slab = BF16_SUBLANES
    while slab <= rows and (rows % slab or rows // slab > steps):
        slab *= 2
    return slab if slab <= rows and slab * a.shape[1] * 4 <= CAST_SLAB_BYTES else None


def _conv_kernel(b_ref, c_ref, u_ref, ch_ref, uh_ref, w_ref, o_ref, *, blocks_per_seq):
    ts = c_ref.shape[0]
    cu = c_ref[...].astype(F32) * u_ref[...].astype(F32)
    seq_start = (pl.program_id(0) % blocks_per_seq) == 0
    halo = ch_ref[...].astype(F32) * uh_ref[...].astype(F32)
    halo = jnp.where(seq_start, 0.0, halo)
    row = lax.broadcasted_iota(jnp.int32, cu.shape, 0)
    h1 = halo[BF16_SUBLANES - 1:BF16_SUBLANES, :]
    h2 = halo[BF16_SUBLANES - 2:BF16_SUBLANES - 1, :]
    prev1 = jnp.where(row == 0, h1, pltpu.roll(cu, 1, 0))
    prev2 = jnp.where(row == 0, h2, jnp.where(row == 1, h1, pltpu.roll(cu, 2, 0)))
    y = w_ref[0:1, :] * prev2 + w_ref[1:2, :] * prev1 + w_ref[2:3, :] * cu
    o_ref[...] = (b_ref[...].astype(F32) * y).astype(o_ref.dtype)


def short_conv(bcx, conv_w, seq):
    t, d3 = bcx.shape
    d = d3 // 3
    ts = _blk(seq, 512)
    dblk = _blk(d, 1024)
    nd = d // dblk
    hb = ts // BF16_SUBLANES

    def halo_map(off):
        return lambda i, j: (jnp.maximum(i * hb - 1, 0), j + off)

    vmem = 2 * (4 * ts * dblk * 2 + 2 * BF16_SUBLANES * dblk * 2) + 6 * ts * dblk * 4
    return pl.pallas_call(
        functools.partial(_conv_kernel, blocks_per_seq=seq // ts),
        grid=(t // ts, nd),
        in_specs=[pl.BlockSpec((ts, dblk), lambda i, j: (i, j)),
                  pl.BlockSpec((ts, dblk), lambda i, j: (i, j + nd)),
                  pl.BlockSpec((ts, dblk), lambda i, j: (i, j + 2 * nd)),
                  pl.BlockSpec((BF16_SUBLANES, dblk), halo_map(nd)),
                  pl.BlockSpec((BF16_SUBLANES, dblk), halo_map(2 * nd)),
                  pl.BlockSpec((CONV_WIDTH, dblk), lambda i, j: (0, j))],
        out_specs=pl.BlockSpec((ts, dblk), lambda i, j: (i, j)),
        out_shape=jax.ShapeDtypeStruct((t, d), BF16),
        compiler_params=_params(("parallel", "parallel"), vmem),
        name="short_conv",
    )(bcx, bcx, bcx, bcx, bcx, conv_w)


def _ple_kernel(hb_ref, wg_ref, p_ref, wu_ref, h_ref, rs_ref, o_ref, *rest, nn, width):
    hb, pb, rstd = hb_ref[...], p_ref[...].astype(BF16), rs_ref[:, 0:1]
    bn = o_ref.shape[1]
    halves = [slice(0, bn // 2), slice(bn // 2, bn)] if bn % (2 * MXU_COLUMNS) == 0 else [slice(0, bn)]
    dots = [(jnp.dot(hb, wg_ref[:, c], preferred_element_type=F32),
             jnp.dot(pb, wu_ref[:, c], preferred_element_type=F32)) for c in halves]
    ssq = None
    for c, (pre, up) in zip(halves, dots):
        out = h_ref[:, c] + up * jax.nn.sigmoid(pre * rstd)
        o_ref[:, c] = out
        if rest:
            rest[0][:, c] = out.astype(rest[0].dtype)
            part = jnp.sum(out * out, axis=1, keepdims=True)
            ssq = part if ssq is None else ssq + part
    if rest:
        _emit_rstd(ssq, pl.program_id(1), nn - 1, width, *rest[1:])


def ple(hb, rstd, w_gate, p, w_up, h, *, emit_norm, bm=1024, bn=512):
    m, d = hb.shape
    pd = p.shape[1]
    n = w_gate.shape[1]
    bm, bn = _blk(m, bm), _blk(n, bn)
    nn = n // bn
    tile = pl.BlockSpec((bm, bn), lambda i, j: (i, j))
    rows = pl.BlockSpec((bm, LANES), lambda i, j: (i, 0))
    vmem = (2 * (bm * d * 2 + d * bn * 2 + bm * pd * 4 + pd * bn * 2 + 2 * bm * bn * 4 + bm * LANES * 4)
            + 3 * bm * bn * 4)
    out_specs, out_shape, scratch = [tile], [jax.ShapeDtypeStruct((m, n), F32)], []
    if emit_norm:
        out_specs += [tile, rows]
        out_shape += [jax.ShapeDtypeStruct((m, n), BF16), jax.ShapeDtypeStruct((m, LANES), F32)]
        scratch.append(pltpu.VMEM((bm, 1), F32))
        vmem += 2 * bm * bn * 2 + 3 * bm * LANES * 4
    outs = pl.pallas_call(
        functools.partial(_ple_kernel, nn=nn, width=n),
        grid=(m // bm, nn),
        in_specs=[pl.BlockSpec((bm, d), lambda i, j: (i, 0)),
                  pl.BlockSpec((d, bn), lambda i, j: (0, j)),
                  pl.BlockSpec((bm, pd), lambda i, j: (i, 0)),
                  pl.BlockSpec((pd, bn), lambda i, j: (0, j)),
                  tile, rows],
        out_specs=out_specs,
        out_shape=out_shape,
        scratch_shapes=scratch,
        compiler_params=_params(("parallel", "arbitrary" if emit_norm else "parallel"), vmem),
        name="ple",
    )(hb, w_gate, p, w_up, h, rstd)
    return outs if emit_norm else outs[0]


def _attn_kernel(q_ref, k_ref, v_ref, tri_ref, bias_ref, o_ref, acc_ref, gone_ref, *, tb, rs, group, seq):
    gone_limit = -EXP_ZERO_BELOW * LOG2E

    def scores(q, k_start, nk, bias):
        z = lax.dot_general(q, k_ref[pl.ds(k_start, nk), :], (((1,), (1,)), ((), ())),
                            preferred_element_type=F32)
        return z if bias is None else z + bias

    def split(z):
        sp = jnp.maximum(z, 0.0) + jnp.log(1.0 + jnp.exp2(-jnp.abs(z))) * LOG2E
        hi = sp.astype(BF16)
        return sp, hi, (sp - hi.astype(F32)).astype(BF16)

    def suffix_sum(hi, lo, nk):
        tri = tri_ref[0:nk, 0:nk]
        return jnp.dot(hi, tri, preferred_element_type=F32) + jnp.dot(lo, tri, preferred_element_type=F32)

    def weights(z, sp, suffix, gone):
        log2_w = z - sp - suffix
        walked = suffix[:, 0:1] + sp[:, 0:1]
        if gone is not None:
            log2_w, walked = log2_w - gone, walked + gone
        return jnp.exp2(log2_w).astype(BF16), walked

    def values(w, k_start, nk):
        return jnp.dot(w, v_ref[pl.ds(k_start, nk), :], preferred_element_type=F32)

    def tile(q, k_start, nk, gone, bias):
        z = scores(q, k_start, nk, bias)
        sp, hi, lo = split(z)
        w, gone = weights(z, sp, suffix_sum(hi, lo, nk), gone)
        return values(w, k_start, nk), gone

    def query_blocks(qis, first):
        q_starts = [pl.multiple_of(qi * tb, tb) for qi in qis]
        ns = tb // rs
        units = [(b, s, True) for b in range(len(qis)) for s in range(ns)]
        if not first:
            units += [(b, s, False) for b in range(len(qis)) for s in range(ns)]
        live = [None] * len(units)
        gone_of, pv_of = {}, {}

        def run(stage, u):
            b, s, diagonal = units[u]
            rows = slice(s * rs, (s + 1) * rs)
            nk = (s + 1) * rs if diagonal else tb
            k_start = q_starts[b] if diagonal else pl.multiple_of(q_starts[b] - tb, tb)
            if stage == 0:
                q = q_ref[pl.ds(q_starts[b] + s * rs, rs), :]
                live[u] = (scores(q, k_start, nk, bias_ref[rows, 0:nk] if diagonal else None),)
            elif stage == 1:
                (z,) = live[u]
                live[u] = (z,) + split(z)
            elif stage == 2:
                z, sp, hi, lo = live[u]
                live[u] = (z, sp, suffix_sum(hi, lo, nk))
            elif stage == 3:
                z, sp, suffix = live[u]
                w, gone_of[b, s] = weights(z, sp, suffix, gone_of.get((b, s)))
                live[u] = (w,)
            else:
                (w,) = live[u]
                live[u] = None
                pv = values(w, k_start, nk)
                if first:
                    o_ref[pl.ds(q_starts[b] + s * rs, rs), :] = pv.astype(o_ref.dtype)
                elif diagonal:
                    pv_of[b, s] = pv
                else:
                    acc_ref[b, rows, :] = pv_of.pop((b, s)) + pv
                    gone_ref[b, rows, :] = gone_of[b, s]

        n_stages = 5
        for step in range(len(units) + n_stages - 1):
            for stage in range(n_stages):
                if 0 <= step - stage < len(units):
                    run(stage, step - stage)
        if first:
            return

        for b, qi in enumerate(qis):
            def cond(carry):
                kb, alive = carry
                return jnp.logical_and(kb >= 0, alive > 0)

            def body(carry, b=b):
                kb, _ = carry
                pv_far, gone_far = tile(q_ref[pl.ds(q_starts[b], tb), :], pl.multiple_of(kb * tb, tb), tb,
                                        gone_ref[b], None)
                acc_ref[b] += pv_far
                gone_ref[b] = gone_far
                return kb - 1, (jnp.min(gone_far) < gone_limit).astype(jnp.int32)

            lax.while_loop(cond, body, (qi - 2, (jnp.min(gone_ref[b]) < gone_limit).astype(jnp.int32)))
            o_ref[pl.ds(q_starts[b], tb), :] = acc_ref[b].astype(o_ref.dtype)

    nq = seq // tb
    query_blocks([0], True)
    n_iter = (nq - 1) // group
    lead = nq - 1 - n_iter * group
    if lead:
        query_blocks(list(range(1, 1 + lead)), False)

    def step(it, carry):
        query_blocks([1 + lead + it * group + g for g in range(group)], False)
        return carry

    lax.fori_loop(0, n_iter, step, 0)


def stick_breaking_attention(q, kv, batch, seq, *, tb=256, rs=128, group=4):
    t, hd = q.shape
    n_heads = hd // HEAD_DIM
    tb = _blk(seq, tb)
    rows = lax.broadcasted_iota(jnp.int32, (tb, tb), 0)
    cols = lax.broadcasted_iota(jnp.int32, (tb, tb), 1)
    tri = (rows > cols).astype(BF16)
    bias = jnp.where(cols < rows, 0.0, MASKED_SCORE).astype(F32)
    vmem = 2 * (4 * seq * HEAD_DIM * 2 + tb * tb * 6) + 16 * tb * tb * 4
    head = lambda b, h: (b, h)
    return pl.pallas_call(
        functools.partial(_attn_kernel, tb=tb, rs=min(rs, tb), group=group, seq=seq),
        grid=(batch, n_heads),
        in_specs=[pl.BlockSpec((seq, HEAD_DIM), head),
                  pl.BlockSpec((seq, HEAD_DIM), head),
                  pl.BlockSpec((seq, HEAD_DIM), lambda b, h: (b, n_heads + h)),
                  pl.BlockSpec((tb, tb), lambda b, h: (0, 0)),
                  pl.BlockSpec((tb, tb), lambda b, h: (0, 0))],
        out_specs=pl.BlockSpec((seq, HEAD_DIM), head),
        out_shape=jax.ShapeDtypeStruct((t, hd), BF16),
        scratch_shapes=[pltpu.VMEM((group, tb, HEAD_DIM), F32), pltpu.VMEM((group, tb, 1), F32)],
        compiler_params=_params(("parallel", "parallel"), vmem),
        name="stick_breaking_attention",
    )(q, kv, kv, tri, bias)


INFO_IDX1, INFO_IDX2, INFO_RANK1, INFO_RANK2, INFO_W1, INFO_W2 = range(6)


def _pack_bf16_pairs(x):
    half = x.shape[1] // 2
    bits = lambda v: lax.bitcast_convert_type(v.astype(BF16).astype(F32), jnp.uint32)
    return (bits(x[:, :half]) >> 16) | (bits(x[:, half:]) & jnp.uint32(0xFFFF0000))


def _unpack_bf16_pairs(words):
    low = lax.bitcast_convert_type(words << 16, F32).astype(BF16)
    high = lax.bitcast_convert_type(words & jnp.uint32(0xFFFF0000), F32).astype(BF16)
    return low, high


def _router_kernel(h_ref, g_ref, wr_ref, br_ref, tri_ref, hn_ref, info_ref, cnt_ref, count_ref):
    @pl.when(pl.program_id(0) == 0)
    def _():
        count_ref[...] = jnp.zeros_like(count_ref)

    x = h_ref[...]
    hn = x * lax.rsqrt(jnp.mean(x * x, axis=-1, keepdims=True) + EPS) * g_ref[...]
    hn_ref[...] = _pack_bf16_pairs(hn)
    logits = jnp.dot(hn, wr_ref[...], precision=lax.Precision.HIGHEST,
                     preferred_element_type=F32) + br_ref[...]
    lane = lax.broadcasted_iota(jnp.int32, logits.shape, 1)
    m1 = jnp.max(logits, axis=1, keepdims=True)
    i1 = jnp.min(jnp.where(logits == m1, lane, LANES), axis=1, keepdims=True)
    rest = jnp.where(lane == i1, -jnp.inf, logits)
    m2 = jnp.max(rest, axis=1, keepdims=True)
    i2 = jnp.min(jnp.where(rest == m2, lane, LANES), axis=1, keepdims=True)
    e2 = jnp.exp(m2 - m1)
    w1 = 1.0 / (1.0 + e2)
    w2 = e2 / (1.0 + e2)
    sel1 = lane == i1
    sel2 = lane == i2
    sel = jnp.logical_or(sel1, sel2).astype(F32)
    before = jnp.dot(tri_ref[...], sel.astype(BF16), preferred_element_type=F32) + count_ref[0:1, :]
    rank1 = jnp.sum(jnp.where(sel1, before, 0.0), axis=1, keepdims=True)
    rank2 = jnp.sum(jnp.where(sel2, before, 0.0), axis=1, keepdims=True)
    count_ref[...] = count_ref[...] + jnp.sum(sel, axis=0, keepdims=True)
    cnt_ref[...] = count_ref[...]
    info = jnp.zeros(logits.shape, F32)
    for slot, val in ((INFO_IDX1, i1.astype(F32)), (INFO_IDX2, i2.astype(F32)), (INFO_RANK1, rank1),
                      (INFO_RANK2, rank2), (INFO_W1, w1), (INFO_W2, w2)):
        info = jnp.where(lane == slot, val, info)
    info_ref[...] = info


def route(h, gain, w_router, b_router, *, bt=512):
    t, d = h.shape
    n_exp = w_router.shape[1]
    bt = _blk(t, bt)
    wr = jnp.zeros((d, LANES), F32).at[:, :n_exp].set(w_router)
    br = jnp.full((1, LANES), -1e30, F32).at[0, :n_exp].set(b_router)
    tri = (lax.broadcasted_iota(jnp.int32, (bt, bt), 1)
           < lax.broadcasted_iota(jnp.int32, (bt, bt), 0)).astype(BF16)
    vmem = 2 * (2 * bt * d * 4 + d * LANES * 4 + bt * bt * 2 + bt * LANES * 4) + 4 * bt * d * 4
    hn, info, counts = pl.pallas_call(
        _router_kernel,
        grid=(t // bt,),
        in_specs=[pl.BlockSpec((bt, d), lambda i: (i, 0)),
                  pl.BlockSpec((1, d), lambda i: (0, 0)),
                  pl.BlockSpec((d, LANES), lambda i: (0, 0)),
                  pl.BlockSpec((1, LANES), lambda i: (0, 0)),
                  pl.BlockSpec((bt, bt), lambda i: (0, 0))],
        out_specs=[pl.BlockSpec((bt, d // 2), lambda i: (i, 0)),
                   pl.BlockSpec((bt, LANES), lambda i: (i, 0)),
                   pl.BlockSpec((8, LANES), lambda i: (0, 0))],
        out_shape=[jax.ShapeDtypeStruct((t, d // 2), jnp.uint32),
                   jax.ShapeDtypeStruct((t, LANES), F32),
                   jax.ShapeDtypeStruct((8, LANES), F32)],
        scratch_shapes=[pltpu.VMEM((8, LANES), F32)],
        compiler_params=_params(("arbitrary",), vmem),
        name="moe_router",
    )(h, gain, wr, br, tri)
    return hn, info, counts[0, :n_exp]


def _gather_rows_kernel(nvalid_ref, tok_ref, src_ref, o_ref, buf_ref, sem, *, tc, rows_per_block):
    valid = pl.program_id(0) * tc < nvalid_ref[0] * rows_per_block

    def copy(j):
        return pltpu.make_async_copy(src_ref.at[pl.ds(tok_ref[0, j], 1)], buf_ref.at[pl.ds(j, 1)], sem)

    def issue(j, _):
        copy(j).start()
        return 0

    def drain(j, _):
        copy(j).wait()
        return 0

    @pl.when(valid)
    def _():
        lax.fori_loop(0, tc, issue, 0)
        lax.fori_loop(0, tc, drain, 0)
        half = buf_ref.shape[1]
        o_ref[:, :half], o_ref[:, half:] = _unpack_bf16_pairs(buf_ref[...])

    @pl.when(jnp.logical_not(valid))
    def _():
        o_ref[...] = jnp.zeros_like(o_ref)


def gather_rows(src, tok_of_row, n_valid, rows_per_block, *, tc=512):
    d = src.shape[1] * 2
    r = tok_of_row.shape[0]
    tc = _blk(r, tc)
    vmem = tc * d * 2 + 2 * tc * d * 2 + tc * d * 4
    return pl.pallas_call(
        functools.partial(_gather_rows_kernel, tc=tc, rows_per_block=rows_per_block),
        grid_spec=pltpu.PrefetchScalarGridSpec(
            num_scalar_prefetch=1,
            grid=(r // tc,),
            in_specs=[pl.BlockSpec((1, tc), lambda i, nv: (0, i), memory_space=pltpu.SMEM),
                      pl.BlockSpec(memory_space=pl.ANY)],
            out_specs=pl.BlockSpec((tc, d), lambda i, nv: (i, 0)),
            scratch_shapes=[pltpu.VMEM((tc, d // 2), jnp.uint32), pltpu.SemaphoreType.DMA(())]),
        out_shape=jax.ShapeDtypeStruct((r, d), BF16),
        compiler_params=_params(("arbitrary",), vmem),
        name="moe_gather_rows",
    )(n_valid, tok_of_row.reshape(1, r), src)


def _combine_kernel(pos_ref, y_ref, w_ref, h_ref, o_ref, hb_ref, rstd_ref, buf_ref, sem, *, tc):
    def copy(j, slot):
        return pltpu.make_async_copy(y_ref.at[pl.ds(pos_ref[slot, j], 1)],
                                     buf_ref.at[slot, pl.ds(j, 1)], sem)

    def issue(j, _):
        copy(j, 0).start()
        copy(j, 1).start()
        return 0

    def drain(j, _):
        copy(j, 0).wait()
        copy(j, 1).wait()
        return 0

    lax.fori_loop(0, tc, issue, 0)
    lax.fori_loop(0, tc, drain, 0)
    w = w_ref[...]
    out = (h_ref[...] + w[:, INFO_W1:INFO_W1 + 1] * buf_ref[0]
           + w[:, INFO_W2:INFO_W2 + 1] * buf_ref[1])
    o_ref[...] = out
    hb_ref[...] = out.astype(hb_ref.dtype)
    rstd_ref[...] = jnp.broadcast_to(lax.rsqrt(jnp.mean(out * out, axis=1, keepdims=True) + EPS),
                                     rstd_ref.shape)


def combine(y_sorted, pos, info, h, *, tc=256):
    t, d = h.shape
    tc = _blk(t, tc)
    vmem = 2 * tc * d * 4 + 2 * (2 * tc * d * 4 + 2 * tc * LANES * 4 + tc * d * 2) + 2 * tc * d * 4
    return pl.pallas_call(
        functools.partial(_combine_kernel, tc=tc),
        grid=(t // tc,),
        in_specs=[pl.BlockSpec((TOP_K, tc), lambda i: (0, i), memory_space=pltpu.SMEM),
                  pl.BlockSpec(memory_space=pl.ANY),
                  pl.BlockSpec((tc, LANES), lambda i: (i, 0)),
                  pl.BlockSpec((tc, d), lambda i: (i, 0))],
        out_specs=[pl.BlockSpec((tc, d), lambda i: (i, 0)),
                   pl.BlockSpec((tc, d), lambda i: (i, 0)),
                   pl.BlockSpec((tc, LANES), lambda i: (i, 0))],
        out_shape=[jax.ShapeDtypeStruct((t, d), F32),
                   jax.ShapeDtypeStruct((t, d), BF16),
                   jax.ShapeDtypeStruct((t, LANES), F32)],
        scratch_shapes=[pltpu.VMEM((TOP_K, tc, d), F32), pltpu.SemaphoreType.DMA(())],
        compiler_params=_params(("arbitrary",), vmem),
        name="moe_combine",
    )(pos, y_sorted, info, h)


def _expert_glu_kernel(be_ref, nvalid_ref, x_ref, wg_ref, wu_ref, o_ref):
    del be_ref

    @pl.when(pl.program_id(0) < nvalid_ref[0])
    def _():
        x = x_ref[...]
        g = jnp.dot(x, wg_ref[...], preferred_element_type=F32)
        u = jnp.dot(x, wu_ref[...], preferred_element_type=F32)
        o_ref[...] = (g * jax.nn.sigmoid(g) * u).astype(o_ref.dtype)

    @pl.when(pl.program_id(0) >= nvalid_ref[0])
    def _():
        o_ref[...] = jnp.zeros_like(o_ref)


def expert_glu(x_sorted, w_gu, block_expert, n_valid, *, bm, bn=512):
    r, d = x_sorted.shape
    f = w_gu.shape[2] // 2
    bn = _blk(f, bn)
    nf = f // bn
    vmem = 2 * (bm * d * 2 + 2 * d * bn * 2 + bm * bn * 2) + 3 * bm * bn * 4
    return pl.pallas_call(
        _expert_glu_kernel,
        grid_spec=pltpu.PrefetchScalarGridSpec(
            num_scalar_prefetch=2,
            grid=(r // bm, nf),
            in_specs=[pl.BlockSpec((bm, d), lambda i, j, be, nv: (i, 0)),
                      pl.BlockSpec((None, d, bn), lambda i, j, be, nv: (be[i], 0, j)),
                      pl.BlockSpec((None, d, bn), lambda i, j, be, nv: (be[i], 0, j + nf))],
            out_specs=pl.BlockSpec((bm, bn), lambda i, j, be, nv: (i, j))),
        out_shape=jax.ShapeDtypeStruct((r, f), BF16),
        compiler_params=_params(("arbitrary", "arbitrary"), vmem),
        name="moe_expert_glu",
    )(block_expert, n_valid, x_sorted, w_gu, w_gu)


def _expert_down_kernel(be_ref, nvalid_ref, x_ref, w_ref, o_ref):
    del be_ref

    @pl.when(pl.program_id(0) < nvalid_ref[0])
    def _():
        o_ref[...] = jnp.dot(x_ref[...], w_ref[...], preferred_element_type=F32)

    @pl.when(pl.program_id(0) >= nvalid_ref[0])
    def _():
        o_ref[...] = jnp.zeros_like(o_ref)


def expert_down(a_sorted, w_down, block_expert, n_valid, *, bm, bn=1024):
    r, f = a_sorted.shape
    d = w_down.shape[2]
    bn = _blk(d, bn)
    vmem = 2 * (bm * f * 2 + f * bn * 2 + bm * bn * 4) + bm * bn * 4
    return pl.pallas_call(
        _expert_down_kernel,
        grid_spec=pltpu.PrefetchScalarGridSpec(
            num_scalar_prefetch=2,
            grid=(r // bm, d // bn),
            in_specs=[pl.BlockSpec((bm, f), lambda i, j, be, nv: (i, 0)),
                      pl.BlockSpec((None, f, bn), lambda i, j, be, nv: (be[i], 0, j))],
            out_specs=pl.BlockSpec((bm, bn), lambda i, j, be, nv: (i, j))),
        out_shape=jax.ShapeDtypeStruct((r, d), F32),
        compiler_params=_params(("arbitrary", "arbitrary"), vmem),
        name="moe_expert_down",
    )(block_expert, n_valid, a_sorted, w_down)


def moe(h, gain, w_router, b_router, w_gu, w_down, *, bm=512):
    t, d = h.shape
    n_exp = w_router.shape[1]
    bm = _blk(t, bm)
    hn, info, counts = route(h, gain, w_router, b_router)
    counts = counts.astype(jnp.int32)
    blocks_per_expert = (counts + bm - 1) // bm
    block_end = jnp.cumsum(blocks_per_expert)
    row_offset = (block_end - blocks_per_expert) * bm
    n_blocks = TOP_K * t // bm + n_exp
    n_valid = block_end[-1:]
    block_ids = jnp.minimum(jnp.arange(n_blocks, dtype=jnp.int32), n_valid[0] - 1)
    block_expert = jnp.sum((block_ids[:, None] >= block_end[None, :]).astype(jnp.int32), axis=1)
    idx = info[:, INFO_IDX1:INFO_IDX2 + 1].astype(jnp.int32)
    rank = info[:, INFO_RANK1:INFO_RANK2 + 1].astype(jnp.int32)
    pos = (row_offset[idx] + rank).T
    token_ids = jnp.broadcast_to(jnp.arange(t, dtype=jnp.int32), (TOP_K, t))
    tok_of_row = jnp.zeros((n_blocks * bm,), jnp.int32).at[pos.reshape(-1)].set(
        token_ids.reshape(-1), unique_indices=True)
    x_sorted = gather_rows(hn, tok_of_row, n_valid, bm)
    a_sorted = expert_glu(x_sorted, w_gu, block_expert, n_valid, bm=bm)
    y_sorted = expert_down(a_sorted, w_down, block_expert, n_valid, bm=bm)
    return combine(y_sorted, pos, info, h)


def kernel(x, p, a_norm, a_w_in, a_conv_w, a_w_out, kv_norm, w_kv, b_norm, b_w_q, b_w_o, ffn_norm, dense_w_gu, dense_w_down, moe_w_router, moe_b_router, moe_w_gu, moe_w_down, ple_norm, ple_w_up, ple_w_gate, final_norm):
    batch, seq, d = x.shape
    depth = p.shape[0]
    n_a = a_norm.shape[0]
    t = batch * seq
    p = p.reshape(depth, t, p.shape[-1])
    bf = lambda w: w.astype(BF16)

    h = x.reshape(t, d)
    hb, rstd = stream_operands(h)
    kv = None

    to_convert = [(name, m, w[m].reshape(-1, w.shape[-1]))
                  for m in range(moe_w_gu.shape[0]) for name, w in (("down", moe_w_down), ("gu", moe_w_gu))]
    converted = {}

    def hosting(*args, **kwargs):
        if not to_convert:
            return normed_matmul(*args, **kwargs)
        name, m, w2d = to_convert[0]
        out, wb = normed_matmul(*args, cast=w2d, **kwargs)
        if wb is not None:
            converted[name, m] = wb
            to_convert.pop(0)
        return out

    def expert_weights(name, w, m):
        to_convert[:] = [job for job in to_convert if job[:2] != (name, m)]
        wb = converted.get((name, m))
        return bf(w[m]) if wb is None else wb.reshape(w.shape[1:])
    for i in range(depth):
        if i < n_a:
            bcx = hosting(hb, rstd, a_norm[i], a_w_in[i])
            mixed, w_out = short_conv(bcx, a_conv_w[i], seq), a_w_out[i]
        else:
            j = i - n_a
            if j == 0:
                kv = hosting(hb, rstd, kv_norm, w_kv)
            q = hosting(hb, rstd, b_norm[j], b_w_q[j], scale=HEAD_DIM ** -0.5 * LOG2E)
            mixed, w_out = stick_breaking_attention(q, kv, batch, seq), b_w_o[j]
        if i % 2 == 0:
            h, hb, rstd = resid_matmul(mixed, bf(w_out), h, emit_norm=True)
            act = hosting(hb, rstd, ffn_norm[i], dense_w_gu[i // 2], glu=True, bn=256)
            f = act.shape[1]
            h, hb, rstd = resid_matmul(act, bf(dense_w_down[i // 2]), h, emit_norm=True,
                                       bk=f // 2 if (f // 2) % LANES == 0 else None)
        else:
            m = i // 2
            h = resid_matmul(mixed, bf(w_out), h, emit_norm=False)
            h, hb, rstd = moe(h, ffn_norm[i:i + 1], moe_w_router[m], moe_b_router[m],
                              expert_weights("gu", moe_w_gu, m), expert_weights("down", moe_w_down, m))
        w_gate = bf(ple_w_gate[i] * ple_norm[i][:, None])
        if i + 1 < depth:
            h, hb, rstd = ple(hb, rstd, w_gate, p[i], bf(ple_w_up[i]), h, emit_norm=True)
        else:
            h = ple(hb, rstd, w_gate, p[i], bf(ple_w_up[i]), h, emit_norm=False)
    (out,) = rmsnorm(h, final_norm[None, :], F32)
    return out.reshape(batch, seq, d)
```

```python
import functools

import jax
import jax.numpy as jnp
from jax import lax
from jax.experimental import pallas as pl
from jax.experimental.pallas import tpu as pltpu

F32 = jnp.float32
BF16 = jnp.bfloat16

EPS = 1e-6
HEAD_DIM = 128
CONV_WIDTH = 3
TOP_K = 2
LANES = 128
BF16_SUBLANES = 16
MXU_COLUMNS = 256
VMEM_BUDGET_V7X = 60000 * 1024
VMEM_SLACK = 8 * 1024 * 1024
CAST_SLAB_BYTES = 4 * 1024 * 1024
LOG2E = 1.4426950408889634
EXP_ZERO_BELOW = -110.0
MASKED_SCORE = -1e30


def _params(semantics, block_bytes):
    limit = min(VMEM_BUDGET_V7X, int(block_bytes) + VMEM_SLACK)
    return pltpu.CompilerParams(dimension_semantics=semantics, vmem_limit_bytes=limit)


def _blk(dim, pref):
    b = min(dim, pref)
    while dim % b:
        b //= 2
    return b


def _rmsnorm_kernel(x_ref, g_ref, *o_refs):
    x = x_ref[...]
    y = x * lax.rsqrt(jnp.mean(x * x, axis=-1, keepdims=True) + EPS)
    for n, o_ref in enumerate(o_refs):
        o_ref[...] = (y * g_ref[n:n + 1, :]).astype(o_ref.dtype)


def rmsnorm(x, gains, out_dtype):
    t, d = x.shape
    g = gains.shape[0]
    bt = _blk(t, 256)
    out_bytes = jnp.dtype(out_dtype).itemsize
    outs = pl.pallas_call(
        _rmsnorm_kernel,
        grid=(t // bt,),
        in_specs=[pl.BlockSpec((bt, d), lambda i: (i, 0)),
                  pl.BlockSpec((g, d), lambda i: (0, 0))],
        out_specs=[pl.BlockSpec((bt, d), lambda i: (i, 0))] * g,
        out_shape=[jax.ShapeDtypeStruct((t, d), out_dtype)] * g,
        compiler_params=_params(("parallel",), 2 * bt * d * (4 + g * out_bytes)),
        name="rmsnorm",
    )(x, gains)
    return outs


def _stream_kernel(x_ref, xb_ref, rstd_ref):
    x = x_ref[...]
    xb_ref[...] = x.astype(xb_ref.dtype)
    rstd_ref[...] = jnp.broadcast_to(lax.rsqrt(jnp.mean(x * x, axis=-1, keepdims=True) + EPS), rstd_ref.shape)


def stream_operands(x):
    t, d = x.shape
    bt = _blk(t, 256)
    return pl.pallas_call(
        _stream_kernel,
        grid=(t // bt,),
        in_specs=[pl.BlockSpec((bt, d), lambda i: (i, 0))],
        out_specs=[pl.BlockSpec((bt, d), lambda i: (i, 0)), pl.BlockSpec((bt, LANES), lambda i: (i, 0))],
        out_shape=[jax.ShapeDtypeStruct((t, d), BF16), jax.ShapeDtypeStruct((t, LANES), F32)],
        compiler_params=_params(("parallel",), 2 * bt * (d * 6 + LANES * 4) + bt * d * 4),
        name="stream_operands",
    )(x)


def _emit_stream(h, j, last_j, width, hb_ref, rstd_ref, ssq_ref):
    hb_ref[...] = h.astype(hb_ref.dtype)
    _emit_rstd(jnp.sum(h * h, axis=1, keepdims=True), j, last_j, width, rstd_ref, ssq_ref)


def _emit_rstd(part, j, last_j, width, rstd_ref, ssq_ref):
    @pl.when(j == 0)
    def _():
        ssq_ref[...] = part

    @pl.when(j > 0)
    def _():
        ssq_ref[...] += part

    @pl.when(j == last_j)
    def _():
        rstd_ref[...] = jnp.broadcast_to(lax.rsqrt(ssq_ref[...] * (1.0 / width) + EPS), rstd_ref.shape)


def _resid_mm_kernel(x_ref, w_ref, r_ref, o_ref, *rest, nn, width):
    out = r_ref[...] + jnp.dot(x_ref[...], w_ref[...], preferred_element_type=F32)
    o_ref[...] = out
    if rest:
        _emit_stream(out, pl.program_id(1), nn - 1, width, *rest)


def resid_matmul(x, w, resid, *, emit_norm, bm=1024, bn=512):
    m, kdim = x.shape
    n = w.shape[1]
    bm, bn = _blk(m, bm), _blk(n, bn)
    nn = n // bn
    tile = pl.BlockSpec((bm, bn), lambda i, j: (i, j))
    rows = pl.BlockSpec((bm, LANES), lambda i, j: (i, 0))
    vmem = 2 * (bm * kdim + kdim * bn) * 2 + 4 * bm * bn * 4 + bm * bn * 4
    out_specs = [tile]
    out_shape = [jax.ShapeDtypeStruct((m, n), F32)]
    scratch = []
    if emit_norm:
        out_specs += [tile, rows]
        out_shape += [jax.ShapeDtypeStruct((m, n), BF16), jax.ShapeDtypeStruct((m, LANES), F32)]
        scratch.append(pltpu.VMEM((bm, 1), F32))
        vmem += 2 * bm * bn * 2 + 3 * bm * LANES * 4
    outs = pl.pallas_call(
        functools.partial(_resid_mm_kernel, nn=nn, width=n),
        grid=(m // bm, nn),
        in_specs=[pl.BlockSpec((bm, kdim), lambda i, j: (i, 0)),
                  pl.BlockSpec((kdim, bn), lambda i, j: (0, j)),
                  tile],
        out_specs=out_specs,
        out_shape=out_shape,
        scratch_shapes=scratch,
        compiler_params=_params(("parallel", "arbitrary" if emit_norm else "parallel"), vmem),
        name="resid_matmul",
    )(x, w, resid)
    return outs if emit_norm else outs[0]


def _normed_kernel(x_ref, rs_ref, g_ref, *refs, n_w, scale, has_cast):
    w_refs, refs = refs[:n_w], refs[n_w:]
    if has_cast:
        cast_ref, o_ref, cast_o_ref = refs[:3]
        cast_o_ref[...] = cast_ref[...].astype(cast_o_ref.dtype)
        wb_refs = refs[3:]
    else:
        o_ref, wb_refs = refs[0], refs[1:]

    @pl.when(pl.program_id(1) == 0)
    def _():
        gain = g_ref[...]
        for w_ref, wb_ref in zip(w_refs, wb_refs):
            wb_ref[...] = (w_ref[...] * gain).astype(BF16)

    x = x_ref[...]
    rstd = rs_ref[:, 0:1]
    outs = [jnp.dot(x, wb_ref[...], preferred_element_type=F32) * rstd for wb_ref in wb_refs]
    if n_w == 2:
        g, u = outs
        out = g * jax.nn.sigmoid(g) * u
    else:
        out = outs[0] if scale is None else outs[0] * scale
    o_ref[...] = out.astype(o_ref.dtype)


def normed_matmul(x, rstd, gain, w, *, glu=False, scale=None, cast=None, bm=1024, bn=512):
    m, kdim = x.shape
    n_w = 2 if glu else 1
    n = w.shape[1] // n_w
    bm, bn = _blk(m, bm), _blk(n, bn)
    nn, nm = n // bn, m // bm
    w_specs = [pl.BlockSpec((kdim, bn), lambda j, i, off=off: (0, j + off)) for off in range(0, n_w * nn, nn)]
    vmem = (2 * bm * kdim * 2 + n_w * kdim * bn * (2 * 4 + 2) + 2 * bm * bn * 2 + 2 * kdim * LANES * 4
            + 2 * bm * LANES * 4 + (n_w + 1) * bm * bn * 4)
    in_specs = [pl.BlockSpec((bm, kdim), lambda j, i: (i, 0)),
                pl.BlockSpec((bm, LANES), lambda j, i: (i, 0)),
                pl.BlockSpec((kdim, 1), lambda j, i: (0, 0))] + w_specs
    args = [x, rstd, gain.reshape(kdim, 1)] + [w] * n_w
    out_specs = [pl.BlockSpec((bm, bn), lambda j, i: (i, j))]
    out_shape = [jax.ShapeDtypeStruct((m, n), BF16)]
    slab = _cast_slab(cast, nn * nm) if cast is not None else None
    if slab is not None:
        rows, cols = cast.shape
        last = rows // slab - 1
        side = pl.BlockSpec((slab, cols), lambda j, i: (jnp.minimum(j * nm + i, last), 0))
        in_specs.append(side)
        args.append(cast)
        out_specs.append(side)
        out_shape.append(jax.ShapeDtypeStruct((rows, cols), BF16))
        vmem += 2 * slab * cols * (4 + 2)
    outs = pl.pallas_call(
        functools.partial(_normed_kernel, n_w=n_w, scale=scale, has_cast=slab is not None),
        grid=(nn, nm),
        in_specs=in_specs,
        out_specs=out_specs,
        out_shape=out_shape,
        scratch_shapes=[pltpu.VMEM((kdim, bn), BF16)] * n_w,
        compiler_params=_params(("arbitrary", "arbitrary"), vmem),
        name="normed_glu" if glu else "normed_matmul",
    )(*args)
    return (outs[0], outs[1] if slab is not None else None) if cast is not None else outs[0]


def _cast_slab(a, steps):
    rows = a.shape[0]
    slab = BF16_SUBLANES
    while slab <= rows and (rows % slab or rows // slab > steps):
        slab *= 2
    return slab if slab <= rows and slab * a.shape[1] * 4 <= CAST_SLAB_BYTES else None


def _conv_kernel(b_ref, c_ref, u_ref, ch_ref, uh_ref, w_ref, o_ref, *, blocks_per_seq):
    ts = c_ref.shape[0]
    cu = c_ref[...].astype(F32) * u_ref[...].astype(F32)
    seq_start = (pl.program_id(0) % blocks_per_seq) == 0
    halo = ch_ref[...].astype(F32) * uh_ref[...].astype(F32)
    halo = jnp.where(seq_start, 0.0, halo)
    row = lax.broadcasted_iota(jnp.int32, cu.shape, 0)
    h1 = halo[BF16_SUBLANES - 1:BF16_SUBLANES, :]
    h2 = halo[BF16_SUBLANES - 2:BF16_SUBLANES - 1, :]
    prev1 = jnp.where(row == 0, h1, pltpu.roll(cu, 1, 0))
    prev2 = jnp.where(row == 0, h2, jnp.where(row == 1, h1, pltpu.roll(cu, 2, 0)))
    y = w_ref[0:1, :] * prev2 + w_ref[1:2, :] * prev1 + w_ref[2:3, :] * cu
    o_ref[...] = (b_ref[...].astype(F32) * y).astype(o_ref.dtype)


def short_conv(bcx, conv_w, seq):
    t, d3 = bcx.shape
    d = d3 // 3
    ts = _blk(seq, 512)
    dblk = _blk(d, 1024)
    nd = d // dblk
    hb = ts // BF16_SUBLANES

    def halo_map(off):
        return lambda i, j: (jnp.maximum(i * hb - 1, 0), j + off)

    vmem = 2 * (4 * ts * dblk * 2 + 2 * BF16_SUBLANES * dblk * 2) + 6 * ts * dblk * 4
    return pl.pallas_call(
        functools.partial(_conv_kernel, blocks_per_seq=seq // ts),
        grid=(t // ts, nd),
        in_specs=[pl.BlockSpec((ts, dblk), lambda i, j: (i, j)),
                  pl.BlockSpec((ts, dblk), lambda i, j: (i, j + nd)),
                  pl.BlockSpec((ts, dblk), lambda i, j: (i, j + 2 * nd)),
                  pl.BlockSpec((BF16_SUBLANES, dblk), halo_map(nd)),
                  pl.BlockSpec((BF16_SUBLANES, dblk), halo_map(2 * nd)),
                  pl.BlockSpec((CONV_WIDTH, dblk), lambda i, j: (0, j))],
        out_specs=pl.BlockSpec((ts, dblk), lambda i, j: (i, j)),
        out_shape=jax.ShapeDtypeStruct((t, d), BF16),
        compiler_params=_params(("parallel", "parallel"), vmem),
        name="short_conv",
    )(bcx, bcx, bcx, bcx, bcx, conv_w)


def _ple_kernel(hb_ref, wg_ref, p_ref, wu_ref, h_ref, rs_ref, o_ref, *rest, nn, width):
    hb, pb, rstd = hb_ref[...], p_ref[...].astype(BF16), rs_ref[:, 0:1]
    bn = o_ref.shape[1]
    halves = [slice(0, bn // 2), slice(bn // 2, bn)] if bn % (2 * MXU_COLUMNS) == 0 else [slice(0, bn)]
    dots = [(jnp.dot(hb, wg_ref[:, c], preferred_element_type=F32),
             jnp.dot(pb, wu_ref[:, c], preferred_element_type=F32)) for c in halves]
    ssq = None
    for c, (pre, up) in zip(halves, dots):
        out = h_ref[:, c] + up * jax.nn.sigmoid(pre * rstd)
        o_ref[:, c] = out
        if rest:
            rest[0][:, c] = out.astype(rest[0].dtype)
            part = jnp.sum(out * out, axis=1, keepdims=True)
            ssq = part if ssq is None else ssq + part
    if rest:
        _emit_rstd(ssq, pl.program_id(1), nn - 1, width, *rest[1:])


def ple(hb, rstd, w_gate, p, w_up, h, *, emit_norm, bm=1024, bn=512):
    m, d = hb.shape
    pd = p.shape[1]
    n = w_gate.shape[1]
    bm, bn = _blk(m, bm), _blk(n, bn)
    nn = n // bn
    tile = pl.BlockSpec((bm, bn), lambda i, j: (i, j))
    rows = pl.BlockSpec((bm, LANES), lambda i, j: (i, 0))
    vmem = (2 * (bm * d * 2 + d * bn * 2 + bm * pd * 4 + pd * bn * 2 + 2 * bm * bn * 4 + bm * LANES * 4)
            + 3 * bm * bn * 4)
    out_specs, out_shape, scratch = [tile], [jax.ShapeDtypeStruct((m, n), F32)], []
    if emit_norm:
        out_specs += [tile, rows]
        out_shape += [jax.ShapeDtypeStruct((m, n), BF16), jax.ShapeDtypeStruct((m, LANES), F32)]
        scratch.append(pltpu.VMEM((bm, 1), F32))
        vmem += 2 * bm * bn * 2 + 3 * bm * LANES * 4
    outs = pl.pallas_call(
        functools.partial(_ple_kernel, nn=nn, width=n),
        grid=(m // bm, nn),
        in_specs=[pl.BlockSpec((bm, d), lambda i, j: (i, 0)),
                  pl.BlockSpec((d, bn), lambda i, j: (0, j)),
                  pl.BlockSpec((bm, pd), lambda i, j: (i, 0)),
                  pl.BlockSpec((pd, bn), lambda i, j: (0, j)),
                  tile, rows],
        out_specs=out_specs,
        out_shape=out_shape,
        scratch_shapes=scratch,
        compiler_params=_params(("parallel", "arbitrary" if emit_norm else "parallel"), vmem),
        name="ple",
    )(hb, w_gate, p, w_up, h, rstd)
    return outs if emit_norm else outs[0]


def _attn_kernel(q_ref, k_ref, v_ref, tri_ref, bias_ref, o_ref, acc_ref, gone_ref, *, tb, rs, group, seq):
    gone_limit = -EXP_ZERO_BELOW * LOG2E

    def scores(q, k_start, nk, bias):
        z = lax.dot_general(q, k_ref[pl.ds(k_start, nk), :], (((1,), (1,)), ((), ())),
                            preferred_element_type=F32)
        return z if bias is None else z + bias

    def split(z):
        sp = jnp.maximum(z, 0.0) + jnp.log(1.0 + jnp.exp2(-jnp.abs(z))) * LOG2E
        hi = sp.astype(BF16)
        return sp, hi, (sp - hi.astype(F32)).astype(BF16)

    def suffix_sum(hi, lo, nk):
        tri = tri_ref[0:nk, 0:nk]
        return jnp.dot(hi, tri, preferred_element_type=F32) + jnp.dot(lo, tri, preferred_element_type=F32)

    def weights(z, sp, suffix, gone):
        log2_w = z - sp - suffix
        walked = suffix[:, 0:1] + sp[:, 0:1]
        if gone is not None:
            log2_w, walked = log2_w - gone, walked + gone
        return jnp.exp2(log2_w).astype(BF16), walked

    def values(w, k_start, nk):
        return jnp.dot(w, v_ref[pl.ds(k_start, nk), :], preferred_element_type=F32)

    def tile(q, k_start, nk, gone, bias):
        z = scores(q, k_start, nk, bias)
        sp, hi, lo = split(z)
        w, gone = weights(z, sp, suffix_sum(hi, lo, nk), gone)
        return values(w, k_start, nk), gone

    def query_blocks(qis, first):
        q_starts = [pl.multiple_of(qi * tb, tb) for qi in qis]
        ns = tb // rs
        units = [(b, s, True) for b in range(len(qis)) for s in range(ns)]
        if not first:
            units += [(b, s, False) for b in range(len(qis)) for s in range(ns)]
        live = [None] * len(units)
        gone_of, pv_of = {}, {}

        def run(stage, u):
            b, s, diagonal = units[u]
            rows = slice(s * rs, (s + 1) * rs)
            nk = (s + 1) * rs if diagonal else tb
            k_start = q_starts[b] if diagonal else pl.multiple_of(q_starts[b] - tb, tb)
            if stage == 0:
                q = q_ref[pl.ds(q_starts[b] + s * rs, rs), :]
                live[u] = (scores(q, k_start, nk, bias_ref[rows, 0:nk] if diagonal else None),)
            elif stage == 1:
                (z,) = live[u]
                live[u] = (z,) + split(z)
            elif stage == 2:
                z, sp, hi, lo = live[u]
                live[u] = (z, sp, suffix_sum(hi, lo, nk))
            elif stage == 3:
                z, sp, suffix = live[u]
                w, gone_of[b, s] = weights(z, sp, suffix, gone_of.get((b, s)))
                live[u] = (w,)
            else:
                (w,) = live[u]
                live[u] = None
                pv = values(w, k_start, nk)
                if first:
                    o_ref[pl.ds(q_starts[b] + s * rs, rs), :] = pv.astype(o_ref.dtype)
                elif diagonal:
                    pv_of[b, s] = pv
                else:
                    acc_ref[b, rows, :] = pv_of.pop((b, s)) + pv
                    gone_ref[b, rows, :] = gone_of[b, s]

        n_stages = 5
        for step in range(len(units) + n_stages - 1):
            for stage in range(n_stages):
                if 0 <= step - stage < len(units):
                    run(stage, step - stage)
        if first:
            return

        for b, qi in enumerate(qis):
            def cond(carry):
                kb, alive = carry
                return jnp.logical_and(kb >= 0, alive > 0)

            def body(carry, b=b):
                kb, _ = carry
                pv_far, gone_far = tile(q_ref[pl.ds(q_starts[b], tb), :], pl.multiple_of(kb * tb, tb), tb,
                                        gone_ref[b], None)
                acc_ref[b] += pv_far
                gone_ref[b] = gone_far
                return kb - 1, (jnp.min(gone_far) < gone_limit).astype(jnp.int32)

            lax.while_loop(cond, body, (qi - 2, (jnp.min(gone_ref[b]) < gone_limit).astype(jnp.int32)))
            o_ref[pl.ds(q_starts[b], tb), :] = acc_ref[b].astype(o_ref.dtype)

    nq = seq // tb
    query_blocks([0], True)
    n_iter = (nq - 1) // group
    lead = nq - 1 - n_iter * group
    if lead:
        query_blocks(list(range(1, 1 + lead)), False)

    def step(it, carry):
        query_blocks([1 + lead + it * group + g for g in range(group)], False)
        return carry

    lax.fori_loop(0, n_iter, step, 0)


def stick_breaking_attention(q, kv, batch, seq, *, tb=256, rs=128, group=4):
    t, hd = q.shape
    n_heads = hd // HEAD_DIM
    tb = _blk(seq, tb)
    rows = lax.broadcasted_iota(jnp.int32, (tb, tb), 0)
    cols = lax.broadcasted_iota(jnp.int32, (tb, tb), 1)
    tri = (rows > cols).astype(BF16)
    bias = jnp.where(cols < rows, 0.0, MASKED_SCORE).astype(F32)
    vmem = 2 * (4 * seq * HEAD_DIM * 2 + tb * tb * 6) + 16 * tb * tb * 4
    head = lambda b, h: (b, h)
    return pl.pallas_call(
        functools.partial(_attn_kernel, tb=tb, rs=min(rs, tb), group=group, seq=seq),
        grid=(batch, n_heads),
        in_specs=[pl.BlockSpec((seq, HEAD_DIM), head),
                  pl.BlockSpec((seq, HEAD_DIM), head),
                  pl.BlockSpec((seq, HEAD_DIM), lambda b, h: (b, n_heads + h)),
                  pl.BlockSpec((tb, tb), lambda b, h: (0, 0)),
                  pl.BlockSpec((tb, tb), lambda b, h: (0, 0))],
        out_specs=pl.BlockSpec((seq, HEAD_DIM), head),
        out_shape=jax.ShapeDtypeStruct((t, hd), BF16),
        scratch_shapes=[pltpu.VMEM((group, tb, HEAD_DIM), F32), pltpu.VMEM((group, tb, 1), F32)],
        compiler_params=_params(("parallel", "parallel"), vmem),
        name="stick_breaking_attention",
    )(q, kv, kv, tri, bias)


INFO_IDX1, INFO_IDX2, INFO_RANK1, INFO_RANK2, INFO_W1, INFO_W2 = range(6)


def _pack_bf16_pairs(x):
    half = x.shape[1] // 2
    bits = lambda v: lax.bitcast_convert_type(v.astype(BF16).astype(F32), jnp.uint32)
    return (bits(x[:, :half]) >> 16) | (bits(x[:, half:]) & jnp.uint32(0xFFFF0000))


def _unpack_bf16_pairs(words):
    low = lax.bitcast_convert_type(words << 16, F32).astype(BF16)
    high = lax.bitcast_convert_type(words & jnp.uint32(0xFFFF0000), F32).astype(BF16)
    return low, high


def _router_kernel(h_ref, g_ref, wr_ref, br_ref, tri_ref, hn_ref, info_ref, cnt_ref, count_ref):
    @pl.when(pl.program_id(0) == 0)
    def _():
        count_ref[...] = jnp.zeros_like(count_ref)

    x = h_ref[...]
    hn = x * lax.rsqrt(jnp.mean(x * x, axis=-1, keepdims=True) + EPS) * g_ref[...]
    hn_ref[...] = _pack_bf16_pairs(hn)
    logits = jnp.dot(hn, wr_ref[...], precision=lax.Precision.HIGHEST,
                     preferred_element_type=F32) + br_ref[...]
    lane = lax.broadcasted_iota(jnp.int32, logits.shape, 1)
    m1 = jnp.max(logits, axis=1, keepdims=True)
    i1 = jnp.min(jnp.where(logits == m1, lane, LANES), axis=1, keepdims=True)
    rest = jnp.where(lane == i1, -jnp.inf, logits)
    m2 = jnp.max(rest, axis=1, keepdims=True)
    i2 = jnp.min(jnp.where(rest == m2, lane, LANES), axis=1, keepdims=True)
    e2 = jnp.exp(m2 - m1)
    w1 = 1.0 / (1.0 + e2)
    w2 = e2 / (1.0 + e2)
    sel1 = lane == i1
    sel2 = lane == i2
    sel = jnp.logical_or(sel1, sel2).astype(F32)
    before = jnp.dot(tri_ref[...], sel.astype(BF16), preferred_element_type=F32) + count_ref[0:1, :]
    rank1 = jnp.sum(jnp.where(sel1, before, 0.0), axis=1, keepdims=True)
    rank2 = jnp.sum(jnp.where(sel2, before, 0.0), axis=1, keepdims=True)
    count_ref[...] = count_ref[...] + jnp.sum(sel, axis=0, keepdims=True)
    cnt_ref[...] = count_ref[...]
    info = jnp.zeros(logits.shape, F32)
    for slot, val in ((INFO_IDX1, i1.astype(F32)), (INFO_IDX2, i2.astype(F32)), (INFO_RANK1, rank1),
                      (INFO_RANK2, rank2), (INFO_W1, w1), (INFO_W2, w2)):
        info = jnp.where(lane == slot, val, info)
    info_ref[...] = info


def route(h, gain, w_router, b_router, *, bt=512):
    t, d = h.shape
    n_exp = w_router.shape[1]
    bt = _blk(t, bt)
    wr = jnp.zeros((d, LANES), F32).at[:, :n_exp].set(w_router)
    br = jnp.full((1, LANES), -1e30, F32).at[0, :n_exp].set(b_router)
    tri = (lax.broadcasted_iota(jnp.int32, (bt, bt), 1)
           < lax.broadcasted_iota(jnp.int32, (bt, bt), 0)).astype(BF16)
    vmem = 2 * (2 * bt * d * 4 + d * LANES * 4 + bt * bt * 2 + bt * LANES * 4) + 4 * bt * d * 4
    hn, info, counts = pl.pallas_call(
        _router_kernel,
        grid=(t // bt,),
        in_specs=[pl.BlockSpec((bt, d), lambda i: (i, 0)),
                  pl.BlockSpec((1, d), lambda i: (0, 0)),
                  pl.BlockSpec((d, LANES), lambda i: (0, 0)),
                  pl.BlockSpec((1, LANES), lambda i: (0, 0)),
                  pl.BlockSpec((bt, bt), lambda i: (0, 0))],
        out_specs=[pl.BlockSpec((bt, d // 2), lambda i: (i, 0)),
                   pl.BlockSpec((bt, LANES), lambda i: (i, 0)),
                   pl.BlockSpec((8, LANES), lambda i: (0, 0))],
        out_shape=[jax.ShapeDtypeStruct((t, d // 2), jnp.uint32),
                   jax.ShapeDtypeStruct((t, LANES), F32),
                   jax.ShapeDtypeStruct((8, LANES), F32)],
        scratch_shapes=[pltpu.VMEM((8, LANES), F32)],
        compiler_params=_params(("arbitrary",), vmem),
        name="moe_router",
    )(h, gain, wr, br, tri)
    return hn, info, counts[0, :n_exp]


def _gather_rows_kernel(nvalid_ref, tok_ref, tok_next_ref, src_ref, o_ref, buf_ref, sems, *, tc,
                        rows_per_block):
    i = pl.program_id(0)
    rows_valid = nvalid_ref[0] * rows_per_block
    slot = i % 2

    def copy(tok, slot, j):
        return pltpu.make_async_copy(src_ref.at[pl.ds(tok[0, j], 1)], buf_ref.at[slot, pl.ds(j, 1)],
                                     sems.at[slot])

    def issue(tok, slot):
        def pair(j, _):
            copy(tok, slot, 2 * j).start(priority=0)
            copy(tok, slot, 2 * j + 1).start(priority=1)
            return 0
        lax.fori_loop(0, tc // 2, pair, 0)

    @pl.when(jnp.logical_and(i == 0, rows_valid > 0))
    def _():
        issue(tok_ref, 0)

    @pl.when(jnp.logical_and(i + 1 < pl.num_programs(0), (i + 1) * tc < rows_valid))
    def _():
        issue(tok_next_ref, 1 - slot)

    @pl.when(i * tc < rows_valid)
    def _():
        def wait(j, _):
            copy(tok_ref, slot, j).wait()
            return 0
        lax.fori_loop(0, tc, wait, 0)
        half = buf_ref.shape[2]
        o_ref[:, :half], o_ref[:, half:] = _unpack_bf16_pairs(buf_ref[slot])

    @pl.when(i * tc >= rows_valid)
    def _():
        o_ref[...] = jnp.zeros_like(o_ref)


def gather_rows(src, tok_of_row, n_valid, rows_per_block, *, tc=512):
    d = src.shape[1] * 2
    r = tok_of_row.shape[0]
    tc = _blk(r, tc)
    last = r // tc - 1
    vmem = 2 * tc * d * 2 + 2 * tc * d * 2 + tc * d * 4
    tok = tok_of_row.reshape(1, r)
    return pl.pallas_call(
        functools.partial(_gather_rows_kernel, tc=tc, rows_per_block=rows_per_block),
        grid_spec=pltpu.PrefetchScalarGridSpec(
            num_scalar_prefetch=1,
            grid=(r // tc,),
            in_specs=[pl.BlockSpec((1, tc), lambda i, nv: (0, i), memory_space=pltpu.SMEM),
                      pl.BlockSpec((1, tc), lambda i, nv: (0, jnp.minimum(i + 1, last)),
                                   memory_space=pltpu.SMEM),
                      pl.BlockSpec(memory_space=pl.ANY)],
            out_specs=pl.BlockSpec((tc, d), lambda i, nv: (i, 0)),
            scratch_shapes=[pltpu.VMEM((2, tc, d // 2), jnp.uint32), pltpu.SemaphoreType.DMA((2,))]),
        out_shape=jax.ShapeDtypeStruct((r, d), BF16),
        compiler_params=_params(("arbitrary",), vmem),
        name="moe_gather_rows",
    )(n_valid, tok, tok, src)


def _combine_kernel(pos_ref, pos_next_ref, y_ref, w_ref, h_ref, o_ref, hb_ref, rstd_ref, buf_ref, sems, *, tc):
    i = pl.program_id(0)
    slot = i % 2

    def copy(pos, slot, j, k):
        return pltpu.make_async_copy(y_ref.at[pl.ds(pos[k, j], 1)], buf_ref.at[slot, k, pl.ds(j, 1)],
                                     sems.at[slot])

    def issue(pos, slot):
        def token(j, _):
            copy(pos, slot, j, 0).start(priority=0)
            copy(pos, slot, j, 1).start(priority=1)
            return 0
        lax.fori_loop(0, tc, token, 0)

    @pl.when(i == 0)
    def _():
        issue(pos_ref, 0)

    @pl.when(i + 1 < pl.num_programs(0))
    def _():
        issue(pos_next_ref, 1 - slot)

    def wait(j, _):
        copy(pos_ref, slot, j, 0).wait()
        copy(pos_ref, slot, j, 1).wait()
        return 0

    lax.fori_loop(0, tc, wait, 0)
    w = w_ref[...]
    out = (h_ref[...] + w[:, INFO_W1:INFO_W1 + 1] * buf_ref[slot, 0]
           + w[:, INFO_W2:INFO_W2 + 1] * buf_ref[slot, 1])
    o_ref[...] = out
    hb_ref[...] = out.astype(hb_ref.dtype)
    rstd_ref[...] = jnp.broadcast_to(lax.rsqrt(jnp.mean(out * out, axis=1, keepdims=True) + EPS),
                                     rstd_ref.shape)


def combine(y_sorted, pos, info, h, *, tc=256):
    t, d = h.shape
    tc = _blk(t, tc)
    last = t // tc - 1
    vmem = 4 * tc * d * 4 + 2 * (2 * tc * d * 4 + 2 * tc * LANES * 4 + tc * d * 2) + 2 * tc * d * 4
    return pl.pallas_call(
        functools.partial(_combine_kernel, tc=tc),
        grid=(t // tc,),
        in_specs=[pl.BlockSpec((TOP_K, tc), lambda i: (0, i), memory_space=pltpu.SMEM),
                  pl.BlockSpec((TOP_K, tc), lambda i: (0, jnp.minimum(i + 1, last)), memory_space=pltpu.SMEM),
                  pl.BlockSpec(memory_space=pl.ANY),
                  pl.BlockSpec((tc, LANES), lambda i: (i, 0)),
                  pl.BlockSpec((tc, d), lambda i: (i, 0))],
        out_specs=[pl.BlockSpec((tc, d), lambda i: (i, 0)),
                   pl.BlockSpec((tc, d), lambda i: (i, 0)),
                   pl.BlockSpec((tc, LANES), lambda i: (i, 0))],
        out_shape=[jax.ShapeDtypeStruct((t, d), F32),
                   jax.ShapeDtypeStruct((t, d), BF16),
                   jax.ShapeDtypeStruct((t, LANES), F32)],
        scratch_shapes=[pltpu.VMEM((2, TOP_K, tc, d), F32), pltpu.SemaphoreType.DMA((2,))],
        compiler_params=_params(("arbitrary",), vmem),
        name="moe_combine",
    )(pos, pos, y_sorted, info, h)


def _expert_glu_kernel(be_ref, nvalid_ref, x_ref, wg_ref, wu_ref, o_ref):
    del be_ref

    @pl.when(pl.program_id(0) < nvalid_ref[0])
    def _():
        x = x_ref[...]
        g = jnp.dot(x, wg_ref[...], preferred_element_type=F32)
        u = jnp.dot(x, wu_ref[...], preferred_element_type=F32)
        o_ref[...] = (g * jax.nn.sigmoid(g) * u).astype(o_ref.dtype)

    @pl.when(pl.program_id(0) >= nvalid_ref[0])
    def _():
        o_ref[...] = jnp.zeros_like(o_ref)


def expert_glu(x_sorted, w_gu, block_expert, n_valid, *, bm, bn=512):
    r, d = x_sorted.shape
    f = w_gu.shape[2] // 2
    bn = _blk(f, bn)
    nf = f // bn
    vmem = 2 * (bm * d * 2 + 2 * d * bn * 2 + bm * bn * 2) + 3 * bm * bn * 4
    return pl.pallas_call(
        _expert_glu_kernel,
        grid_spec=pltpu.PrefetchScalarGridSpec(
            num_scalar_prefetch=2,
            grid=(r // bm, nf),
            in_specs=[pl.BlockSpec((bm, d), lambda i, j, be, nv: (i, 0)),
                      pl.BlockSpec((None, d, bn), lambda i, j, be, nv: (be[i], 0, j)),
                      pl.BlockSpec((None, d, bn), lambda i, j, be, nv: (be[i], 0, j + nf))],
            out_specs=pl.BlockSpec((bm, bn), lambda i, j, be, nv: (i, j))),
        out_shape=jax.ShapeDtypeStruct((r, f), BF16),
        compiler_params=_params(("arbitrary", "arbitrary"), vmem),
        name="moe_expert_glu",
    )(block_expert, n_valid, x_sorted, w_gu, w_gu)


def _expert_down_kernel(be_ref, nvalid_ref, x_ref, w_ref, o_ref):
    del be_ref

    @pl.when(pl.program_id(0) < nvalid_ref[0])
    def _():
        o_ref[...] = jnp.dot(x_ref[...], w_ref[...], preferred_element_type=F32)

    @pl.when(pl.program_id(0) >= nvalid_ref[0])
    def _():
        o_ref[...] = jnp.zeros_like(o_ref)


def expert_down(a_sorted, w_down, block_expert, n_valid, *, bm, bn=1024):
    r, f = a_sorted.shape
    d = w_down.shape[2]
    bn = _blk(d, bn)
    vmem = 2 * (bm * f * 2 + f * bn * 2 + bm * bn * 4) + bm * bn * 4
    return pl.pallas_call(
        _expert_down_kernel,
        grid_spec=pltpu.PrefetchScalarGridSpec(
            num_scalar_prefetch=2,
            grid=(r // bm, d // bn),
            in_specs=[pl.BlockSpec((bm, f), lambda i, j, be, nv: (i, 0)),
                      pl.BlockSpec((None, f, bn), lambda i, j, be, nv: (be[i], 0, j))],
            out_specs=pl.BlockSpec((bm, bn), lambda i, j, be, nv: (i, j))),
        out_shape=jax.ShapeDtypeStruct((r, d), F32),
        compiler_params=_params(("arbitrary", "arbitrary"), vmem),
        name="moe_expert_down",
    )(block_expert, n_valid, a_sorted, w_down)


def moe(h, gain, w_router, b_router, w_gu, w_down, *, bm=512):
    t, d = h.shape
    n_exp = w_router.shape[1]
    bm = _blk(t, bm)
    hn, info, counts = route(h, gain, w_router, b_router)
    counts = counts.astype(jnp.int32)
    blocks_per_expert = (counts + bm - 1) // bm
    block_end = jnp.cumsum(blocks_per_expert)
    row_offset = (block_end - blocks_per_expert) * bm
    n_blocks = TOP_K * t // bm + n_exp
    n_valid = block_end[-1:]
    block_ids = jnp.minimum(jnp.arange(n_blocks, dtype=jnp.int32), n_valid[0] - 1)
    block_expert = jnp.sum((block_ids[:, None] >= block_end[None, :]).astype(jnp.int32), axis=1)
    idx = info[:, INFO_IDX1:INFO_IDX2 + 1].astype(jnp.int32)
    rank = info[:, INFO_RANK1:INFO_RANK2 + 1].astype(jnp.int32)
    pos = (row_offset[idx] + rank).T
    token_ids = jnp.broadcast_to(jnp.arange(t, dtype=jnp.int32), (TOP_K, t))
    tok_of_row = jnp.zeros((n_blocks * bm,), jnp.int32).at[pos.reshape(-1)].set(
        token_ids.reshape(-1), unique_indices=True)
    x_sorted = gather_rows(hn, tok_of_row, n_valid, bm)
    a_sorted = expert_glu(x_sorted, w_gu, block_expert, n_valid, bm=bm)
    y_sorted = expert_down(a_sorted, w_down, block_expert, n_valid, bm=bm)
    return combine(y_sorted, pos, info, h)


def kernel(x, p, a_norm, a_w_in, a_conv_w, a_w_out, kv_norm, w_kv, b_norm, b_w_q, b_w_o, ffn_norm, dense_w_gu, dense_w_down, moe_w_router, moe_b_router, moe_w_gu, moe_w_down, ple_norm, ple_w_up, ple_w_gate, final_norm):
    batch, seq, d = x.shape
    depth = p.shape[0]
    n_a = a_norm.shape[0]
    t = batch * seq
    p = p.reshape(depth, t, p.shape[-1])
    bf = lambda w: w.astype(BF16)

    h = x.reshape(t, d)
    hb, rstd = stream_operands(h)
    kv = None

    to_convert = [(name, m, w[m].reshape(-1, w.shape[-1]))
                  for m in range(moe_w_gu.shape[0]) for name, w in (("down", moe_w_down), ("gu", moe_w_gu))]
    converted = {}

    def hosting(*args, **kwargs):
        if not to_convert:
            return normed_matmul(*args, **kwargs)
        name, m, w2d = to_convert[0]
        out, wb = normed_matmul(*args, cast=w2d, **kwargs)
        if wb is not None:
            converted[name, m] = wb
            to_convert.pop(0)
        return out

    def expert_weights(name, w, m):
        to_convert[:] = [job for job in to_convert if job[:2] != (name, m)]
        wb = converted.get((name, m))
        return bf(w[m]) if wb is None else wb.reshape(w.shape[1:])
    for i in range(depth):
        if i < n_a:
            bcx = hosting(hb, rstd, a_norm[i], a_w_in[i])
            mixed, w_out = short_conv(bcx, a_conv_w[i], seq), a_w_out[i]
        else:
            j = i - n_a
            if j == 0:
                kv = hosting(hb, rstd, kv_norm, w_kv)
            q = hosting(hb, rstd, b_norm[j], b_w_q[j], scale=HEAD_DIM ** -0.5 * LOG2E)
            mixed, w_out = stick_breaking_attention(q, kv, batch, seq), b_w_o[j]
        if i % 2 == 0:
            h, hb, rstd = resid_matmul(mixed, bf(w_out), h, emit_norm=True)
            act = hosting(hb, rstd, ffn_norm[i], dense_w_gu[i // 2], glu=True, bn=256)
            h, hb, rstd = resid_matmul(act, bf(dense_w_down[i // 2]), h, emit_norm=True, bm=512)
        else:
            m = i // 2
            h = resid_matmul(mixed, bf(w_out), h, emit_norm=False)
            h, hb, rstd = moe(h, ffn_norm[i:i + 1], moe_w_router[m], moe_b_router[m],
                              expert_weights("gu", moe_w_gu, m), expert_weights("down", moe_w_down, m))
        w_gate = bf(ple_w_gate[i] * ple_norm[i][:, None])
        if i + 1 < depth:
            h, hb, rstd = ple(hb, rstd, w_gate, p[i], bf(ple_w_up[i]), h, emit_norm=True)
        else:
            h = ple(hb, rstd, w_gate, p[i], bf(ple_w_up[i]), h, emit_norm=False)
    (out,) = rmsnorm(h, final_norm[None, :], F32)
    return out.reshape(batch, seq, d)
```

```python
import functools

import jax
import jax.numpy as jnp
from jax import lax
from jax.experimental import pallas as pl
from jax.experimental.pallas import tpu as pltpu

F32 = jnp.float32
BF16 = jnp.bfloat16

EPS = 1e-6
HEAD_DIM = 128
CONV_WIDTH = 3
TOP_K = 2
LANES = 128
BF16_SUBLANES = 16
MXU_COLUMNS = 256
VMEM_BUDGET_V7X = 60000 * 1024
VMEM_SLACK = 8 * 1024 * 1024
CAST_SLAB_BYTES = 4 * 1024 * 1024
LOG2E = 1.4426950408889634
EXP_ZERO_BELOW = -110.0
MASKED_SCORE = -1e30


def _params(semantics, block_bytes):
    limit = min(VMEM_BUDGET_V7X, int(block_bytes) + VMEM_SLACK)
    return pltpu.CompilerParams(dimension_semantics=semantics, vmem_limit_bytes=limit)


def _blk(dim, pref):
    b = min(dim, pref)
    while dim % b:
        b //= 2
    return b


def _rmsnorm_kernel(x_ref, g_ref, *o_refs):
    x = x_ref[...]
    y = x * lax.rsqrt(jnp.mean(x * x, axis=-1, keepdims=True) + EPS)
    for n, o_ref in enumerate(o_refs):
        o_ref[...] = (y * g_ref[n:n + 1, :]).astype(o_ref.dtype)


def rmsnorm(x, gains, out_dtype):
    t, d = x.shape
    g = gains.shape[0]
    bt = _blk(t, 256)
    out_bytes = jnp.dtype(out_dtype).itemsize
    outs = pl.pallas_call(
        _rmsnorm_kernel,
        grid=(t // bt,),
        in_specs=[pl.BlockSpec((bt, d), lambda i: (i, 0)),
                  pl.BlockSpec((g, d), lambda i: (0, 0))],
        out_specs=[pl.BlockSpec((bt, d), lambda i: (i, 0))] * g,
        out_shape=[jax.ShapeDtypeStruct((t, d), out_dtype)] * g,
        compiler_params=_params(("parallel",), 2 * bt * d * (4 + g * out_bytes)),
        name="rmsnorm",
    )(x, gains)
    return outs


def _stream_kernel(x_ref, xb_ref, rstd_ref):
    x = x_ref[...]
    xb_ref[...] = x.astype(xb_ref.dtype)
    rstd_ref[...] = jnp.broadcast_to(lax.rsqrt(jnp.mean(x * x, axis=-1, keepdims=True) + EPS), rstd_ref.shape)


def stream_operands(x):
    t, d = x.shape
    bt = _blk(t, 256)
    return pl.pallas_call(
        _stream_kernel,
        grid=(t // bt,),
        in_specs=[pl.BlockSpec((bt, d), lambda i: (i, 0))],
        out_specs=[pl.BlockSpec((bt, d), lambda i: (i, 0)), pl.BlockSpec((bt, LANES), lambda i: (i, 0))],
        out_shape=[jax.ShapeDtypeStruct((t, d), BF16), jax.ShapeDtypeStruct((t, LANES), F32)],
        compiler_params=_params(("parallel",), 2 * bt * (d * 6 + LANES * 4) + bt * d * 4),
        name="stream_operands",
    )(x)


def _emit_stream(h, j, last_j, width, hb_ref, rstd_ref, ssq_ref):
    hb_ref[...] = h.astype(hb_ref.dtype)
    _emit_rstd(jnp.sum(h * h, axis=1, keepdims=True), j, last_j, width, rstd_ref, ssq_ref)


def _emit_rstd(part, j, last_j, width, rstd_ref, ssq_ref):
    @pl.when(j == 0)
    def _():
        ssq_ref[...] = part

    @pl.when(j > 0)
    def _():
        ssq_ref[...] += part

    @pl.when(j == last_j)
    def _():
        rstd_ref[...] = jnp.broadcast_to(lax.rsqrt(ssq_ref[...] * (1.0 / width) + EPS), rstd_ref.shape)


def _resid_mm_kernel(x_ref, w_ref, r_ref, o_ref, *rest, nn, width):
    out = r_ref[...] + jnp.dot(x_ref[...], w_ref[...], preferred_element_type=F32)
    o_ref[...] = out
    if rest:
        _emit_stream(out, pl.program_id(1), nn - 1, width, *rest)


def resid_matmul(x, w, resid, *, emit_norm, bm=1024, bn=512):
    m, kdim = x.shape
    n = w.shape[1]
    bm, bn = _blk(m, bm), _blk(n, bn)
    nn = n // bn
    tile = pl.BlockSpec((bm, bn), lambda i, j: (i, j))
    rows = pl.BlockSpec((bm, LANES), lambda i, j: (i, 0))
    vmem = 2 * (bm * kdim + kdim * bn) * 2 + 4 * bm * bn * 4 + bm * bn * 4
    out_specs = [tile]
    out_shape = [jax.ShapeDtypeStruct((m, n), F32)]
    scratch = []
    if emit_norm:
        out_specs += [tile, rows]
        out_shape += [jax.ShapeDtypeStruct((m, n), BF16), jax.ShapeDtypeStruct((m, LANES), F32)]
        scratch.append(pltpu.VMEM((bm, 1), F32))
        vmem += 2 * bm * bn * 2 + 3 * bm * LANES * 4
    outs = pl.pallas_call(
        functools.partial(_resid_mm_kernel, nn=nn, width=n),
        grid=(m // bm, nn),
        in_specs=[pl.BlockSpec((bm, kdim), lambda i, j: (i, 0)),
                  pl.BlockSpec((kdim, bn), lambda i, j: (0, j)),
                  tile],
        out_specs=out_specs,
        out_shape=out_shape,
        scratch_shapes=scratch,
        compiler_params=_params(("parallel", "arbitrary" if emit_norm else "parallel"), vmem),
        name="resid_matmul",
    )(x, w, resid)
    return outs if emit_norm else outs[0]


def _normed_kernel(x_ref, rs_ref, g_ref, *refs, n_w, scale, has_cast):
    w_refs, refs = refs[:n_w], refs[n_w:]
    if has_cast:
        cast_ref, o_ref, cast_o_ref = refs[:3]
        cast_o_ref[...] = cast_ref[...].astype(cast_o_ref.dtype)
        wb_refs = refs[3:]
    else:
        o_ref, wb_refs = refs[0], refs[1:]

    @pl.when(pl.program_id(1) == 0)
    def _():
        gain = g_ref[...]
        for w_ref, wb_ref in zip(w_refs, wb_refs):
            wb_ref[...] = (w_ref[...] * gain).astype(BF16)

    x = x_ref[...]
    rstd = rs_ref[:, 0:1]
    outs = [jnp.dot(x, wb_ref[...], preferred_element_type=F32) * rstd for wb_ref in wb_refs]
    if n_w == 2:
        g, u = outs
        out = g * jax.nn.sigmoid(g) * u
    else:
        out = outs[0] if scale is None else outs[0] * scale
    o_ref[...] = out.astype(o_ref.dtype)


def normed_matmul(x, rstd, gain, w, *, glu=False, scale=None, cast=None, bm=1024, bn=512):
    m, kdim = x.shape
    n_w = 2 if glu else 1
    n = w.shape[1] // n_w
    bm, bn = _blk(m, bm), _blk(n, bn)
    nn, nm = n // bn, m // bm
    w_specs = [pl.BlockSpec((kdim, bn), lambda j, i, off=off: (0, j + off)) for off in range(0, n_w * nn, nn)]
    vmem = (2 * bm * kdim * 2 + n_w * kdim * bn * (2 * 4 + 2) + 2 * bm * bn * 2 + 2 * kdim * LANES * 4
            + 2 * bm * LANES * 4 + (n_w + 1) * bm * bn * 4)
    in_specs = [pl.BlockSpec((bm, kdim), lambda j, i: (i, 0)),
                pl.BlockSpec((bm, LANES), lambda j, i: (i, 0)),
                pl.BlockSpec((kdim, 1), lambda j, i: (0, 0))] + w_specs
    args = [x, rstd, gain.reshape(kdim, 1)] + [w] * n_w
    out_specs = [pl.BlockSpec((bm, bn), lambda j, i: (i, j))]
    out_shape = [jax.ShapeDtypeStruct((m, n), BF16)]
    slab = _cast_slab(cast, nn * nm) if cast is not None else None
    if slab is not None:
        rows, cols = cast.shape
        last = rows // slab - 1
        side = pl.BlockSpec((slab, cols), lambda j, i: (jnp.minimum(j * nm + i, last), 0))
        in_specs.append(side)
        args.append(cast)
        out_specs.append(side)
        out_shape.append(jax.ShapeDtypeStruct((rows, cols), BF16))
        vmem += 2 * slab * cols * (4 + 2)
    outs = pl.pallas_call(
        functools.partial(_normed_kernel, n_w=n_w, scale=scale, has_cast=slab is not None),
        grid=(nn, nm),
        in_specs=in_specs,
        out_specs=out_specs,
        out_shape=out_shape,
        scratch_shapes=[pltpu.VMEM((kdim, bn), BF16)] * n_w,
        compiler_params=_params(("arbitrary", "arbitrary"), vmem),
        name="normed_glu" if glu else "normed_matmul",
    )(*args)
    return (outs[0], outs[1] if slab is not None else None) if cast is not None else outs[0]


def _cast_slab(a, steps):
    rows = a.shape[0]
    slab = BF16_SUBLANES
    while slab <= rows and (rows % slab or rows // slab > steps):
        slab *= 2
    return slab if slab <= rows and slab * a.shape[1] * 4 <= CAST_SLAB_BYTES else None


def _conv_kernel(b_ref, c_ref, u_ref, ch_ref, uh_ref, w_ref, o_ref, *, blocks_per_seq):
    ts = c_ref.shape[0]
    cu = c_ref[...].astype(F32) * u_ref[...].astype(F32)
    seq_start = (pl.program_id(0) % blocks_per_seq) == 0
    halo = ch_ref[...].astype(F32) * uh_ref[...].astype(F32)
    halo = jnp.where(seq_start, 0.0, halo)
    row = lax.broadcasted_iota(jnp.int32, cu.shape, 0)
    h1 = halo[BF16_SUBLANES - 1:BF16_SUBLANES, :]
    h2 = halo[BF16_SUBLANES - 2:BF16_SUBLANES - 1, :]
    prev1 = jnp.where(row == 0, h1, pltpu.roll(cu, 1, 0))
    prev2 = jnp.where(row == 0, h2, jnp.where(row == 1, h1, pltpu.roll(cu, 2, 0)))
    y = w_ref[0:1, :] * prev2 + w_ref[1:2, :] * prev1 + w_ref[2:3, :] * cu
    o_ref[...] = (b_ref[...].astype(F32) * y).astype(o_ref.dtype)


def short_conv(bcx, conv_w, seq):
    t, d3 = bcx.shape
    d = d3 // 3
    ts = _blk(seq, 512)
    dblk = _blk(d, 1024)
    nd = d // dblk
    hb = ts // BF16_SUBLANES

    def halo_map(off):
        return lambda i, j: (jnp.maximum(i * hb - 1, 0), j + off)

    vmem = 2 * (4 * ts * dblk * 2 + 2 * BF16_SUBLANES * dblk * 2) + 6 * ts * dblk * 4
    return pl.pallas_call(
        functools.partial(_conv_kernel, blocks_per_seq=seq // ts),
        grid=(t // ts, nd),
        in_specs=[pl.BlockSpec((ts, dblk), lambda i, j: (i, j)),
                  pl.BlockSpec((ts, dblk), lambda i, j: (i, j + nd)),
                  pl.BlockSpec((ts, dblk), lambda i, j: (i, j + 2 * nd)),
                  pl.BlockSpec((BF16_SUBLANES, dblk), halo_map(nd)),
                  pl.BlockSpec((BF16_SUBLANES, dblk), halo_map(2 * nd)),
                  pl.BlockSpec((CONV_WIDTH, dblk), lambda i, j: (0, j))],
        out_specs=pl.BlockSpec((ts, dblk), lambda i, j: (i, j)),
        out_shape=jax.ShapeDtypeStruct((t, d), BF16),
        compiler_params=_params(("parallel", "parallel"), vmem),
        name="short_conv",
    )(bcx, bcx, bcx, bcx, bcx, conv_w)


def _ple_kernel(hb_ref, wg_ref, p_ref, wu_ref, h_ref, rs_ref, o_ref, *rest, nn, width):
    hb, pb, rstd = hb_ref[...], p_ref[...].astype(BF16), rs_ref[:, 0:1]
    bn = o_ref.shape[1]
    halves = [slice(0, bn // 2), slice(bn // 2, bn)] if bn % (2 * MXU_COLUMNS) == 0 else [slice(0, bn)]
    dots = [(jnp.dot(hb, wg_ref[:, c], preferred_element_type=F32),
             jnp.dot(pb, wu_ref[:, c], preferred_element_type=F32)) for c in halves]
    ssq = None
    for c, (pre, up) in zip(halves, dots):
        out = h_ref[:, c] + up * jax.nn.sigmoid(pre * rstd)
        o_ref[:, c] = out
        if rest:
            rest[0][:, c] = out.astype(rest[0].dtype)
            part = jnp.sum(out * out, axis=1, keepdims=True)
            ssq = part if ssq is None else ssq + part
    if rest:
        _emit_rstd(ssq, pl.program_id(1), nn - 1, width, *rest[1:])


def ple(hb, rstd, w_gate, p, w_up, h, *, emit_norm, bm=1024, bn=512):
    m, d = hb.shape
    pd = p.shape[1]
    n = w_gate.shape[1]
    bm, bn = _blk(m, bm), _blk(n, bn)
    nn = n // bn
    tile = pl.BlockSpec((bm, bn), lambda i, j: (i, j))
    rows = pl.BlockSpec((bm, LANES), lambda i, j: (i, 0))
    vmem = (2 * (bm * d * 2 + d * bn * 2 + bm * pd * 4 + pd * bn * 2 + 2 * bm * bn * 4 + bm * LANES * 4)
            + 3 * bm * bn * 4)
    out_specs, out_shape, scratch = [tile], [jax.ShapeDtypeStruct((m, n), F32)], []
    if emit_norm:
        out_specs += [tile, rows]
        out_shape += [jax.ShapeDtypeStruct((m, n), BF16), jax.ShapeDtypeStruct((m, LANES), F32)]
        scratch.append(pltpu.VMEM((bm, 1), F32))
        vmem += 2 * bm * bn * 2 + 3 * bm * LANES * 4
    outs = pl.pallas_call(
        functools.partial(_ple_kernel, nn=nn, width=n),
        grid=(m // bm, nn),
        in_specs=[pl.BlockSpec((bm, d), lambda i, j: (i, 0)),
                  pl.BlockSpec((d, bn), lambda i, j: (0, j)),
                  pl.BlockSpec((bm, pd), lambda i, j: (i, 0)),
                  pl.BlockSpec((pd, bn), lambda i, j: (0, j)),
                  tile, rows],
        out_specs=out_specs,
        out_shape=out_shape,
        scratch_shapes=scratch,
        compiler_params=_params(("parallel", "arbitrary" if emit_norm else "parallel"), vmem),
        name="ple",
    )(hb, w_gate, p, w_up, h, rstd)
    return outs if emit_norm else outs[0]


def _attn_kernel(q_ref, k_ref, v_ref, tri_ref, bias_ref, o_ref, acc_ref, gone_ref, *, tb, rs, group, seq):
    gone_limit = -EXP_ZERO_BELOW * LOG2E

    def scores(q, k_start, nk, bias):
        z = lax.dot_general(q, k_ref[pl.ds(k_start, nk), :], (((1,), (1,)), ((), ())),
                            preferred_element_type=F32)
        return z if bias is None else z + bias

    def split(z):
        sp = jnp.maximum(z, 0.0) + jnp.log(1.0 + jnp.exp2(-jnp.abs(z))) * LOG2E
        hi = sp.astype(BF16)
        return sp, hi, (sp - hi.astype(F32)).astype(BF16)

    def suffix_sum(hi, lo, nk):
        tri = tri_ref[0:nk, 0:nk]
        return jnp.dot(hi, tri, preferred_element_type=F32) + jnp.dot(lo, tri, preferred_element_type=F32)

    def weights(z, sp, suffix, gone):
        log2_w = z - sp - suffix
        walked = suffix[:, 0:1] + sp[:, 0:1]
        if gone is not None:
            log2_w, walked = log2_w - gone, walked + gone
        return jnp.exp2(log2_w).astype(BF16), walked

    def values(w, k_start, nk):
        return jnp.dot(w, v_ref[pl.ds(k_start, nk), :], preferred_element_type=F32)

    def tile(q, k_start, nk, gone, bias):
        z = scores(q, k_start, nk, bias)
        sp, hi, lo = split(z)
        w, gone = weights(z, sp, suffix_sum(hi, lo, nk), gone)
        return values(w, k_start, nk), gone

    def query_blocks(qis, first):
        q_starts = [pl.multiple_of(qi * tb, tb) for qi in qis]
        ns = tb // rs
        units = [(b, s, True) for b in range(len(qis)) for s in range(ns)]
        if not first:
            units += [(b, s, False) for b in range(len(qis)) for s in range(ns)]
        live = [None] * len(units)
        gone_of, pv_of = {}, {}

        def run(stage, u):
            b, s, diagonal = units[u]
            rows = slice(s * rs, (s + 1) * rs)
            nk = (s + 1) * rs if diagonal else tb
            k_start = q_starts[b] if diagonal else pl.multiple_of(q_starts[b] - tb, tb)
            if stage == 0:
                q = q_ref[pl.ds(q_starts[b] + s * rs, rs), :]
                live[u] = (scores(q, k_start, nk, bias_ref[rows, 0:nk] if diagonal else None),)
            elif stage == 1:
                (z,) = live[u]
                live[u] = (z,) + split(z)
            elif stage == 2:
                z, sp, hi, lo = live[u]
                live[u] = (z, sp, suffix_sum(hi, lo, nk))
            elif stage == 3:
                z, sp, suffix = live[u]
                w, gone_of[b, s] = weights(z, sp, suffix, gone_of.get((b, s)))
                live[u] = (w,)
            else:
                (w,) = live[u]
                live[u] = None
                pv = values(w, k_start, nk)
                if first:
                    o_ref[pl.ds(q_starts[b] + s * rs, rs), :] = pv.astype(o_ref.dtype)
                elif diagonal:
                    pv_of[b, s] = pv
                else:
                    acc_ref[b, rows, :] = pv_of.pop((b, s)) + pv
                    gone_ref[b, rows, :] = gone_of[b, s]

        n_stages = 5
        for step in range(len(units) + n_stages - 1):
            for stage in range(n_stages):
                if 0 <= step - stage < len(units):
                    run(stage, step - stage)
        if first:
            return

        for b, qi in enumerate(qis):
            def cond(carry):
                kb, alive = carry
                return jnp.logical_and(kb >= 0, alive > 0)

            def body(carry, b=b):
                kb, _ = carry
                pv_far, gone_far = tile(q_ref[pl.ds(q_starts[b], tb), :], pl.multiple_of(kb * tb, tb), tb,
                                        gone_ref[b], None)
                acc_ref[b] += pv_far
                gone_ref[b] = gone_far
                return kb - 1, (jnp.min(gone_far) < gone_limit).astype(jnp.int32)

            lax.while_loop(cond, body, (qi - 2, (jnp.min(gone_ref[b]) < gone_limit).astype(jnp.int32)))
            o_ref[pl.ds(q_starts[b], tb), :] = acc_ref[b].astype(o_ref.dtype)

    nq = seq // tb
    query_blocks([0], True)
    n_iter = (nq - 1) // group
    lead = nq - 1 - n_iter * group
    if lead:
        query_blocks(list(range(1, 1 + lead)), False)

    def step(it, carry):
        query_blocks([1 + lead + it * group + g for g in range(group)], False)
        return carry

    lax.fori_loop(0, n_iter, step, 0)


def stick_breaking_attention(q, kv, batch, seq, *, tb=256, rs=128, group=4):
    t, hd = q.shape
    n_heads = hd // HEAD_DIM
    tb = _blk(seq, tb)
    rows = lax.broadcasted_iota(jnp.int32, (tb, tb), 0)
    cols = lax.broadcasted_iota(jnp.int32, (tb, tb), 1)
    tri = (rows > cols).astype(BF16)
    bias = jnp.where(cols < rows, 0.0, MASKED_SCORE).astype(F32)
    vmem = 2 * (4 * seq * HEAD_DIM * 2 + tb * tb * 6) + 16 * tb * tb * 4
    head = lambda b, h: (b, h)
    return pl.pallas_call(
        functools.partial(_attn_kernel, tb=tb, rs=min(rs, tb), group=group, seq=seq),
        grid=(batch, n_heads),
        in_specs=[pl.BlockSpec((seq, HEAD_DIM), head),
                  pl.BlockSpec((seq, HEAD_DIM), head),
                  pl.BlockSpec((seq, HEAD_DIM), lambda b, h: (b, n_heads + h)),
                  pl.BlockSpec((tb, tb), lambda b, h: (0, 0)),
                  pl.BlockSpec((tb, tb), lambda b, h: (0, 0))],
        out_specs=pl.BlockSpec((seq, HEAD_DIM), head),
        out_shape=jax.ShapeDtypeStruct((t, hd), BF16),
        scratch_shapes=[pltpu.VMEM((group, tb, HEAD_DIM), F32), pltpu.VMEM((group, tb, 1), F32)],
        compiler_params=_params(("parallel", "parallel"), vmem),
        name="stick_breaking_attention",
    )(q, kv, kv, tri, bias)


INFO_IDX1, INFO_IDX2, INFO_RANK1, INFO_RANK2, INFO_W1, INFO_W2 = range(6)


def _pack_bf16_pairs(x):
    half = x.shape[1] // 2
    bits = lambda v: lax.bitcast_convert_type(v.astype(BF16).astype(F32), jnp.uint32)
    return (bits(x[:, :half]) >> 16) | (bits(x[:, half:]) & jnp.uint32(0xFFFF0000))


def _unpack_bf16_pairs(words):
    low = lax.bitcast_convert_type(words << 16, F32).astype(BF16)
    high = lax.bitcast_convert_type(words & jnp.uint32(0xFFFF0000), F32).astype(BF16)
    return low, high


def _router_kernel(h_ref, g_ref, wr_ref, br_ref, tri_ref, hn_ref, info_ref, cnt_ref, count_ref):
    @pl.when(pl.program_id(0) == 0)
    def _():
        count_ref[...] = jnp.zeros_like(count_ref)

    x = h_ref[...]
    hn = x * lax.rsqrt(jnp.mean(x * x, axis=-1, keepdims=True) + EPS) * g_ref[...]
    hn_ref[...] = _pack_bf16_pairs(hn)
    logits = jnp.dot(hn, wr_ref[...], precision=lax.Precision.HIGHEST,
                     preferred_element_type=F32) + br_ref[...]
    lane = lax.broadcasted_iota(jnp.int32, logits.shape, 1)
    m1 = jnp.max(logits, axis=1, keepdims=True)
    i1 = jnp.min(jnp.where(logits == m1, lane, LANES), axis=1, keepdims=True)
    rest = jnp.where(lane == i1, -jnp.inf, logits)
    m2 = jnp.max(rest, axis=1, keepdims=True)
    i2 = jnp.min(jnp.where(rest == m2, lane, LANES), axis=1, keepdims=True)
    e2 = jnp.exp(m2 - m1)
    w1 = 1.0 / (1.0 + e2)
    w2 = e2 / (1.0 + e2)
    sel1 = lane == i1
    sel2 = lane == i2
    sel = jnp.logical_or(sel1, sel2).astype(F32)
    before = jnp.dot(tri_ref[...], sel.astype(BF16), preferred_element_type=F32) + count_ref[0:1, :]
    rank1 = jnp.sum(jnp.where(sel1, before, 0.0), axis=1, keepdims=True)
    rank2 = jnp.sum(jnp.where(sel2, before, 0.0), axis=1, keepdims=True)
    count_ref[...] = count_ref[...] + jnp.sum(sel, axis=0, keepdims=True)
    cnt_ref[...] = count_ref[...]
    info = jnp.zeros(logits.shape, F32)
    for slot, val in ((INFO_IDX1, i1.astype(F32)), (INFO_IDX2, i2.astype(F32)), (INFO_RANK1, rank1),
                      (INFO_RANK2, rank2), (INFO_W1, w1), (INFO_W2, w2)):
        info = jnp.where(lane == slot, val, info)
    info_ref[...] = info


def route(h, gain, w_router, b_router, *, bt=512):
    t, d = h.shape
    n_exp = w_router.shape[1]
    bt = _blk(t, bt)
    wr = jnp.zeros((d, LANES), F32).at[:, :n_exp].set(w_router)
    br = jnp.full((1, LANES), -1e30, F32).at[0, :n_exp].set(b_router)
    tri = (lax.broadcasted_iota(jnp.int32, (bt, bt), 1)
           < lax.broadcasted_iota(jnp.int32, (bt, bt), 0)).astype(BF16)
    vmem = 2 * (2 * bt * d * 4 + d * LANES * 4 + bt * bt * 2 + bt * LANES * 4) + 4 * bt * d * 4
    hn, info, counts = pl.pallas_call(
        _router_kernel,
        grid=(t // bt,),
        in_specs=[pl.BlockSpec((bt, d), lambda i: (i, 0)),
                  pl.BlockSpec((1, d), lambda i: (0, 0)),
                  pl.BlockSpec((d, LANES), lambda i: (0, 0)),
                  pl.BlockSpec((1, LANES), lambda i: (0, 0)),
                  pl.BlockSpec((bt, bt), lambda i: (0, 0))],
        out_specs=[pl.BlockSpec((bt, d // 2), lambda i: (i, 0)),
                   pl.BlockSpec((bt, LANES), lambda i: (i, 0)),
                   pl.BlockSpec((8, LANES), lambda i: (0, 0))],
        out_shape=[jax.ShapeDtypeStruct((t, d // 2), jnp.uint32),
                   jax.ShapeDtypeStruct((t, LANES), F32),
                   jax.ShapeDtypeStruct((8, LANES), F32)],
        scratch_shapes=[pltpu.VMEM((8, LANES), F32)],
        compiler_params=_params(("arbitrary",), vmem),
        name="moe_router",
    )(h, gain, wr, br, tri)
    return hn, info, counts[0, :n_exp]


def _combine_kernel(pos_ref, pos_next_ref, y_ref, w_ref, h_ref, o_ref, hb_ref, rstd_ref, buf_ref, sems, *, tc):
    i = pl.program_id(0)
    slot = i % 2

    def copy(pos, slot, j, k):
        return pltpu.make_async_copy(y_ref.at[pl.ds(pos[k, j], 1)], buf_ref.at[slot, k, pl.ds(j, 1)],
                                     sems.at[slot])

    def issue(pos, slot):
        def token(j, _):
            copy(pos, slot, j, 0).start(priority=0)
            copy(pos, slot, j, 1).start(priority=1)
            return 0
        lax.fori_loop(0, tc, token, 0)

    @pl.when(i == 0)
    def _():
        issue(pos_ref, 0)

    @pl.when(i + 1 < pl.num_programs(0))
    def _():
        issue(pos_next_ref, 1 - slot)

    def wait(j, _):
        copy(pos_ref, slot, j, 0).wait()
        copy(pos_ref, slot, j, 1).wait()
        return 0

    lax.fori_loop(0, tc, wait, 0)
    w = w_ref[...]
    out = (h_ref[...] + w[:, INFO_W1:INFO_W1 + 1] * buf_ref[slot, 0]
           + w[:, INFO_W2:INFO_W2 + 1] * buf_ref[slot, 1])
    o_ref[...] = out
    hb_ref[...] = out.astype(hb_ref.dtype)
    rstd_ref[...] = jnp.broadcast_to(lax.rsqrt(jnp.mean(out * out, axis=1, keepdims=True) + EPS),
                                     rstd_ref.shape)


def combine(y_sorted, pos, info, h, *, tc=256):
    t, d = h.shape
    tc = _blk(t, tc)
    last = t // tc - 1
    vmem = 4 * tc * d * 4 + 2 * (2 * tc * d * 4 + 2 * tc * LANES * 4 + tc * d * 2) + 2 * tc * d * 4
    return pl.pallas_call(
        functools.partial(_combine_kernel, tc=tc),
        grid=(t // tc,),
        in_specs=[pl.BlockSpec((TOP_K, tc), lambda i: (0, i), memory_space=pltpu.SMEM),
                  pl.BlockSpec((TOP_K, tc), lambda i: (0, jnp.minimum(i + 1, last)), memory_space=pltpu.SMEM),
                  pl.BlockSpec(memory_space=pl.ANY),
                  pl.BlockSpec((tc, LANES), lambda i: (i, 0)),
                  pl.BlockSpec((tc, d), lambda i: (i, 0))],
        out_specs=[pl.BlockSpec((tc, d), lambda i: (i, 0)),
                   pl.BlockSpec((tc, d), lambda i: (i, 0)),
                   pl.BlockSpec((tc, LANES), lambda i: (i, 0))],
        out_shape=[jax.ShapeDtypeStruct((t, d), F32),
                   jax.ShapeDtypeStruct((t, d), BF16),
                   jax.ShapeDtypeStruct((t, LANES), F32)],
        scratch_shapes=[pltpu.VMEM((2, TOP_K, tc, d), F32), pltpu.SemaphoreType.DMA((2,))],
        compiler_params=_params(("arbitrary",), vmem),
        name="moe_combine",
    )(pos, pos, y_sorted, info, h)


def _expert_glu_kernel(be_ref, nvalid_ref, tok_ref, src_ref, wg_ref, wu_ref, o_ref, buf_ref, x_ref, sems, *,
                       bm, n_blocks):
    del be_ref
    i, j, nf = pl.program_id(0), pl.program_id(1), pl.num_programs(1)
    n_valid = nvalid_ref[0]
    slot = i % 2
    share = bm // nf
    nxt = jnp.minimum(i + 1, n_blocks - 1)

    def copy(block, slot, r):
        return pltpu.make_async_copy(src_ref.at[pl.ds(tok_ref[block * bm + r], 1)],
                                     buf_ref.at[slot, pl.ds(r, 1)], sems.at[slot])

    def wait_block(block, slot):
        def row(r, _):
            copy(block, slot, r).wait()
            return 0
        lax.fori_loop(0, bm, row, 0)

    @pl.when(jnp.logical_and(i == 0, j == 0))
    def _():
        def row(r, _):
            copy(0, 0, r).start()
            return 0
        lax.fori_loop(0, bm, row, 0)

    @pl.when(jnp.logical_and(j == 0, i <= n_valid))
    def _():
        wait_block(i, slot)
        half = buf_ref.shape[2]
        x_ref[:, :half], x_ref[:, half:] = _unpack_bf16_pairs(buf_ref[slot])

    @pl.when(i < n_valid)
    def _():
        for r in range(share):
            copy(nxt, 1 - slot, j * share + r).start(priority=r % 2)
        x = x_ref[...]
        g = jnp.dot(x, wg_ref[...], preferred_element_type=F32)
        u = jnp.dot(x, wu_ref[...], preferred_element_type=F32)
        o_ref[...] = (g * jax.nn.sigmoid(g) * u).astype(o_ref.dtype)

    @pl.when(i >= n_valid)
    def _():
        o_ref[...] = jnp.zeros_like(o_ref)

    @pl.when(jnp.logical_and(jnp.logical_and(i == n_blocks - 1, j == nf - 1), i < n_valid))
    def _():
        wait_block(nxt, 1 - slot)


def expert_glu(src, tok_of_row, w_gu, block_expert, n_valid, *, bm, bn=512):
    r = tok_of_row.shape[0]
    d = src.shape[1] * 2
    f = w_gu.shape[2] // 2
    bn = _blk(f, bn)
    nf = f // bn
    assert bm % nf == 0
    vmem = 2 * bm * d * 2 + bm * d * 2 + 2 * (2 * d * bn * 2 + bm * bn * 2) + 3 * bm * bn * 4 + bm * d * 4
    return pl.pallas_call(
        functools.partial(_expert_glu_kernel, bm=bm, n_blocks=r // bm),
        grid_spec=pltpu.PrefetchScalarGridSpec(
            num_scalar_prefetch=3,
            grid=(r // bm, nf),
            in_specs=[pl.BlockSpec(memory_space=pl.ANY),
                      pl.BlockSpec((None, d, bn), lambda i, j, be, nv, tok: (be[i], 0, j)),
                      pl.BlockSpec((None, d, bn), lambda i, j, be, nv, tok: (be[i], 0, j + nf))],
            out_specs=pl.BlockSpec((bm, bn), lambda i, j, be, nv, tok: (i, j)),
            scratch_shapes=[pltpu.VMEM((2, bm, d // 2), jnp.uint32), pltpu.VMEM((bm, d), BF16),
                            pltpu.SemaphoreType.DMA((2,))]),
        out_shape=jax.ShapeDtypeStruct((r, f), BF16),
        compiler_params=_params(("arbitrary", "arbitrary"), vmem),
        name="moe_expert_glu",
    )(block_expert, n_valid, tok_of_row, src, w_gu, w_gu)


def _expert_down_kernel(be_ref, nvalid_ref, x_ref, w_ref, o_ref):
    del be_ref

    @pl.when(pl.program_id(0) < nvalid_ref[0])
    def _():
        o_ref[...] = jnp.dot(x_ref[...], w_ref[...], preferred_element_type=F32)

    @pl.when(pl.program_id(0) >= nvalid_ref[0])
    def _():
        o_ref[...] = jnp.zeros_like(o_ref)


def expert_down(a_sorted, w_down, block_expert, n_valid, *, bm, bn=1024):
    r, f = a_sorted.shape
    d = w_down.shape[2]
    bn = _blk(d, bn)
    vmem = 2 * (bm * f * 2 + f * bn * 2 + bm * bn * 4) + bm * bn * 4
    return pl.pallas_call(
        _expert_down_kernel,
        grid_spec=pltpu.PrefetchScalarGridSpec(
            num_scalar_prefetch=2,
            grid=(r // bm, d // bn),
            in_specs=[pl.BlockSpec((bm, f), lambda i, j, be, nv: (i, 0)),
                      pl.BlockSpec((None, f, bn), lambda i, j, be, nv: (be[i], 0, j))],
            out_specs=pl.BlockSpec((bm, bn), lambda i, j, be, nv: (i, j))),
        out_shape=jax.ShapeDtypeStruct((r, d), F32),
        compiler_params=_params(("arbitrary", "arbitrary"), vmem),
        name="moe_expert_down",
    )(block_expert, n_valid, a_sorted, w_down)


def moe(h, gain, w_router, b_router, w_gu, w_down, *, bm=512):
    t, d = h.shape
    n_exp = w_router.shape[1]
    bm = _blk(t, bm)
    hn, info, counts = route(h, gain, w_router, b_router)
    counts = counts.astype(jnp.int32)
    blocks_per_expert = (counts + bm - 1) // bm
    block_end = jnp.cumsum(blocks_per_expert)
    row_offset = (block_end - blocks_per_expert) * bm
    n_blocks = TOP_K * t // bm + n_exp
    n_valid = block_end[-1:]
    block_ids = jnp.minimum(jnp.arange(n_blocks, dtype=jnp.int32), n_valid[0] - 1)
    block_expert = jnp.sum((block_ids[:, None] >= block_end[None, :]).astype(jnp.int32), axis=1)
    idx = info[:, INFO_IDX1:INFO_IDX2 + 1].astype(jnp.int32)
    rank = info[:, INFO_RANK1:INFO_RANK2 + 1].astype(jnp.int32)
    pos = (row_offset[idx] + rank).T
    token_ids = jnp.broadcast_to(jnp.arange(t, dtype=jnp.int32), (TOP_K, t))
    tok_of_row = jnp.zeros((n_blocks * bm,), jnp.int32).at[pos.reshape(-1)].set(
        token_ids.reshape(-1), unique_indices=True)
    a_sorted = expert_glu(hn, tok_of_row, w_gu, block_expert, n_valid, bm=bm)
    y_sorted = expert_down(a_sorted, w_down, block_expert, n_valid, bm=bm)
    return combine(y_sorted, pos, info, h)


def kernel(x, p, a_norm, a_w_in, a_conv_w, a_w_out, kv_norm, w_kv, b_norm, b_w_q, b_w_o, ffn_norm, dense_w_gu, dense_w_down, moe_w_router, moe_b_router, moe_w_gu, moe_w_down, ple_norm, ple_w_up, ple_w_gate, final_norm):
    batch, seq, d = x.shape
    depth = p.shape[0]
    n_a = a_norm.shape[0]
    t = batch * seq
    p = p.reshape(depth, t, p.shape[-1])
    bf = lambda w: w.astype(BF16)

    h = x.reshape(t, d)
    hb, rstd = stream_operands(h)
    kv = None

    to_convert = [(name, m, w[m].reshape(-1, w.shape[-1]))
                  for m in range(moe_w_gu.shape[0]) for name, w in (("down", moe_w_down), ("gu", moe_w_gu))]
    converted = {}

    def hosting(*args, **kwargs):
        if not to_convert:
            return normed_matmul(*args, **kwargs)
        name, m, w2d = to_convert[0]
        out, wb = normed_matmul(*args, cast=w2d, **kwargs)
        if wb is not None:
            converted[name, m] = wb
            to_convert.pop(0)
        return out

    def expert_weights(name, w, m):
        to_convert[:] = [job for job in to_convert if job[:2] != (name, m)]
        wb = converted.get((name, m))
        return bf(w[m]) if wb is None else wb.reshape(w.shape[1:])
    for i in range(depth):
        if i < n_a:
            bcx = hosting(hb, rstd, a_norm[i], a_w_in[i])
            mixed, w_out = short_conv(bcx, a_conv_w[i], seq), a_w_out[i]
        else:
            j = i - n_a
            if j == 0:
                kv = hosting(hb, rstd, kv_norm, w_kv)
            q = hosting(hb, rstd, b_norm[j], b_w_q[j], scale=HEAD_DIM ** -0.5 * LOG2E)
            mixed, w_out = stick_breaking_attention(q, kv, batch, seq), b_w_o[j]
        if i % 2 == 0:
            h, hb, rstd = resid_matmul(mixed, bf(w_out), h, emit_norm=True)
            act = hosting(hb, rstd, ffn_norm[i], dense_w_gu[i // 2], glu=True, bn=256)
            h, hb, rstd = resid_matmul(act, bf(dense_w_down[i // 2]), h, emit_norm=True, bm=512)
        else:
            m = i // 2
            h = resid_matmul(mixed, bf(w_out), h, emit_norm=False)
            h, hb, rstd = moe(h, ffn_norm[i:i + 1], moe_w_router[m], moe_b_router[m],
                              expert_weights("gu", moe_w_gu, m), expert_weights("down", moe_w_down, m))
        w_gate = bf(ple_w_gate[i] * ple_norm[i][:, None])
        if i + 1 < depth:
            h, hb, rstd = ple(hb, rstd, w_gate, p[i], bf(ple_w_up[i]), h, emit_norm=True)
        else:
            h = ple(hb, rstd, w_gate, p[i], bf(ple_w_up[i]), h, emit_norm=False)
    (out,) = rmsnorm(h, final_norm[None, :], F32)
    return out.reshape(batch, seq, d)
```

```python
import functools

import jax
import jax.numpy as jnp
from jax import lax
from jax.experimental import pallas as pl
from jax.experimental.pallas import tpu as pltpu

F32 = jnp.float32
BF16 = jnp.bfloat16

EPS = 1e-6
HEAD_DIM = 128
CONV_WIDTH = 3
TOP_K = 2
LANES = 128
BF16_SUBLANES = 16
MXU_COLUMNS = 256
VMEM_BUDGET_V7X = 60000 * 1024
VMEM_SLACK = 8 * 1024 * 1024
CAST_SLAB_BYTES = 4 * 1024 * 1024
LOG2E = 1.4426950408889634
EXP_ZERO_BELOW = -110.0
MASKED_SCORE = -1e30


def _params(semantics, block_bytes):
    limit = min(VMEM_BUDGET_V7X, int(block_bytes) + VMEM_SLACK)
    return pltpu.CompilerParams(dimension_semantics=semantics, vmem_limit_bytes=limit)


def _blk(dim, pref):
    b = min(dim, pref)
    while dim % b:
        b //= 2
    return b


def _rmsnorm_kernel(x_ref, g_ref, *o_refs):
    x = x_ref[...]
    y = x * lax.rsqrt(jnp.mean(x * x, axis=-1, keepdims=True) + EPS)
    for n, o_ref in enumerate(o_refs):
        o_ref[...] = (y * g_ref[n:n + 1, :]).astype(o_ref.dtype)


def rmsnorm(x, gains, out_dtype):
    t, d = x.shape
    g = gains.shape[0]
    bt = _blk(t, 256)
    out_bytes = jnp.dtype(out_dtype).itemsize
    outs = pl.pallas_call(
        _rmsnorm_kernel,
        grid=(t // bt,),
        in_specs=[pl.BlockSpec((bt, d), lambda i: (i, 0)),
                  pl.BlockSpec((g, d), lambda i: (0, 0))],
        out_specs=[pl.BlockSpec((bt, d), lambda i: (i, 0))] * g,
        out_shape=[jax.ShapeDtypeStruct((t, d), out_dtype)] * g,
        compiler_params=_params(("parallel",), 2 * bt * d * (4 + g * out_bytes)),
        name="rmsnorm",
    )(x, gains)
    return outs


def _stream_kernel(x_ref, xb_ref, rstd_ref):
    x = x_ref[...]
    xb_ref[...] = x.astype(xb_ref.dtype)
    rstd_ref[...] = jnp.broadcast_to(lax.rsqrt(jnp.mean(x * x, axis=-1, keepdims=True) + EPS), rstd_ref.shape)


def stream_operands(x):
    t, d = x.shape
    bt = _blk(t, 256)
    return pl.pallas_call(
        _stream_kernel,
        grid=(t // bt,),
        in_specs=[pl.BlockSpec((bt, d), lambda i: (i, 0))],
        out_specs=[pl.BlockSpec((bt, d), lambda i: (i, 0)), pl.BlockSpec((bt, LANES), lambda i: (i, 0))],
        out_shape=[jax.ShapeDtypeStruct((t, d), BF16), jax.ShapeDtypeStruct((t, LANES), F32)],
        compiler_params=_params(("parallel",), 2 * bt * (d * 6 + LANES * 4) + bt * d * 4),
        name="stream_operands",
    )(x)


def _emit_stream(h, j, last_j, width, hb_ref, rstd_ref, ssq_ref):
    hb_ref[...] = h.astype(hb_ref.dtype)
    _emit_rstd(jnp.sum(h * h, axis=1, keepdims=True), j, last_j, width, rstd_ref, ssq_ref)


def _emit_rstd(part, j, last_j, width, rstd_ref, ssq_ref):
    @pl.when(j == 0)
    def _():
        ssq_ref[...] = part

    @pl.when(j > 0)
    def _():
        ssq_ref[...] += part

    @pl.when(j == last_j)
    def _():
        rstd_ref[...] = jnp.broadcast_to(lax.rsqrt(ssq_ref[...] * (1.0 / width) + EPS), rstd_ref.shape)


def _resid_mm_kernel(x_ref, w_ref, r_ref, o_ref, *rest, nn, width):
    out = r_ref[...] + jnp.dot(x_ref[...], w_ref[...], preferred_element_type=F32)
    o_ref[...] = out
    if rest:
        _emit_stream(out, pl.program_id(1), nn - 1, width, *rest)


def resid_matmul(x, w, resid, *, emit_norm, bm=1024, bn=512):
    m, kdim = x.shape
    n = w.shape[1]
    bm, bn = _blk(m, bm), _blk(n, bn)
    nn = n // bn
    tile = pl.BlockSpec((bm, bn), lambda i, j: (i, j))
    rows = pl.BlockSpec((bm, LANES), lambda i, j: (i, 0))
    vmem = 2 * (bm * kdim + kdim * bn) * 2 + 4 * bm * bn * 4 + bm * bn * 4
    out_specs = [tile]
    out_shape = [jax.ShapeDtypeStruct((m, n), F32)]
    scratch = []
    if emit_norm:
        out_specs += [tile, rows]
        out_shape += [jax.ShapeDtypeStruct((m, n), BF16), jax.ShapeDtypeStruct((m, LANES), F32)]
        scratch.append(pltpu.VMEM((bm, 1), F32))
        vmem += 2 * bm * bn * 2 + 3 * bm * LANES * 4
    outs = pl.pallas_call(
        functools.partial(_resid_mm_kernel, nn=nn, width=n),
        grid=(m // bm, nn),
        in_specs=[pl.BlockSpec((bm, kdim), lambda i, j: (i, 0)),
                  pl.BlockSpec((kdim, bn), lambda i, j: (0, j)),
                  tile],
        out_specs=out_specs,
        out_shape=out_shape,
        scratch_shapes=scratch,
        compiler_params=_params(("parallel", "arbitrary" if emit_norm else "parallel"), vmem),
        name="resid_matmul",
    )(x, w, resid)
    return outs if emit_norm else outs[0]


def _normed_kernel(x_ref, rs_ref, g_ref, *refs, n_w, scale, has_cast):
    w_refs, refs = refs[:n_w], refs[n_w:]
    if has_cast:
        cast_ref, o_ref, cast_o_ref = refs[:3]
        cast_o_ref[...] = cast_ref[...].astype(cast_o_ref.dtype)
        wb_refs = refs[3:]
    else:
        o_ref, wb_refs = refs[0], refs[1:]

    @pl.when(pl.program_id(1) == 0)
    def _():
        gain = g_ref[...]
        for w_ref, wb_ref in zip(w_refs, wb_refs):
            wb_ref[...] = (w_ref[...] * gain).astype(BF16)

    x = x_ref[...]
    rstd = rs_ref[:, 0:1]
    outs = [jnp.dot(x, wb_ref[...], preferred_element_type=F32) * rstd for wb_ref in wb_refs]
    if n_w == 2:
        g, u = outs
        out = g * jax.nn.sigmoid(g) * u
    else:
        out = outs[0] if scale is None else outs[0] * scale
    o_ref[...] = out.astype(o_ref.dtype)


def normed_matmul(x, rstd, gain, w, *, glu=False, scale=None, cast=None, bm=1024, bn=512):
    m, kdim = x.shape
    n_w = 2 if glu else 1
    n = w.shape[1] // n_w
    bm, bn = _blk(m, bm), _blk(n, bn)
    nn, nm = n // bn, m // bm
    w_specs = [pl.BlockSpec((kdim, bn), lambda j, i, off=off: (0, j + off)) for off in range(0, n_w * nn, nn)]
    vmem = (2 * bm * kdim * 2 + n_w * kdim * bn * (2 * 4 + 2) + 2 * bm * bn * 2 + 2 * kdim * LANES * 4
            + 2 * bm * LANES * 4 + (n_w + 1) * bm * bn * 4)
    in_specs = [pl.BlockSpec((bm, kdim), lambda j, i: (i, 0)),
                pl.BlockSpec((bm, LANES), lambda j, i: (i, 0)),
                pl.BlockSpec((kdim, 1), lambda j, i: (0, 0))] + w_specs
    args = [x, rstd, gain.reshape(kdim, 1)] + [w] * n_w
    out_specs = [pl.BlockSpec((bm, bn), lambda j, i: (i, j))]
    out_shape = [jax.ShapeDtypeStruct((m, n), BF16)]
    slab = _cast_slab(cast, nn * nm) if cast is not None else None
    if slab is not None:
        rows, cols = cast.shape
        last = rows // slab - 1
        side = pl.BlockSpec((slab, cols), lambda j, i: (jnp.minimum(j * nm + i, last), 0))
        in_specs.append(side)
        args.append(cast)
        out_specs.append(side)
        out_shape.append(jax.ShapeDtypeStruct((rows, cols), BF16))
        vmem += 2 * slab * cols * (4 + 2)
    outs = pl.pallas_call(
        functools.partial(_normed_kernel, n_w=n_w, scale=scale, has_cast=slab is not None),
        grid=(nn, nm),
        in_specs=in_specs,
        out_specs=out_specs,
        out_shape=out_shape,
        scratch_shapes=[pltpu.VMEM((kdim, bn), BF16)] * n_w,
        compiler_params=_params(("arbitrary", "arbitrary"), vmem),
        name="normed_glu" if glu else "normed_matmul",
    )(*args)
    return (outs[0], outs[1] if slab is not None else None) if cast is not None else outs[0]


def _cast_slab(a, steps):
    rows = a.shape[0]
    slab = BF16_SUBLANES
    while slab <= rows and (rows % slab or rows // slab > steps):
        slab *= 2
    return slab if slab <= rows and slab * a.shape[1] * 4 <= CAST_SLAB_BYTES else None


def _conv_kernel(b_ref, c_ref, u_ref, ch_ref, uh_ref, w_ref, o_ref, *, blocks_per_seq):
    ts = c_ref.shape[0]
    cu = c_ref[...].astype(F32) * u_ref[...].astype(F32)
    seq_start = (pl.program_id(0) % blocks_per_seq) == 0
    halo = ch_ref[...].astype(F32) * uh_ref[...].astype(F32)
    halo = jnp.where(seq_start, 0.0, halo)
    row = lax.broadcasted_iota(jnp.int32, cu.shape, 0)
    h1 = halo[BF16_SUBLANES - 1:BF16_SUBLANES, :]
    h2 = halo[BF16_SUBLANES - 2:BF16_SUBLANES - 1, :]
    prev1 = jnp.where(row == 0, h1, pltpu.roll(cu, 1, 0))
    prev2 = jnp.where(row == 0, h2, jnp.where(row == 1, h1, pltpu.roll(cu, 2, 0)))
    y = w_ref[0:1, :] * prev2 + w_ref[1:2, :] * prev1 + w_ref[2:3, :] * cu
    o_ref[...] = (b_ref[...].astype(F32) * y).astype(o_ref.dtype)


def short_conv(bcx, conv_w, seq):
    t, d3 = bcx.shape
    d = d3 // 3
    ts = _blk(seq, 1024)
    dblk = _blk(d, 1024)
    nd = d // dblk
    hb = ts // BF16_SUBLANES

    def halo_map(off):
        return lambda i, j: (jnp.maximum(i * hb - 1, 0), j + off)

    vmem = 2 * (4 * ts * dblk * 2 + 2 * BF16_SUBLANES * dblk * 2) + 6 * ts * dblk * 4
    return pl.pallas_call(
        functools.partial(_conv_kernel, blocks_per_seq=seq // ts),
        grid=(t // ts, nd),
        in_specs=[pl.BlockSpec((ts, dblk), lambda i, j: (i, j)),
                  pl.BlockSpec((ts, dblk), lambda i, j: (i, j + nd)),
                  pl.BlockSpec((ts, dblk), lambda i, j: (i, j + 2 * nd)),
                  pl.BlockSpec((BF16_SUBLANES, dblk), halo_map(nd)),
                  pl.BlockSpec((BF16_SUBLANES, dblk), halo_map(2 * nd)),
                  pl.BlockSpec((CONV_WIDTH, dblk), lambda i, j: (0, j))],
        out_specs=pl.BlockSpec((ts, dblk), lambda i, j: (i, j)),
        out_shape=jax.ShapeDtypeStruct((t, d), BF16),
        compiler_params=_params(("parallel", "parallel"), vmem),
        name="short_conv",
    )(bcx, bcx, bcx, bcx, bcx, conv_w)


def _ple_kernel(hb_ref, wg_ref, p_ref, wu_ref, h_ref, rs_ref, o_ref, *rest, nn, width):
    hb, pb, rstd = hb_ref[...], p_ref[...].astype(BF16), rs_ref[:, 0:1]
    bn = o_ref.shape[1]
    halves = [slice(0, bn // 2), slice(bn // 2, bn)] if bn % (2 * MXU_COLUMNS) == 0 else [slice(0, bn)]
    dots = [(jnp.dot(hb, wg_ref[:, c], preferred_element_type=F32),
             jnp.dot(pb, wu_ref[:, c], preferred_element_type=F32)) for c in halves]
    ssq = None
    for c, (pre, up) in zip(halves, dots):
        out = h_ref[:, c] + up * jax.nn.sigmoid(pre * rstd)
        o_ref[:, c] = out
        if rest:
            rest[0][:, c] = out.astype(rest[0].dtype)
            part = jnp.sum(out * out, axis=1, keepdims=True)
            ssq = part if ssq is None else ssq + part
    if rest:
        _emit_rstd(ssq, pl.program_id(1), nn - 1, width, *rest[1:])


def ple(hb, rstd, w_gate, p, w_up, h, *, emit_norm, bm=1024, bn=512):
    m, d = hb.shape
    pd = p.shape[1]
    n = w_gate.shape[1]
    bm, bn = _blk(m, bm), _blk(n, bn)
    nn = n // bn
    tile = pl.BlockSpec((bm, bn), lambda i, j: (i, j))
    rows = pl.BlockSpec((bm, LANES), lambda i, j: (i, 0))
    vmem = (2 * (bm * d * 2 + d * bn * 2 + bm * pd * 4 + pd * bn * 2 + 2 * bm * bn * 4 + bm * LANES * 4)
            + 3 * bm * bn * 4)
    out_specs, out_shape, scratch = [tile], [jax.ShapeDtypeStruct((m, n), F32)], []
    if emit_norm:
        out_specs += [tile, rows]
        out_shape += [jax.ShapeDtypeStruct((m, n), BF16), jax.ShapeDtypeStruct((m, LANES), F32)]
        scratch.append(pltpu.VMEM((bm, 1), F32))
        vmem += 2 * bm * bn * 2 + 3 * bm * LANES * 4
    outs = pl.pallas_call(
        functools.partial(_ple_kernel, nn=nn, width=n),
        grid=(m // bm, nn),
        in_specs=[pl.BlockSpec((bm, d), lambda i, j: (i, 0)),
                  pl.BlockSpec((d, bn), lambda i, j: (0, j)),
                  pl.BlockSpec((bm, pd), lambda i, j: (i, 0)),
                  pl.BlockSpec((pd, bn), lambda i, j: (0, j)),
                  tile, rows],
        out_specs=out_specs,
        out_shape=out_shape,
        scratch_shapes=scratch,
        compiler_params=_params(("parallel", "arbitrary" if emit_norm else "parallel"), vmem),
        name="ple",
    )(hb, w_gate, p, w_up, h, rstd)
    return outs if emit_norm else outs[0]


def _attn_kernel(q_ref, k_ref, v_ref, tri_ref, bias_ref, o_ref, acc_ref, gone_ref, *, tb, rs, group, seq):
    gone_limit = -EXP_ZERO_BELOW * LOG2E

    def scores(q, k_start, nk, bias):
        z = lax.dot_general(q, k_ref[pl.ds(k_start, nk), :], (((1,), (1,)), ((), ())),
                            preferred_element_type=F32)
        return z if bias is None else z + bias

    def split(z):
        sp = jnp.maximum(z, 0.0) + jnp.log(1.0 + jnp.exp2(-jnp.abs(z))) * LOG2E
        hi = sp.astype(BF16)
        return sp, hi, (sp - hi.astype(F32)).astype(BF16)

    def suffix_sum(hi, lo, nk):
        tri = tri_ref[0:nk, 0:nk]
        return jnp.dot(hi, tri, preferred_element_type=F32) + jnp.dot(lo, tri, preferred_element_type=F32)

    def weights(z, sp, suffix, gone):
        log2_w = z - sp - suffix
        walked = suffix[:, 0:1] + sp[:, 0:1]
        if gone is not None:
            log2_w, walked = log2_w - gone, walked + gone
        return jnp.exp2(log2_w).astype(BF16), walked

    def values(w, k_start, nk):
        return jnp.dot(w, v_ref[pl.ds(k_start, nk), :], preferred_element_type=F32)

    def tile(q, k_start, nk, gone, bias):
        z = scores(q, k_start, nk, bias)
        sp, hi, lo = split(z)
        w, gone = weights(z, sp, suffix_sum(hi, lo, nk), gone)
        return values(w, k_start, nk), gone

    def query_blocks(qis, first):
        q_starts = [pl.multiple_of(qi * tb, tb) for qi in qis]
        ns = tb // rs
        units = [(b, s, True) for b in range(len(qis)) for s in range(ns)]
        if not first:
            units += [(b, s, False) for b in range(len(qis)) for s in range(ns)]
        live = [None] * len(units)
        gone_of, pv_of = {}, {}

        def run(stage, u):
            b, s, diagonal = units[u]
            rows = slice(s * rs, (s + 1) * rs)
            nk = (s + 1) * rs if diagonal else tb
            k_start = q_starts[b] if diagonal else pl.multiple_of(q_starts[b] - tb, tb)
            if stage == 0:
                q = q_ref[pl.ds(q_starts[b] + s * rs, rs), :]
                live[u] = (scores(q, k_start, nk, bias_ref[rows, 0:nk] if diagonal else None),)
            elif stage == 1:
                (z,) = live[u]
                live[u] = (z,) + split(z)
            elif stage == 2:
                z, sp, hi, lo = live[u]
                live[u] = (z, sp, suffix_sum(hi, lo, nk))
            elif stage == 3:
                z, sp, suffix = live[u]
                w, gone_of[b, s] = weights(z, sp, suffix, gone_of.get((b, s)))
                live[u] = (w,)
            else:
                (w,) = live[u]
                live[u] = None
                pv = values(w, k_start, nk)
                if first:
                    o_ref[pl.ds(q_starts[b] + s * rs, rs), :] = pv.astype(o_ref.dtype)
                elif diagonal:
                    pv_of[b, s] = pv
                else:
                    acc_ref[b, rows, :] = pv_of.pop((b, s)) + pv
                    gone_ref[b, rows, :] = gone_of[b, s]

        n_stages = 5
        for step in range(len(units) + n_stages - 1):
            for stage in range(n_stages):
                if 0 <= step - stage < len(units):
                    run(stage, step - stage)
        if first:
            return

        for b, qi in enumerate(qis):
            def cond(carry):
                kb, alive = carry
                return jnp.logical_and(kb >= 0, alive > 0)

            def body(carry, b=b):
                kb, _ = carry
                pv_far, gone_far = tile(q_ref[pl.ds(q_starts[b], tb), :], pl.multiple_of(kb * tb, tb), tb,
                                        gone_ref[b], None)
                acc_ref[b] += pv_far
                gone_ref[b] = gone_far
                return kb - 1, (jnp.min(gone_far) < gone_limit).astype(jnp.int32)

            lax.while_loop(cond, body, (qi - 2, (jnp.min(gone_ref[b]) < gone_limit).astype(jnp.int32)))
            o_ref[pl.ds(q_starts[b], tb), :] = acc_ref[b].astype(o_ref.dtype)

    nq = seq // tb
    query_blocks([0], True)
    n_iter = (nq - 1) // group
    lead = nq - 1 - n_iter * group
    if lead:
        query_blocks(list(range(1, 1 + lead)), False)

    def step(it, carry):
        query_blocks([1 + lead + it * group + g for g in range(group)], False)
        return carry

    lax.fori_loop(0, n_iter, step, 0)


def stick_breaking_attention(q, kv, batch, seq, *, tb=256, rs=128, group=4):
    t, hd = q.shape
    n_heads = hd // HEAD_DIM
    tb = _blk(seq, tb)
    rows = lax.broadcasted_iota(jnp.int32, (tb, tb), 0)
    cols = lax.broadcasted_iota(jnp.int32, (tb, tb), 1)
    tri = (rows > cols).astype(BF16)
    bias = jnp.where(cols < rows, 0.0, MASKED_SCORE).astype(F32)
    vmem = 2 * (4 * seq * HEAD_DIM * 2 + tb * tb * 6) + 16 * tb * tb * 4
    head = lambda b, h: (b, h)
    return pl.pallas_call(
        functools.partial(_attn_kernel, tb=tb, rs=min(rs, tb), group=group, seq=seq),
        grid=(batch, n_heads),
        in_specs=[pl.BlockSpec((seq, HEAD_DIM), head),
                  pl.BlockSpec((seq, HEAD_DIM), head),
                  pl.BlockSpec((seq, HEAD_DIM), lambda b, h: (b, n_heads + h)),
                  pl.BlockSpec((tb, tb), lambda b, h: (0, 0)),
                  pl.BlockSpec((tb, tb), lambda b, h: (0, 0))],
        out_specs=pl.BlockSpec((seq, HEAD_DIM), head),
        out_shape=jax.ShapeDtypeStruct((t, hd), BF16),
        scratch_shapes=[pltpu.VMEM((group, tb, HEAD_DIM), F32), pltpu.VMEM((group, tb, 1), F32)],
        compiler_params=_params(("parallel", "parallel"), vmem),
        name="stick_breaking_attention",
    )(q, kv, kv, tri, bias)


INFO_IDX1, INFO_IDX2, INFO_RANK1, INFO_RANK2, INFO_W1, INFO_W2 = range(6)


def _pack_bf16_pairs(x):
    half = x.shape[1] // 2
    bits = lambda v: lax.bitcast_convert_type(v.astype(BF16).astype(F32), jnp.uint32)
    return (bits(x[:, :half]) >> 16) | (bits(x[:, half:]) & jnp.uint32(0xFFFF0000))


def _unpack_bf16_pairs(words):
    low = lax.bitcast_convert_type(words << 16, F32).astype(BF16)
    high = lax.bitcast_convert_type(words & jnp.uint32(0xFFFF0000), F32).astype(BF16)
    return low, high


def _router_kernel(h_ref, g_ref, wr_ref, br_ref, tri_ref, hn_ref, info_ref, cnt_ref, count_ref):
    @pl.when(pl.program_id(0) == 0)
    def _():
        count_ref[...] = jnp.zeros_like(count_ref)

    x = h_ref[...]
    hn = x * lax.rsqrt(jnp.mean(x * x, axis=-1, keepdims=True) + EPS) * g_ref[...]
    hn_ref[...] = _pack_bf16_pairs(hn)
    logits = jnp.dot(hn, wr_ref[...], precision=lax.Precision.HIGHEST,
                     preferred_element_type=F32) + br_ref[...]
    lane = lax.broadcasted_iota(jnp.int32, logits.shape, 1)
    m1 = jnp.max(logits, axis=1, keepdims=True)
    i1 = jnp.min(jnp.where(logits == m1, lane, LANES), axis=1, keepdims=True)
    rest = jnp.where(lane == i1, -jnp.inf, logits)
    m2 = jnp.max(rest, axis=1, keepdims=True)
    i2 = jnp.min(jnp.where(rest == m2, lane, LANES), axis=1, keepdims=True)
    e2 = jnp.exp(m2 - m1)
    w1 = 1.0 / (1.0 + e2)
    w2 = e2 / (1.0 + e2)
    sel1 = lane == i1
    sel2 = lane == i2
    sel = jnp.logical_or(sel1, sel2).astype(F32)
    before = jnp.dot(tri_ref[...], sel.astype(BF16), preferred_element_type=F32) + count_ref[0:1, :]
    rank1 = jnp.sum(jnp.where(sel1, before, 0.0), axis=1, keepdims=True)
    rank2 = jnp.sum(jnp.where(sel2, before, 0.0), axis=1, keepdims=True)
    count_ref[...] = count_ref[...] + jnp.sum(sel, axis=0, keepdims=True)
    cnt_ref[...] = count_ref[...]
    info = jnp.zeros(logits.shape, F32)
    for slot, val in ((INFO_IDX1, i1.astype(F32)), (INFO_IDX2, i2.astype(F32)), (INFO_RANK1, rank1),
                      (INFO_RANK2, rank2), (INFO_W1, w1), (INFO_W2, w2)):
        info = jnp.where(lane == slot, val, info)
    info_ref[...] = info


def route(h, gain, w_router, b_router, *, bt=512):
    t, d = h.shape
    n_exp = w_router.shape[1]
    bt = _blk(t, bt)
    wr = jnp.zeros((d, LANES), F32).at[:, :n_exp].set(w_router)
    br = jnp.full((1, LANES), -1e30, F32).at[0, :n_exp].set(b_router)
    tri = (lax.broadcasted_iota(jnp.int32, (bt, bt), 1)
           < lax.broadcasted_iota(jnp.int32, (bt, bt), 0)).astype(BF16)
    vmem = 2 * (2 * bt * d * 4 + d * LANES * 4 + bt * bt * 2 + bt * LANES * 4) + 4 * bt * d * 4
    hn, info, counts = pl.pallas_call(
        _router_kernel,
        grid=(t // bt,),
        in_specs=[pl.BlockSpec((bt, d), lambda i: (i, 0)),
                  pl.BlockSpec((1, d), lambda i: (0, 0)),
                  pl.BlockSpec((d, LANES), lambda i: (0, 0)),
                  pl.BlockSpec((1, LANES), lambda i: (0, 0)),
                  pl.BlockSpec((bt, bt), lambda i: (0, 0))],
        out_specs=[pl.BlockSpec((bt, d // 2), lambda i: (i, 0)),
                   pl.BlockSpec((bt, LANES), lambda i: (i, 0)),
                   pl.BlockSpec((8, LANES), lambda i: (0, 0))],
        out_shape=[jax.ShapeDtypeStruct((t, d // 2), jnp.uint32),
                   jax.ShapeDtypeStruct((t, LANES), F32),
                   jax.ShapeDtypeStruct((8, LANES), F32)],
        scratch_shapes=[pltpu.VMEM((8, LANES), F32)],
        compiler_params=_params(("arbitrary",), vmem),
        name="moe_router",
    )(h, gain, wr, br, tri)
    return hn, info, counts[0, :n_exp]


def _combine_kernel(pos_ref, pos_next_ref, y_ref, w_ref, h_ref, o_ref, hb_ref, rstd_ref, buf_ref, sems, *, tc):
    i = pl.program_id(0)
    slot = i % 2

    def copy(pos, slot, j, k):
        return pltpu.make_async_copy(y_ref.at[pl.ds(pos[k, j], 1)], buf_ref.at[slot, k, pl.ds(j, 1)],
                                     sems.at[slot])

    def issue(pos, slot):
        def token(j, _):
            copy(pos, slot, j, 0).start(priority=0)
            copy(pos, slot, j, 1).start(priority=1)
            return 0
        lax.fori_loop(0, tc, token, 0)

    @pl.when(i == 0)
    def _():
        issue(pos_ref, 0)

    @pl.when(i + 1 < pl.num_programs(0))
    def _():
        issue(pos_next_ref, 1 - slot)

    def wait(j, _):
        copy(pos_ref, slot, j, 0).wait()
        copy(pos_ref, slot, j, 1).wait()
        return 0

    lax.fori_loop(0, tc, wait, 0, unroll=4)
    w = w_ref[...]
    out = (h_ref[...] + w[:, INFO_W1:INFO_W1 + 1] * buf_ref[slot, 0]
           + w[:, INFO_W2:INFO_W2 + 1] * buf_ref[slot, 1])
    o_ref[...] = out
    hb_ref[...] = out.astype(hb_ref.dtype)
    rstd_ref[...] = jnp.broadcast_to(lax.rsqrt(jnp.mean(out * out, axis=1, keepdims=True) + EPS),
                                     rstd_ref.shape)


def combine(y_sorted, pos, info, h, *, tc=256):
    t, d = h.shape
    tc = _blk(t, tc)
    last = t // tc - 1
    vmem = 4 * tc * d * 4 + 2 * (2 * tc * d * 4 + 2 * tc * LANES * 4 + tc * d * 2) + 2 * tc * d * 4
    return pl.pallas_call(
        functools.partial(_combine_kernel, tc=tc),
        grid=(t // tc,),
        in_specs=[pl.BlockSpec((TOP_K, tc), lambda i: (0, i), memory_space=pltpu.SMEM),
                  pl.BlockSpec((TOP_K, tc), lambda i: (0, jnp.minimum(i + 1, last)), memory_space=pltpu.SMEM),
                  pl.BlockSpec(memory_space=pl.ANY),
                  pl.BlockSpec((tc, LANES), lambda i: (i, 0)),
                  pl.BlockSpec((tc, d), lambda i: (i, 0))],
        out_specs=[pl.BlockSpec((tc, d), lambda i: (i, 0)),
                   pl.BlockSpec((tc, d), lambda i: (i, 0)),
                   pl.BlockSpec((tc, LANES), lambda i: (i, 0))],
        out_shape=[jax.ShapeDtypeStruct((t, d), F32),
                   jax.ShapeDtypeStruct((t, d), BF16),
                   jax.ShapeDtypeStruct((t, LANES), F32)],
        scratch_shapes=[pltpu.VMEM((2, TOP_K, tc, d), F32), pltpu.SemaphoreType.DMA((2,))],
        compiler_params=_params(("arbitrary",), vmem),
        name="moe_combine",
    )(pos, pos, y_sorted, info, h)


GATHER_SLOTS = 3


def _expert_glu_kernel(be_ref, nvalid_ref, tok_ref, src_ref, wg_ref, wu_ref, o_ref, buf_ref, x_ref, sems, *,
                       bm):
    del be_ref
    i, j, nf = pl.program_id(0), pl.program_id(1), pl.num_programs(1)
    n_valid = nvalid_ref[0]
    share = bm // nf

    def copy(block, r):
        slot = block % GATHER_SLOTS
        return pltpu.make_async_copy(src_ref.at[pl.ds(tok_ref[block * bm + r], 1)],
                                     buf_ref.at[slot, pl.ds(r, 1)], sems.at[slot])

    @pl.when(jnp.logical_and(i == 0, j == 0))
    def _():
        def row(r, _):
            copy(0, r).start(priority=0)
            copy(1, r).start(priority=1)
            return 0
        lax.fori_loop(0, bm, row, 0)

    @pl.when(jnp.logical_and(j == 0, i < n_valid + GATHER_SLOTS - 1))
    def _():
        def row(r, _):
            copy(i, r).wait()
            return 0
        lax.fori_loop(0, bm, row, 0, unroll=8)
        half = buf_ref.shape[2]
        x_ref[:, :half], x_ref[:, half:] = _unpack_bf16_pairs(buf_ref[i % GATHER_SLOTS])

    @pl.when(i < n_valid)
    def _():
        for r in range(share):
            copy(i + GATHER_SLOTS - 1, j * share + r).start(priority=r % 2)
        x = x_ref[...]
        g = jnp.dot(x, wg_ref[...], preferred_element_type=F32)
        u = jnp.dot(x, wu_ref[...], preferred_element_type=F32)
        o_ref[...] = (g * jax.nn.sigmoid(g) * u).astype(o_ref.dtype)

    @pl.when(i >= n_valid)
    def _():
        o_ref[...] = jnp.zeros_like(o_ref)


def expert_glu(src, tok_of_row, w_gu, block_expert, n_valid, *, bm, bn=512):
    r = tok_of_row.shape[0]
    d = src.shape[1] * 2
    f = w_gu.shape[2] // 2
    bn = _blk(f, bn)
    nf = f // bn
    assert bm % nf == 0
    vmem = (GATHER_SLOTS * bm * d * 2 + bm * d * 2 + 2 * (2 * d * bn * 2 + bm * bn * 2) + 3 * bm * bn * 4
            + bm * d * 4)
    col = lambda i, j, nv: jnp.where(i < nv[0], j, 0)
    return pl.pallas_call(
        functools.partial(_expert_glu_kernel, bm=bm),
        grid_spec=pltpu.PrefetchScalarGridSpec(
            num_scalar_prefetch=3,
            grid=(r // bm, nf),
            in_specs=[pl.BlockSpec(memory_space=pl.ANY),
                      pl.BlockSpec((None, d, bn), lambda i, j, be, nv, tok: (be[i], 0, col(i, j, nv))),
                      pl.BlockSpec((None, d, bn), lambda i, j, be, nv, tok: (be[i], 0, col(i, j, nv) + nf))],
            out_specs=pl.BlockSpec((bm, bn), lambda i, j, be, nv, tok: (i, j)),
            scratch_shapes=[pltpu.VMEM((GATHER_SLOTS, bm, d // 2), jnp.uint32), pltpu.VMEM((bm, d), BF16),
                            pltpu.SemaphoreType.DMA((GATHER_SLOTS,))]),
        out_shape=jax.ShapeDtypeStruct((r, f), BF16),
        compiler_params=_params(("arbitrary", "arbitrary"), vmem),
        name="moe_expert_glu",
    )(block_expert, n_valid, tok_of_row, src, w_gu, w_gu)


def _expert_down_kernel(be_ref, nvalid_ref, x_ref, w_ref, o_ref):
    del be_ref

    @pl.when(pl.program_id(0) < nvalid_ref[0])
    def _():
        o_ref[...] = jnp.dot(x_ref[...], w_ref[...], preferred_element_type=F32)

    @pl.when(pl.program_id(0) >= nvalid_ref[0])
    def _():
        o_ref[...] = jnp.zeros_like(o_ref)


def expert_down(a_sorted, w_down, block_expert, n_valid, *, bm, bn=1024):
    r, f = a_sorted.shape
    d = w_down.shape[2]
    bn = _blk(d, bn)
    vmem = 2 * (bm * f * 2 + f * bn * 2 + bm * bn * 4) + bm * bn * 4
    return pl.pallas_call(
        _expert_down_kernel,
        grid_spec=pltpu.PrefetchScalarGridSpec(
            num_scalar_prefetch=2,
            grid=(r // bm, d // bn),
            in_specs=[pl.BlockSpec((bm, f), lambda i, j, be, nv: (i, 0)),
                      pl.BlockSpec((None, f, bn), lambda i, j, be, nv: (be[i], 0, jnp.where(i < nv[0], j, 0)))],
            out_specs=pl.BlockSpec((bm, bn), lambda i, j, be, nv: (i, j))),
        out_shape=jax.ShapeDtypeStruct((r, d), F32),
        compiler_params=_params(("arbitrary", "arbitrary"), vmem),
        name="moe_expert_down",
    )(block_expert, n_valid, a_sorted, w_down)


def moe(h, gain, w_router, b_router, w_gu, w_down, *, bm=512):
    t, d = h.shape
    n_exp = w_router.shape[1]
    bm = _blk(t, bm)
    hn, info, counts = route(h, gain, w_router, b_router)
    counts = counts.astype(jnp.int32)
    blocks_per_expert = (counts + bm - 1) // bm
    block_end = jnp.cumsum(blocks_per_expert)
    row_offset = (block_end - blocks_per_expert) * bm
    n_blocks = TOP_K * t // bm + n_exp + GATHER_SLOTS - 1
    n_valid = block_end[-1:]
    block_ids = jnp.minimum(jnp.arange(n_blocks, dtype=jnp.int32), n_valid[0] - 1)
    block_expert = jnp.sum((block_ids[:, None] >= block_end[None, :]).astype(jnp.int32), axis=1)
    idx = info[:, INFO_IDX1:INFO_IDX2 + 1].astype(jnp.int32)
    rank = info[:, INFO_RANK1:INFO_RANK2 + 1].astype(jnp.int32)
    pos = (row_offset[idx] + rank).T
    token_ids = jnp.broadcast_to(jnp.arange(t, dtype=jnp.int32), (TOP_K, t))
    tok_of_row = jnp.zeros((n_blocks * bm,), jnp.int32).at[pos.reshape(-1)].set(
        token_ids.reshape(-1), unique_indices=True)
    a_sorted = expert_glu(hn, tok_of_row, w_gu, block_expert, n_valid, bm=bm)
    y_sorted = expert_down(a_sorted, w_down, block_expert, n_valid, bm=bm)
    return combine(y_sorted, pos, info, h)


def kernel(x, p, a_norm, a_w_in, a_conv_w, a_w_out, kv_norm, w_kv, b_norm, b_w_q, b_w_o, ffn_norm, dense_w_gu, dense_w_down, moe_w_router, moe_b_router, moe_w_gu, moe_w_down, ple_norm, ple_w_up, ple_w_gate, final_norm):
    batch, seq, d = x.shape
    depth = p.shape[0]
    n_a = a_norm.shape[0]
    t = batch * seq
    p = p.reshape(depth, t, p.shape[-1])
    bf = lambda w: w.astype(BF16)

    h = x.reshape(t, d)
    hb, rstd = stream_operands(h)
    kv = None

    to_convert = [(name, m, w[m].reshape(-1, w.shape[-1]))
                  for m in range(moe_w_gu.shape[0]) for name, w in (("down", moe_w_down), ("gu", moe_w_gu))]
    converted = {}

    def hosting(*args, **kwargs):
        if not to_convert:
            return normed_matmul(*args, **kwargs)
        name, m, w2d = to_convert[0]
        out, wb = normed_matmul(*args, cast=w2d, **kwargs)
        if wb is not None:
            converted[name, m] = wb
            to_convert.pop(0)
        return out

    def expert_weights(name, w, m):
        to_convert[:] = [job for job in to_convert if job[:2] != (name, m)]
        wb = converted.get((name, m))
        return bf(w[m]) if wb is None else wb.reshape(w.shape[1:])
    for i in range(depth):
        if i < n_a:
            bcx = hosting(hb, rstd, a_norm[i], a_w_in[i])
            mixed, w_out = short_conv(bcx, a_conv_w[i], seq), a_w_out[i]
        else:
            j = i - n_a
            if j == 0:
                kv = hosting(hb, rstd, kv_norm, w_kv)
            q = hosting(hb, rstd, b_norm[j], b_w_q[j], scale=HEAD_DIM ** -0.5 * LOG2E)
            mixed, w_out = stick_breaking_attention(q, kv, batch, seq), b_w_o[j]
        if i % 2 == 0:
            h, hb, rstd = resid_matmul(mixed, bf(w_out), h, emit_norm=True)
            act = hosting(hb, rstd, ffn_norm[i], dense_w_gu[i // 2], glu=True, bn=256)
            h, hb, rstd = resid_matmul(act, bf(dense_w_down[i // 2]), h, emit_norm=True, bm=512)
        else:
            m = i // 2
            h = resid_matmul(mixed, bf(w_out), h, emit_norm=False)
            h, hb, rstd = moe(h, ffn_norm[i:i + 1], moe_w_router[m], moe_b_router[m],
                              expert_weights("gu", moe_w_gu, m), expert_weights("down", moe_w_down, m))
        w_gate = bf(ple_w_gate[i] * ple_norm[i][:, None])
        if i + 1 < depth:
            h, hb, rstd = ple(hb, rstd, w_gate, p[i], bf(ple_w_up[i]), h, emit_norm=True)
        else:
            h = ple(hb, rstd, w_gate, p[i], bf(ple_w_up[i]), h, emit_norm=False)
    (out,) = rmsnorm(h, final_norm[None, :], F32)
    return out.reshape(batch, seq, d)
```

```python
import functools

import jax
import jax.numpy as jnp
from jax import lax
from jax.experimental import pallas as pl
from jax.experimental.pallas import tpu as pltpu

F32 = jnp.float32
BF16 = jnp.bfloat16

EPS = 1e-6
HEAD_DIM = 128
CONV_WIDTH = 3
TOP_K = 2
LANES = 128
BF16_SUBLANES = 16
MXU_COLUMNS = 256
VMEM_BUDGET_V7X = 60000 * 1024
VMEM_SLACK = 8 * 1024 * 1024
CAST_SLAB_BYTES = 4 * 1024 * 1024
LOG2E = 1.4426950408889634
EXP_ZERO_BELOW = -110.0
MASKED_SCORE = -1e30


def _params(semantics, block_bytes):
    limit = min(VMEM_BUDGET_V7X, int(block_bytes) + VMEM_SLACK)
    return pltpu.CompilerParams(dimension_semantics=semantics, vmem_limit_bytes=limit)


def _blk(dim, pref):
    b = min(dim, pref)
    while dim % b:
        b //= 2
    return b


def _rmsnorm_kernel(x_ref, g_ref, *o_refs):
    x = x_ref[...]
    y = x * lax.rsqrt(jnp.mean(x * x, axis=-1, keepdims=True) + EPS)
    for n, o_ref in enumerate(o_refs):
        o_ref[...] = (y * g_ref[n:n + 1, :]).astype(o_ref.dtype)


def rmsnorm(x, gains, out_dtype):
    t, d = x.shape
    g = gains.shape[0]
    bt = _blk(t, 256)
    out_bytes = jnp.dtype(out_dtype).itemsize
    outs = pl.pallas_call(
        _rmsnorm_kernel,
        grid=(t // bt,),
        in_specs=[pl.BlockSpec((bt, d), lambda i: (i, 0)),
                  pl.BlockSpec((g, d), lambda i: (0, 0))],
        out_specs=[pl.BlockSpec((bt, d), lambda i: (i, 0))] * g,
        out_shape=[jax.ShapeDtypeStruct((t, d), out_dtype)] * g,
        compiler_params=_params(("parallel",), 2 * bt * d * (4 + g * out_bytes)),
        name="rmsnorm",
    )(x, gains)
    return outs


def _stream_kernel(x_ref, xb_ref, rstd_ref):
    x = x_ref[...]
    xb_ref[...] = x.astype(xb_ref.dtype)
    rstd_ref[...] = jnp.broadcast_to(lax.rsqrt(jnp.mean(x * x, axis=-1, keepdims=True) + EPS), rstd_ref.shape)


def stream_operands(x):
    t, d = x.shape
    bt = _blk(t, 256)
    return pl.pallas_call(
        _stream_kernel,
        grid=(t // bt,),
        in_specs=[pl.BlockSpec((bt, d), lambda i: (i, 0))],
        out_specs=[pl.BlockSpec((bt, d), lambda i: (i, 0)), pl.BlockSpec((bt, LANES), lambda i: (i, 0))],
        out_shape=[jax.ShapeDtypeStruct((t, d), BF16), jax.ShapeDtypeStruct((t, LANES), F32)],
        compiler_params=_params(("parallel",), 2 * bt * (d * 6 + LANES * 4) + bt * d * 4),
        name="stream_operands",
    )(x)


def _emit_stream(h, j, last_j, width, hb_ref, rstd_ref, ssq_ref):
    hb_ref[...] = h.astype(hb_ref.dtype)
    _emit_rstd(jnp.sum(h * h, axis=1, keepdims=True), j, last_j, width, rstd_ref, ssq_ref)


def _emit_rstd(part, j, last_j, width, rstd_ref, ssq_ref):
    @pl.when(j == 0)
    def _():
        ssq_ref[...] = part

    @pl.when(j > 0)
    def _():
        ssq_ref[...] += part

    @pl.when(j == last_j)
    def _():
        rstd_ref[...] = jnp.broadcast_to(lax.rsqrt(ssq_ref[...] * (1.0 / width) + EPS), rstd_ref.shape)


def _resid_mm_kernel(x_ref, w_ref, r_ref, o_ref, *rest, nn, width):
    out = r_ref[...] + jnp.dot(x_ref[...], w_ref[...], preferred_element_type=F32)
    o_ref[...] = out
    if rest:
        _emit_stream(out, pl.program_id(1), nn - 1, width, *rest)


def resid_matmul(x, w, resid, *, emit_norm, bm=1024, bn=512):
    m, kdim = x.shape
    n = w.shape[1]
    bm, bn = _blk(m, bm), _blk(n, bn)
    nn = n // bn
    tile = pl.BlockSpec((bm, bn), lambda i, j: (i, j))
    rows = pl.BlockSpec((bm, LANES), lambda i, j: (i, 0))
    vmem = 2 * (bm * kdim + kdim * bn) * 2 + 4 * bm * bn * 4 + bm * bn * 4
    out_specs = [tile]
    out_shape = [jax.ShapeDtypeStruct((m, n), F32)]
    scratch = []
    if emit_norm:
        out_specs += [tile, rows]
        out_shape += [jax.ShapeDtypeStruct((m, n), BF16), jax.ShapeDtypeStruct((m, LANES), F32)]
        scratch.append(pltpu.VMEM((bm, 1), F32))
        vmem += 2 * bm * bn * 2 + 3 * bm * LANES * 4
    outs = pl.pallas_call(
        functools.partial(_resid_mm_kernel, nn=nn, width=n),
        grid=(m // bm, nn),
        in_specs=[pl.BlockSpec((bm, kdim), lambda i, j: (i, 0)),
                  pl.BlockSpec((kdim, bn), lambda i, j: (0, j)),
                  tile],
        out_specs=out_specs,
        out_shape=out_shape,
        scratch_shapes=scratch,
        compiler_params=_params(("parallel", "arbitrary" if emit_norm else "parallel"), vmem),
        name="resid_matmul",
    )(x, w, resid)
    return outs if emit_norm else outs[0]


def _normed_kernel(x_ref, rs_ref, g_ref, *refs, n_w, scale, has_cast):
    w_refs, refs = refs[:n_w], refs[n_w:]
    if has_cast:
        cast_ref, o_ref, cast_o_ref = refs[:3]
        cast_o_ref[...] = cast_ref[...].astype(cast_o_ref.dtype)
        wb_refs = refs[3:]
    else:
        o_ref, wb_refs = refs[0], refs[1:]

    @pl.when(pl.program_id(1) == 0)
    def _():
        gain = g_ref[...]
        for w_ref, wb_ref in zip(w_refs, wb_refs):
            wb_ref[...] = (w_ref[...] * gain).astype(BF16)

    x = x_ref[...]
    rstd = rs_ref[:, 0:1]
    outs = [jnp.dot(x, wb_ref[...], preferred_element_type=F32) * rstd for wb_ref in wb_refs]
    if n_w == 2:
        g, u = outs
        out = g * jax.nn.sigmoid(g) * u
    else:
        out = outs[0] if scale is None else outs[0] * scale
    o_ref[...] = out.astype(o_ref.dtype)


def normed_matmul(x, rstd, gain, w, *, glu=False, scale=None, cast=None, bm=1024, bn=512):
    m, kdim = x.shape
    n_w = 2 if glu else 1
    n = w.shape[1] // n_w
    bm, bn = _blk(m, bm), _blk(n, bn)
    nn, nm = n // bn, m // bm
    w_specs = [pl.BlockSpec((kdim, bn), lambda j, i, off=off: (0, j + off)) for off in range(0, n_w * nn, nn)]
    vmem = (2 * bm * kdim * 2 + n_w * kdim * bn * (2 * 4 + 2) + 2 * bm * bn * 2 + 2 * kdim * LANES * 4
            + 2 * bm * LANES * 4 + (n_w + 1) * bm * bn * 4)
    in_specs = [pl.BlockSpec((bm, kdim), lambda j, i: (i, 0)),
                pl.BlockSpec((bm, LANES), lambda j, i: (i, 0)),
                pl.BlockSpec((kdim, 1), lambda j, i: (0, 0))] + w_specs
    args = [x, rstd, gain.reshape(kdim, 1)] + [w] * n_w
    out_specs = [pl.BlockSpec((bm, bn), lambda j, i: (i, j))]
    out_shape = [jax.ShapeDtypeStruct((m, n), BF16)]
    slab = _cast_slab(cast, nn * nm) if cast is not None else None
    if slab is not None:
        rows, cols = cast.shape
        last = rows // slab - 1
        side = pl.BlockSpec((slab, cols), lambda j, i: (jnp.minimum(j * nm + i, last), 0))
        in_specs.append(side)
        args.append(cast)
        out_specs.append(side)
        out_shape.append(jax.ShapeDtypeStruct((rows, cols), BF16))
        vmem += 2 * slab * cols * (4 + 2)
    outs = pl.pallas_call(
        functools.partial(_normed_kernel, n_w=n_w, scale=scale, has_cast=slab is not None),
        grid=(nn, nm),
        in_specs=in_specs,
        out_specs=out_specs,
        out_shape=out_shape,
        scratch_shapes=[pltpu.VMEM((kdim, bn), BF16)] * n_w,
        compiler_params=_params(("arbitrary", "arbitrary"), vmem),
        name="normed_glu" if glu else "normed_matmul",
    )(*args)
    return (outs[0], outs[1] if slab is not None else None) if cast is not None else outs[0]


def _cast_slab(a, steps):
    rows = a.shape[0]
    slab = BF16_SUBLANES
    while slab <= rows and (rows % slab or rows // slab > steps):
        slab *= 2
    return slab if slab <= rows and slab * a.shape[1] * 4 <= CAST_SLAB_BYTES else None


def _conv_kernel(b_ref, c_ref, u_ref, ch_ref, uh_ref, w_ref, o_ref, *, blocks_per_seq):
    ts = c_ref.shape[0]
    cu = c_ref[...].astype(F32) * u_ref[...].astype(F32)
    seq_start = (pl.program_id(0) % blocks_per_seq) == 0
    halo = ch_ref[...].astype(F32) * uh_ref[...].astype(F32)
    halo = jnp.where(seq_start, 0.0, halo)
    row = lax.broadcasted_iota(jnp.int32, cu.shape, 0)
    h1 = halo[BF16_SUBLANES - 1:BF16_SUBLANES, :]
    h2 = halo[BF16_SUBLANES - 2:BF16_SUBLANES - 1, :]
    prev1 = jnp.where(row == 0, h1, pltpu.roll(cu, 1, 0))
    prev2 = jnp.where(row == 0, h2, jnp.where(row == 1, h1, pltpu.roll(cu, 2, 0)))
    y = w_ref[0:1, :] * prev2 + w_ref[1:2, :] * prev1 + w_ref[2:3, :] * cu
    o_ref[...] = (b_ref[...].astype(F32) * y).astype(o_ref.dtype)


def short_conv(bcx, conv_w, seq):
    t, d3 = bcx.shape
    d = d3 // 3
    ts = _blk(seq, 1024)
    dblk = _blk(d, 1024)
    nd = d // dblk
    hb = ts // BF16_SUBLANES

    def halo_map(off):
        return lambda i, j: (jnp.maximum(i * hb - 1, 0), j + off)

    vmem = 2 * (4 * ts * dblk * 2 + 2 * BF16_SUBLANES * dblk * 2) + 6 * ts * dblk * 4
    return pl.pallas_call(
        functools.partial(_conv_kernel, blocks_per_seq=seq // ts),
        grid=(t // ts, nd),
        in_specs=[pl.BlockSpec((ts, dblk), lambda i, j: (i, j)),
                  pl.BlockSpec((ts, dblk), lambda i, j: (i, j + nd)),
                  pl.BlockSpec((ts, dblk), lambda i, j: (i, j + 2 * nd)),
                  pl.BlockSpec((BF16_SUBLANES, dblk), halo_map(nd)),
                  pl.BlockSpec((BF16_SUBLANES, dblk), halo_map(2 * nd)),
                  pl.BlockSpec((CONV_WIDTH, dblk), lambda i, j: (0, j))],
        out_specs=pl.BlockSpec((ts, dblk), lambda i, j: (i, j)),
        out_shape=jax.ShapeDtypeStruct((t, d), BF16),
        compiler_params=_params(("parallel", "parallel"), vmem),
        name="short_conv",
    )(bcx, bcx, bcx, bcx, bcx, conv_w)


def _ple_kernel(hb_ref, wg_ref, p_ref, wu_ref, h_ref, rs_ref, o_ref, *rest, nn, width):
    hb, pb, rstd = hb_ref[...], p_ref[...].astype(BF16), rs_ref[:, 0:1]
    bn = o_ref.shape[1]
    halves = [slice(0, bn // 2), slice(bn // 2, bn)] if bn % (2 * MXU_COLUMNS) == 0 else [slice(0, bn)]
    dots = [(jnp.dot(hb, wg_ref[:, c], preferred_element_type=F32),
             jnp.dot(pb, wu_ref[:, c], preferred_element_type=F32)) for c in halves]
    ssq = None
    for c, (pre, up) in zip(halves, dots):
        out = h_ref[:, c] + up * jax.nn.sigmoid(pre * rstd)
        o_ref[:, c] = out
        if rest:
            rest[0][:, c] = out.astype(rest[0].dtype)
            part = jnp.sum(out * out, axis=1, keepdims=True)
            ssq = part if ssq is None else ssq + part
    if rest:
        _emit_rstd(ssq, pl.program_id(1), nn - 1, width, *rest[1:])


def ple(hb, rstd, w_gate, p, w_up, h, *, emit_norm, bm=1024, bn=512):
    m, d = hb.shape
    pd = p.shape[1]
    n = w_gate.shape[1]
    bm, bn = _blk(m, bm), _blk(n, bn)
    nn = n // bn
    tile = pl.BlockSpec((bm, bn), lambda i, j: (i, j))
    rows = pl.BlockSpec((bm, LANES), lambda i, j: (i, 0))
    vmem = (2 * (bm * d * 2 + d * bn * 2 + bm * pd * 4 + pd * bn * 2 + 2 * bm * bn * 4 + bm * LANES * 4)
            + 3 * bm * bn * 4)
    out_specs, out_shape, scratch = [tile], [jax.ShapeDtypeStruct((m, n), F32)], []
    if emit_norm:
        out_specs += [tile, rows]
        out_shape += [jax.ShapeDtypeStruct((m, n), BF16), jax.ShapeDtypeStruct((m, LANES), F32)]
        scratch.append(pltpu.VMEM((bm, 1), F32))
        vmem += 2 * bm * bn * 2 + 3 * bm * LANES * 4
    outs = pl.pallas_call(
        functools.partial(_ple_kernel, nn=nn, width=n),
        grid=(m // bm, nn),
        in_specs=[pl.BlockSpec((bm, d), lambda i, j: (i, 0)),
                  pl.BlockSpec((d, bn), lambda i, j: (0, j)),
                  pl.BlockSpec((bm, pd), lambda i, j: (i, 0)),
                  pl.BlockSpec((pd, bn), lambda i, j: (0, j)),
                  tile, rows],
        out_specs=out_specs,
        out_shape=out_shape,
        scratch_shapes=scratch,
        compiler_params=_params(("parallel", "arbitrary" if emit_norm else "parallel"), vmem),
        name="ple",
    )(hb, w_gate, p, w_up, h, rstd)
    return outs if emit_norm else outs[0]


def _attn_kernel(q_ref, k_ref, v_ref, tri_ref, bias_ref, o_ref, acc_ref, gone_ref, *, tb, rs, group, seq):
    gone_limit = -EXP_ZERO_BELOW * LOG2E

    def scores(q, k_start, nk, bias):
        z = lax.dot_general(q, k_ref[pl.ds(k_start, nk), :], (((1,), (1,)), ((), ())),
                            preferred_element_type=F32)
        return z if bias is None else z + bias

    def split(z):
        sp = jnp.maximum(z, 0.0) + jnp.log(1.0 + jnp.exp2(-jnp.abs(z))) * LOG2E
        hi = sp.astype(BF16)
        return sp, hi, (sp - hi.astype(F32)).astype(BF16)

    def suffix_sum(hi, lo, nk):
        tri = tri_ref[0:nk, 0:nk]
        return jnp.dot(hi, tri, preferred_element_type=F32) + jnp.dot(lo, tri, preferred_element_type=F32)

    def weights(z, sp, suffix, gone):
        log2_w = z - sp - suffix
        walked = suffix[:, 0:1] + sp[:, 0:1]
        if gone is not None:
            log2_w, walked = log2_w - gone, walked + gone
        return jnp.exp2(log2_w).astype(BF16), walked

    def values(w, k_start, nk):
        return jnp.dot(w, v_ref[pl.ds(k_start, nk), :], preferred_element_type=F32)

    def tile(q, k_start, nk, gone, bias):
        z = scores(q, k_start, nk, bias)
        sp, hi, lo = split(z)
        w, gone = weights(z, sp, suffix_sum(hi, lo, nk), gone)
        return values(w, k_start, nk), gone

    def query_blocks(qis, first):
        q_starts = [pl.multiple_of(qi * tb, tb) for qi in qis]
        ns = tb // rs
        units = [(b, s, True) for b in range(len(qis)) for s in range(ns)]
        if not first:
            units += [(b, s, False) for b in range(len(qis)) for s in range(ns)]
        live = [None] * len(units)
        gone_of, pv_of = {}, {}

        def run(stage, u):
            b, s, diagonal = units[u]
            rows = slice(s * rs, (s + 1) * rs)
            nk = (s + 1) * rs if diagonal else tb
            k_start = q_starts[b] if diagonal else pl.multiple_of(q_starts[b] - tb, tb)
            if stage == 0:
                q = q_ref[pl.ds(q_starts[b] + s * rs, rs), :]
                live[u] = (scores(q, k_start, nk, bias_ref[rows, 0:nk] if diagonal else None),)
            elif stage == 1:
                (z,) = live[u]
                live[u] = (z,) + split(z)
            elif stage == 2:
                z, sp, hi, lo = live[u]
                live[u] = (z, sp, suffix_sum(hi, lo, nk))
            elif stage == 3:
                z, sp, suffix = live[u]
                w, gone_of[b, s] = weights(z, sp, suffix, gone_of.get((b, s)))
                live[u] = (w,)
            else:
                (w,) = live[u]
                live[u] = None
                pv = values(w, k_start, nk)
                if first:
                    o_ref[pl.ds(q_starts[b] + s * rs, rs), :] = pv.astype(o_ref.dtype)
                elif diagonal:
                    pv_of[b, s] = pv
                else:
                    acc_ref[b, rows, :] = pv_of.pop((b, s)) + pv
                    gone_ref[b, rows, :] = gone_of[b, s]

        n_stages = 5
        for step in range(len(units) + n_stages - 1):
            for stage in range(n_stages):
                if 0 <= step - stage < len(units):
                    run(stage, step - stage)
        if first:
            return

        for b, qi in enumerate(qis):
            def cond(carry):
                kb, alive = carry
                return jnp.logical_and(kb >= 0, alive > 0)

            def body(carry, b=b):
                kb, _ = carry
                pv_far, gone_far = tile(q_ref[pl.ds(q_starts[b], tb), :], pl.multiple_of(kb * tb, tb), tb,
                                        gone_ref[b], None)
                acc_ref[b] += pv_far
                gone_ref[b] = gone_far
                return kb - 1, (jnp.min(gone_far) < gone_limit).astype(jnp.int32)

            lax.while_loop(cond, body, (qi - 2, (jnp.min(gone_ref[b]) < gone_limit).astype(jnp.int32)))
            o_ref[pl.ds(q_starts[b], tb), :] = acc_ref[b].astype(o_ref.dtype)

    nq = seq // tb
    query_blocks([0], True)
    n_iter = (nq - 1) // group
    lead = nq - 1 - n_iter * group
    if lead:
        query_blocks(list(range(1, 1 + lead)), False)

    def step(it, carry):
        query_blocks([1 + lead + it * group + g for g in range(group)], False)
        return carry

    lax.fori_loop(0, n_iter, step, 0)


def stick_breaking_attention(q, kv, batch, seq, *, tb=256, rs=128, group=4):
    t, hd = q.shape
    n_heads = hd // HEAD_DIM
    tb = _blk(seq, tb)
    rows = lax.broadcasted_iota(jnp.int32, (tb, tb), 0)
    cols = lax.broadcasted_iota(jnp.int32, (tb, tb), 1)
    tri = (rows > cols).astype(BF16)
    bias = jnp.where(cols < rows, 0.0, MASKED_SCORE).astype(F32)
    vmem = 2 * (4 * seq * HEAD_DIM * 2 + tb * tb * 6) + 16 * tb * tb * 4
    head = lambda b, h: (b, h)
    return pl.pallas_call(
        functools.partial(_attn_kernel, tb=tb, rs=min(rs, tb), group=group, seq=seq),
        grid=(batch, n_heads),
        in_specs=[pl.BlockSpec((seq, HEAD_DIM), head),
                  pl.BlockSpec((seq, HEAD_DIM), head),
                  pl.BlockSpec((seq, HEAD_DIM), lambda b, h: (b, n_heads + h)),
                  pl.BlockSpec((tb, tb), lambda b, h: (0, 0)),
                  pl.BlockSpec((tb, tb), lambda b, h: (0, 0))],
        out_specs=pl.BlockSpec((seq, HEAD_DIM), head),
        out_shape=jax.ShapeDtypeStruct((t, hd), BF16),
        scratch_shapes=[pltpu.VMEM((group, tb, HEAD_DIM), F32), pltpu.VMEM((group, tb, 1), F32)],
        compiler_params=_params(("parallel", "parallel"), vmem),
        name="stick_breaking_attention",
    )(q, kv, kv, tri, bias)


INFO_IDX1, INFO_IDX2, INFO_RANK1, INFO_RANK2, INFO_W1, INFO_W2 = range(6)


def _pack_bf16_pairs(x):
    half = x.shape[1] // 2
    bits = lambda v: lax.bitcast_convert_type(v.astype(BF16).astype(F32), jnp.uint32)
    return (bits(x[:, :half]) >> 16) | (bits(x[:, half:]) & jnp.uint32(0xFFFF0000))


def _unpack_bf16_pairs(words):
    low = lax.bitcast_convert_type(words << 16, F32).astype(BF16)
    high = lax.bitcast_convert_type(words & jnp.uint32(0xFFFF0000), F32).astype(BF16)
    return low, high


def _router_kernel(h_ref, g_ref, wr_ref, br_ref, tri_ref, hn_ref, info_ref, cnt_ref, count_ref):
    @pl.when(pl.program_id(0) == 0)
    def _():
        count_ref[...] = jnp.zeros_like(count_ref)

    x = h_ref[...]
    hn = x * lax.rsqrt(jnp.mean(x * x, axis=-1, keepdims=True) + EPS) * g_ref[...]
    hn_ref[...] = _pack_bf16_pairs(hn)
    logits = jnp.dot(hn, wr_ref[...], precision=lax.Precision.HIGHEST,
                     preferred_element_type=F32) + br_ref[...]
    lane = lax.broadcasted_iota(jnp.int32, logits.shape, 1)
    m1 = jnp.max(logits, axis=1, keepdims=True)
    i1 = jnp.min(jnp.where(logits == m1, lane, LANES), axis=1, keepdims=True)
    rest = jnp.where(lane == i1, -jnp.inf, logits)
    m2 = jnp.max(rest, axis=1, keepdims=True)
    i2 = jnp.min(jnp.where(rest == m2, lane, LANES), axis=1, keepdims=True)
    e2 = jnp.exp(m2 - m1)
    w1 = 1.0 / (1.0 + e2)
    w2 = e2 / (1.0 + e2)
    sel1 = lane == i1
    sel2 = lane == i2
    sel = jnp.logical_or(sel1, sel2).astype(F32)
    before = jnp.dot(tri_ref[...], sel.astype(BF16), preferred_element_type=F32) + count_ref[0:1, :]
    rank1 = jnp.sum(jnp.where(sel1, before, 0.0), axis=1, keepdims=True)
    rank2 = jnp.sum(jnp.where(sel2, before, 0.0), axis=1, keepdims=True)
    count_ref[...] = count_ref[...] + jnp.sum(sel, axis=0, keepdims=True)
    cnt_ref[...] = count_ref[...]
    info = jnp.zeros(logits.shape, F32)
    for slot, val in ((INFO_IDX1, i1.astype(F32)), (INFO_IDX2, i2.astype(F32)), (INFO_RANK1, rank1),
                      (INFO_RANK2, rank2), (INFO_W1, w1), (INFO_W2, w2)):
        info = jnp.where(lane == slot, val, info)
    info_ref[...] = info


def route(h, gain, w_router, b_router, *, bt=512):
    t, d = h.shape
    n_exp = w_router.shape[1]
    bt = _blk(t, bt)
    wr = jnp.zeros((d, LANES), F32).at[:, :n_exp].set(w_router)
    br = jnp.full((1, LANES), -1e30, F32).at[0, :n_exp].set(b_router)
    tri = (lax.broadcasted_iota(jnp.int32, (bt, bt), 1)
           < lax.broadcasted_iota(jnp.int32, (bt, bt), 0)).astype(BF16)
    vmem = 2 * (2 * bt * d * 4 + d * LANES * 4 + bt * bt * 2 + bt * LANES * 4) + 4 * bt * d * 4
    hn, info, counts = pl.pallas_call(
        _router_kernel,
        grid=(t // bt,),
        in_specs=[pl.BlockSpec((bt, d), lambda i: (i, 0)),
                  pl.BlockSpec((1, d), lambda i: (0, 0)),
                  pl.BlockSpec((d, LANES), lambda i: (0, 0)),
                  pl.BlockSpec((1, LANES), lambda i: (0, 0)),
                  pl.BlockSpec((bt, bt), lambda i: (0, 0))],
        out_specs=[pl.BlockSpec((bt, d // 2), lambda i: (i, 0)),
                   pl.BlockSpec((bt, LANES), lambda i: (i, 0)),
                   pl.BlockSpec((8, LANES), lambda i: (0, 0))],
        out_shape=[jax.ShapeDtypeStruct((t, d // 2), jnp.uint32),
                   jax.ShapeDtypeStruct((t, LANES), F32),
                   jax.ShapeDtypeStruct((8, LANES), F32)],
        scratch_shapes=[pltpu.VMEM((8, LANES), F32)],
        compiler_params=_params(("arbitrary",), vmem),
        name="moe_router",
    )(h, gain, wr, br, tri)
    return hn, info, counts[0, :n_exp]


def _combine_kernel(pos_ref, pos_next_ref, y_ref, w_ref, h_ref, o_ref, hb_ref, rstd_ref, buf_ref, sems, *, tc):
    i = pl.program_id(0)
    slot = i % 2

    def copy(pos, slot, j, k):
        return pltpu.make_async_copy(y_ref.at[pl.ds(pos[k, j], 1)], buf_ref.at[slot, k, pl.ds(j, 1)],
                                     sems.at[slot])

    def issue(pos, slot):
        def token(j, _):
            copy(pos, slot, j, 0).start(priority=0)
            copy(pos, slot, j, 1).start(priority=1)
            return 0
        lax.fori_loop(0, tc, token, 0)

    @pl.when(i == 0)
    def _():
        issue(pos_ref, 0)

    @pl.when(i + 1 < pl.num_programs(0))
    def _():
        issue(pos_next_ref, 1 - slot)

    def wait(j, _):
        copy(pos_ref, slot, j, 0).wait()
        copy(pos_ref, slot, j, 1).wait()
        return 0

    lax.fori_loop(0, tc, wait, 0, unroll=4)
    w = w_ref[...]
    out = (h_ref[...] + w[:, INFO_W1:INFO_W1 + 1] * buf_ref[slot, 0]
           + w[:, INFO_W2:INFO_W2 + 1] * buf_ref[slot, 1])
    o_ref[...] = out
    hb_ref[...] = out.astype(hb_ref.dtype)
    rstd_ref[...] = jnp.broadcast_to(lax.rsqrt(jnp.mean(out * out, axis=1, keepdims=True) + EPS),
                                     rstd_ref.shape)


def combine(y_sorted, pos, info, h, *, tc=256):
    t, d = h.shape
    tc = _blk(t, tc)
    last = t // tc - 1
    vmem = 4 * tc * d * 4 + 2 * (2 * tc * d * 4 + 2 * tc * LANES * 4 + tc * d * 2) + 2 * tc * d * 4
    return pl.pallas_call(
        functools.partial(_combine_kernel, tc=tc),
        grid=(t // tc,),
        in_specs=[pl.BlockSpec((TOP_K, tc), lambda i: (0, i), memory_space=pltpu.SMEM),
                  pl.BlockSpec((TOP_K, tc), lambda i: (0, jnp.minimum(i + 1, last)), memory_space=pltpu.SMEM),
                  pl.BlockSpec(memory_space=pl.ANY),
                  pl.BlockSpec((tc, LANES), lambda i: (i, 0)),
                  pl.BlockSpec((tc, d), lambda i: (i, 0))],
        out_specs=[pl.BlockSpec((tc, d), lambda i: (i, 0)),
                   pl.BlockSpec((tc, d), lambda i: (i, 0)),
                   pl.BlockSpec((tc, LANES), lambda i: (i, 0))],
        out_shape=[jax.ShapeDtypeStruct((t, d), F32),
                   jax.ShapeDtypeStruct((t, d), BF16),
                   jax.ShapeDtypeStruct((t, LANES), F32)],
        scratch_shapes=[pltpu.VMEM((2, TOP_K, tc, d), F32), pltpu.SemaphoreType.DMA((2,))],
        compiler_params=_params(("arbitrary",), vmem),
        name="moe_combine",
    )(pos, pos, y_sorted, info, h)


GATHER_SLOTS = 3
ROW_PARTS = 4


def _expert_glu_kernel(be_ref, nvalid_ref, rows_ref, tok_ref, src_ref, wg_ref, wu_ref, o_ref, buf_ref, x_ref, sems, *,
                       bm):
    del be_ref
    i, j, nf = pl.program_id(0), pl.program_id(1), pl.num_programs(1)
    n_valid = nvalid_ref[0]
    share = bm // nf

    def copy(block, r):
        slot = block % GATHER_SLOTS
        return pltpu.make_async_copy(src_ref.at[pl.ds(tok_ref[block * bm + r], 1)],
                                     buf_ref.at[slot, pl.ds(r, 1)], sems.at[slot])

    @pl.when(jnp.logical_and(i == 0, j == 0))
    def _():
        def row(r, _):
            copy(0, r).start(priority=0)
            copy(1, r).start(priority=1)
            return 0
        lax.fori_loop(0, bm, row, 0)

    @pl.when(jnp.logical_and(j == 0, i < n_valid + GATHER_SLOTS - 1))
    def _():
        def row(r, _):
            copy(i, r).wait()
            return 0
        lax.fori_loop(0, bm, row, 0, unroll=8)
        half = buf_ref.shape[2]
        x_ref[:, :half], x_ref[:, half:] = _unpack_bf16_pairs(buf_ref[i % GATHER_SLOTS])

    part = bm // ROW_PARTS
    parts = (rows_ref[i] + part - 1) // part
    for q in range(1, ROW_PARTS + 1):
        @pl.when(parts == q)
        def _(n=q * part):
            for r in range(share):
                copy(i + GATHER_SLOTS - 1, j * share + r).start(priority=r % 2)
            x = x_ref[0:n, :]
            g = jnp.dot(x, wg_ref[...], preferred_element_type=F32)
            u = jnp.dot(x, wu_ref[...], preferred_element_type=F32)
            o_ref[0:n, :] = (g * jax.nn.sigmoid(g) * u).astype(o_ref.dtype)
            if n < bm:
                o_ref[n:, :] = jnp.zeros((bm - n, o_ref.shape[1]), o_ref.dtype)

    @pl.when(parts == 0)
    def _():
        o_ref[...] = jnp.zeros_like(o_ref)


def expert_glu(src, tok_of_row, w_gu, block_expert, n_valid, block_rows, *, bm, bn=512):
    r = tok_of_row.shape[0]
    d = src.shape[1] * 2
    f = w_gu.shape[2] // 2
    bn = _blk(f, bn)
    nf = f // bn
    assert bm % nf == 0
    vmem = (GATHER_SLOTS * bm * d * 2 + bm * d * 2 + 2 * (2 * d * bn * 2 + bm * bn * 2) + 3 * bm * bn * 4
            + bm * d * 4)
    col = lambda i, j, nv: jnp.where(i < nv[0], j, 0)
    return pl.pallas_call(
        functools.partial(_expert_glu_kernel, bm=bm),
        grid_spec=pltpu.PrefetchScalarGridSpec(
            num_scalar_prefetch=4,
            grid=(r // bm, nf),
            in_specs=[pl.BlockSpec(memory_space=pl.ANY),
                      pl.BlockSpec((None, d, bn), lambda i, j, be, nv, rows, tok: (be[i], 0, col(i, j, nv))),
                      pl.BlockSpec((None, d, bn), lambda i, j, be, nv, rows, tok: (be[i], 0, col(i, j, nv) + nf))],
            out_specs=pl.BlockSpec((bm, bn), lambda i, j, be, nv, rows, tok: (i, j)),
            scratch_shapes=[pltpu.VMEM((GATHER_SLOTS, bm, d // 2), jnp.uint32), pltpu.VMEM((bm, d), BF16),
                            pltpu.SemaphoreType.DMA((GATHER_SLOTS,))]),
        out_shape=jax.ShapeDtypeStruct((r, f), BF16),
        compiler_params=_params(("arbitrary", "arbitrary"), vmem),
        name="moe_expert_glu",
    )(block_expert, n_valid, block_rows, tok_of_row, src, w_gu, w_gu)


def _expert_down_kernel(be_ref, nvalid_ref, rows_ref, x_ref, w_ref, o_ref):
    del be_ref, nvalid_ref
    bm = o_ref.shape[0]
    part = bm // ROW_PARTS
    parts = (rows_ref[pl.program_id(0)] + part - 1) // part
    for q in range(1, ROW_PARTS + 1):
        @pl.when(parts == q)
        def _(n=q * part):
            o_ref[0:n, :] = jnp.dot(x_ref[0:n, :], w_ref[...], preferred_element_type=F32)
            if n < bm:
                o_ref[n:, :] = jnp.zeros((bm - n, o_ref.shape[1]), o_ref.dtype)

    @pl.when(parts == 0)
    def _():
        o_ref[...] = jnp.zeros_like(o_ref)


def expert_down(a_sorted, w_down, block_expert, n_valid, block_rows, *, bm, bn=1024):
    r, f = a_sorted.shape
    d = w_down.shape[2]
    bn = _blk(d, bn)
    vmem = 2 * (bm * f * 2 + f * bn * 2 + bm * bn * 4) + bm * bn * 4
    return pl.pallas_call(
        _expert_down_kernel,
        grid_spec=pltpu.PrefetchScalarGridSpec(
            num_scalar_prefetch=3,
            grid=(r // bm, d // bn),
            in_specs=[pl.BlockSpec((bm, f), lambda i, j, be, nv, rows: (i, 0)),
                      pl.BlockSpec((None, f, bn),
                                   lambda i, j, be, nv, rows: (be[i], 0, jnp.where(i < nv[0], j, 0)))],
            out_specs=pl.BlockSpec((bm, bn), lambda i, j, be, nv, rows: (i, j))),
        out_shape=jax.ShapeDtypeStruct((r, d), F32),
        compiler_params=_params(("arbitrary", "arbitrary"), vmem),
        name="moe_expert_down",
    )(block_expert, n_valid, block_rows, a_sorted, w_down)


def moe(h, gain, w_router, b_router, w_gu, w_down, *, bm=512):
    t, d = h.shape
    n_exp = w_router.shape[1]
    bm = _blk(t, bm)
    hn, info, counts = route(h, gain, w_router, b_router)
    counts = counts.astype(jnp.int32)
    blocks_per_expert = (counts + bm - 1) // bm
    block_end = jnp.cumsum(blocks_per_expert)
    row_offset = (block_end - blocks_per_expert) * bm
    n_blocks = TOP_K * t // bm + n_exp + GATHER_SLOTS - 1
    n_valid = block_end[-1:]
    all_blocks = jnp.arange(n_blocks, dtype=jnp.int32)
    block_ids = jnp.minimum(all_blocks, n_valid[0] - 1)
    block_expert = jnp.sum((block_ids[:, None] >= block_end[None, :]).astype(jnp.int32), axis=1)
    rows_before = (block_ids - (block_end - blocks_per_expert)[block_expert]) * bm
    block_rows = jnp.where(all_blocks < n_valid[0], jnp.clip(counts[block_expert] - rows_before, 0, bm), 0)
    idx = info[:, INFO_IDX1:INFO_IDX2 + 1].astype(jnp.int32)
    rank = info[:, INFO_RANK1:INFO_RANK2 + 1].astype(jnp.int32)
    pos = (row_offset[idx] + rank).T
    token_ids = jnp.broadcast_to(jnp.arange(t, dtype=jnp.int32), (TOP_K, t))
    tok_of_row = jnp.zeros((n_blocks * bm,), jnp.int32).at[pos.reshape(-1)].set(
        token_ids.reshape(-1), unique_indices=True)
    a_sorted = expert_glu(hn, tok_of_row, w_gu, block_expert, n_valid, block_rows, bm=bm)
    y_sorted = expert_down(a_sorted, w_down, block_expert, n_valid, block_rows, bm=bm)
    return combine(y_sorted, pos, info, h)


def kernel(x, p, a_norm, a_w_in, a_conv_w, a_w_out, kv_norm, w_kv, b_norm, b_w_q, b_w_o, ffn_norm, dense_w_gu, dense_w_down, moe_w_router, moe_b_router, moe_w_gu, moe_w_down, ple_norm, ple_w_up, ple_w_gate, final_norm):
    batch, seq, d = x.shape
    depth = p.shape[0]
    n_a = a_norm.shape[0]
    t = batch * seq
    p = p.reshape(depth, t, p.shape[-1])
    bf = lambda w: w.astype(BF16)

    h = x.reshape(t, d)
    hb, rstd = stream_operands(h)
    kv = None

    to_convert = [(name, m, w[m].reshape(-1, w.shape[-1]))
                  for m in range(moe_w_gu.shape[0]) for name, w in (("down", moe_w_down), ("gu", moe_w_gu))]
    converted = {}

    def hosting(*args, **kwargs):
        if not to_convert:
            return normed_matmul(*args, **kwargs)
        name, m, w2d = to_convert[0]
        out, wb = normed_matmul(*args, cast=w2d, **kwargs)
        if wb is not None:
            converted[name, m] = wb
            to_convert.pop(0)
        return out

    def expert_weights(name, w, m):
        to_convert[:] = [job for job in to_convert if job[:2] != (name, m)]
        wb = converted.get((name, m))
        return bf(w[m]) if wb is None else wb.reshape(w.shape[1:])
    for i in range(depth):
        if i < n_a:
            bcx = hosting(hb, rstd, a_norm[i], a_w_in[i])
            mixed, w_out = short_conv(bcx, a_conv_w[i], seq), a_w_out[i]
        else:
            j = i - n_a
            if j == 0:
                kv = hosting(hb, rstd, kv_norm, w_kv)
            q = hosting(hb, rstd, b_norm[j], b_w_q[j], scale=HEAD_DIM ** -0.5 * LOG2E)
            mixed, w_out = stick_breaking_attention(q, kv, batch, seq), b_w_o[j]
        if i % 2 == 0:
            h, hb, rstd = resid_matmul(mixed, bf(w_out), h, emit_norm=True)
            act = hosting(hb, rstd, ffn_norm[i], dense_w_gu[i // 2], glu=True, bn=256)
            h, hb, rstd = resid_matmul(act, bf(dense_w_down[i // 2]), h, emit_norm=True, bm=512)
        else:
            m = i // 2
            h = resid_matmul(mixed, bf(w_out), h, emit_norm=False)
            h, hb, rstd = moe(h, ffn_norm[i:i + 1], moe_w_router[m], moe_b_router[m],
                              expert_weights("gu", moe_w_gu, m), expert_weights("down", moe_w_down, m))
        w_gate = bf(ple_w_gate[i] * ple_norm[i][:, None])
        if i + 1 < depth:
            h, hb, rstd = ple(hb, rstd, w_gate, p[i], bf(ple_w_up[i]), h, emit_norm=True)
        else:
            h = ple(hb, rstd, w_gate, p[i], bf(ple_w_up[i]), h, emit_norm=False)
    (out,) = rmsnorm(h, final_norm[None, :], F32)
    return out.reshape(batch, seq, d)
```

```python
import functools

import jax
import jax.numpy as jnp
from jax import lax
from jax.experimental import pallas as pl
from jax.experimental.pallas import tpu as pltpu

F32 = jnp.float32
BF16 = jnp.bfloat16

EPS = 1e-6
HEAD_DIM = 128
CONV_WIDTH = 3
TOP_K = 2
LANES = 128
BF16_SUBLANES = 16
MXU_COLUMNS = 256
VMEM_BUDGET_V7X = 60000 * 1024
VMEM_SLACK = 8 * 1024 * 1024
CAST_SLAB_BYTES = 4 * 1024 * 1024
LOG2E = 1.4426950408889634
EXP_ZERO_BELOW = -110.0
MASKED_SCORE = -1e30


def _params(semantics, block_bytes):
    limit = min(VMEM_BUDGET_V7X, int(block_bytes) + VMEM_SLACK)
    return pltpu.CompilerParams(dimension_semantics=semantics, vmem_limit_bytes=limit)


def _blk(dim, pref):
    b = min(dim, pref)
    while dim % b:
        b //= 2
    return b


def _stream_kernel(x_ref, xb_ref, rstd_ref):
    x = x_ref[...]
    xb_ref[...] = x.astype(xb_ref.dtype)
    rstd_ref[...] = jnp.broadcast_to(lax.rsqrt(jnp.mean(x * x, axis=-1, keepdims=True) + EPS), rstd_ref.shape)


def stream_operands(x):
    t, d = x.shape
    bt = _blk(t, 256)
    return pl.pallas_call(
        _stream_kernel,
        grid=(t // bt,),
        in_specs=[pl.BlockSpec((bt, d), lambda i: (i, 0))],
        out_specs=[pl.BlockSpec((bt, d), lambda i: (i, 0)), pl.BlockSpec((bt, LANES), lambda i: (i, 0))],
        out_shape=[jax.ShapeDtypeStruct((t, d), BF16), jax.ShapeDtypeStruct((t, LANES), F32)],
        compiler_params=_params(("parallel",), 2 * bt * (d * 6 + LANES * 4) + bt * d * 4),
        name="stream_operands",
    )(x)


def _emit_stream(h, j, last_j, width, hb_ref, rstd_ref, ssq_ref):
    hb_ref[...] = h.astype(hb_ref.dtype)
    _emit_rstd(jnp.sum(h * h, axis=1, keepdims=True), j, last_j, width, rstd_ref, ssq_ref)


def _emit_rstd(part, j, last_j, width, rstd_ref, ssq_ref):
    @pl.when(j == 0)
    def _():
        ssq_ref[...] = part

    @pl.when(j > 0)
    def _():
        ssq_ref[...] += part

    @pl.when(j == last_j)
    def _():
        rstd_ref[...] = jnp.broadcast_to(lax.rsqrt(ssq_ref[...] * (1.0 / width) + EPS), rstd_ref.shape)


def _resid_mm_kernel(x_ref, w_ref, r_ref, o_ref, *rest, nn, width):
    out = r_ref[...] + jnp.dot(x_ref[...], w_ref[...], preferred_element_type=F32)
    o_ref[...] = out
    if rest:
        _emit_stream(out, pl.program_id(1), nn - 1, width, *rest)


def resid_matmul(x, w, resid, *, emit_norm, bm=1024, bn=512):
    m, kdim = x.shape
    n = w.shape[1]
    bm, bn = _blk(m, bm), _blk(n, bn)
    nn = n // bn
    tile = pl.BlockSpec((bm, bn), lambda i, j: (i, j))
    rows = pl.BlockSpec((bm, LANES), lambda i, j: (i, 0))
    vmem = 2 * (bm * kdim + kdim * bn) * 2 + 4 * bm * bn * 4 + bm * bn * 4
    out_specs = [tile]
    out_shape = [jax.ShapeDtypeStruct((m, n), F32)]
    scratch = []
    if emit_norm:
        out_specs += [tile, rows]
        out_shape += [jax.ShapeDtypeStruct((m, n), BF16), jax.ShapeDtypeStruct((m, LANES), F32)]
        scratch.append(pltpu.VMEM((bm, 1), F32))
        vmem += 2 * bm * bn * 2 + 3 * bm * LANES * 4
    outs = pl.pallas_call(
        functools.partial(_resid_mm_kernel, nn=nn, width=n),
        grid=(m // bm, nn),
        in_specs=[pl.BlockSpec((bm, kdim), lambda i, j: (i, 0)),
                  pl.BlockSpec((kdim, bn), lambda i, j: (0, j)),
                  tile],
        out_specs=out_specs,
        out_shape=out_shape,
        scratch_shapes=scratch,
        compiler_params=_params(("parallel", "arbitrary" if emit_norm else "parallel"), vmem),
        name="resid_matmul",
    )(x, w, resid)
    return outs if emit_norm else outs[0]


def _normed_kernel(x_ref, rs_ref, g_ref, *refs, n_w, scale, has_cast):
    w_refs, refs = refs[:n_w], refs[n_w:]
    if has_cast:
        cast_ref, o_ref, cast_o_ref = refs[:3]
        cast_o_ref[...] = cast_ref[...].astype(cast_o_ref.dtype)
        wb_refs = refs[3:]
    else:
        o_ref, wb_refs = refs[0], refs[1:]

    @pl.when(pl.program_id(1) == 0)
    def _():
        gain = g_ref[...]
        for w_ref, wb_ref in zip(w_refs, wb_refs):
            wb_ref[...] = (w_ref[...] * gain).astype(BF16)

    x = x_ref[...]
    rstd = rs_ref[:, 0:1]
    outs = [jnp.dot(x, wb_ref[...], preferred_element_type=F32) * rstd for wb_ref in wb_refs]
    if n_w == 2:
        g, u = outs
        out = g * jax.nn.sigmoid(g) * u
    else:
        out = outs[0] if scale is None else outs[0] * scale
    o_ref[...] = out.astype(o_ref.dtype)


def normed_matmul(x, rstd, gain, w, *, glu=False, scale=None, cast=None, bm=1024, bn=512):
    m, kdim = x.shape
    n_w = 2 if glu else 1
    n = w.shape[1] // n_w
    bm, bn = _blk(m, bm), _blk(n, bn)
    nn, nm = n // bn, m // bm
    w_specs = [pl.BlockSpec((kdim, bn), lambda j, i, off=off: (0, j + off)) for off in range(0, n_w * nn, nn)]
    vmem = (2 * bm * kdim * 2 + n_w * kdim * bn * (2 * 4 + 2) + 2 * bm * bn * 2 + 2 * kdim * LANES * 4
            + 2 * bm * LANES * 4 + (n_w + 1) * bm * bn * 4)
    in_specs = [pl.BlockSpec((bm, kdim), lambda j, i: (i, 0)),
                pl.BlockSpec((bm, LANES), lambda j, i: (i, 0)),
                pl.BlockSpec((kdim, 1), lambda j, i: (0, 0))] + w_specs
    args = [x, rstd, gain.reshape(kdim, 1)] + [w] * n_w
    out_specs = [pl.BlockSpec((bm, bn), lambda j, i: (i, j))]
    out_shape = [jax.ShapeDtypeStruct((m, n), BF16)]
    slab = _cast_slab(cast, nn * nm) if cast is not None else None
    if slab is not None:
        rows, cols = cast.shape
        last = rows // slab - 1
        side = pl.BlockSpec((slab, cols), lambda j, i: (jnp.minimum(j * nm + i, last), 0))
        in_specs.append(side)
        args.append(cast)
        out_specs.append(side)
        out_shape.append(jax.ShapeDtypeStruct((rows, cols), BF16))
        vmem += 2 * slab * cols * (4 + 2)
    outs = pl.pallas_call(
        functools.partial(_normed_kernel, n_w=n_w, scale=scale, has_cast=slab is not None),
        grid=(nn, nm),
        in_specs=in_specs,
        out_specs=out_specs,
        out_shape=out_shape,
        scratch_shapes=[pltpu.VMEM((kdim, bn), BF16)] * n_w,
        compiler_params=_params(("arbitrary", "arbitrary"), vmem),
        name="normed_glu" if glu else "normed_matmul",
    )(*args)
    return (outs[0], outs[1] if slab is not None else None) if cast is not None else outs[0]


def _cast_slab(a, steps):
    rows = a.shape[0]
    slab = BF16_SUBLANES
    while slab <= rows and (rows % slab or rows // slab > steps):
        slab *= 2
    return slab if slab <= rows and slab * a.shape[1] * 4 <= CAST_SLAB_BYTES else None


def _conv_kernel(b_ref, c_ref, u_ref, ch_ref, uh_ref, w_ref, o_ref, *, blocks_per_seq):
    ts = c_ref.shape[0]
    cu = c_ref[...].astype(F32) * u_ref[...].astype(F32)
    seq_start = (pl.program_id(0) % blocks_per_seq) == 0
    halo = ch_ref[...].astype(F32) * uh_ref[...].astype(F32)
    halo = jnp.where(seq_start, 0.0, halo)
    row = lax.broadcasted_iota(jnp.int32, cu.shape, 0)
    h1 = halo[BF16_SUBLANES - 1:BF16_SUBLANES, :]
    h2 = halo[BF16_SUBLANES - 2:BF16_SUBLANES - 1, :]
    prev1 = jnp.where(row == 0, h1, pltpu.roll(cu, 1, 0))
    prev2 = jnp.where(row == 0, h2, jnp.where(row == 1, h1, pltpu.roll(cu, 2, 0)))
    y = w_ref[0:1, :] * prev2 + w_ref[1:2, :] * prev1 + w_ref[2:3, :] * cu
    o_ref[...] = (b_ref[...].astype(F32) * y).astype(o_ref.dtype)


def short_conv(bcx, conv_w, seq):
    t, d3 = bcx.shape
    d = d3 // 3
    ts = _blk(seq, 1024)
    dblk = _blk(d, 1024)
    nd = d // dblk
    hb = ts // BF16_SUBLANES

    def halo_map(off):
        return lambda i, j: (jnp.maximum(i * hb - 1, 0), j + off)

    vmem = 2 * (4 * ts * dblk * 2 + 2 * BF16_SUBLANES * dblk * 2) + 6 * ts * dblk * 4
    return pl.pallas_call(
        functools.partial(_conv_kernel, blocks_per_seq=seq // ts),
        grid=(t // ts, nd),
        in_specs=[pl.BlockSpec((ts, dblk), lambda i, j: (i, j)),
                  pl.BlockSpec((ts, dblk), lambda i, j: (i, j + nd)),
                  pl.BlockSpec((ts, dblk), lambda i, j: (i, j + 2 * nd)),
                  pl.BlockSpec((BF16_SUBLANES, dblk), halo_map(nd)),
                  pl.BlockSpec((BF16_SUBLANES, dblk), halo_map(2 * nd)),
                  pl.BlockSpec((CONV_WIDTH, dblk), lambda i, j: (0, j))],
        out_specs=pl.BlockSpec((ts, dblk), lambda i, j: (i, j)),
        out_shape=jax.ShapeDtypeStruct((t, d), BF16),
        compiler_params=_params(("parallel", "parallel"), vmem),
        name="short_conv",
    )(bcx, bcx, bcx, bcx, bcx, conv_w)


def _ple_kernel(hb_ref, wg_ref, p_ref, wu_ref, h_ref, rs_ref, o_ref, *rest, nn, width):
    hb, pb, rstd = hb_ref[...], p_ref[...].astype(BF16), rs_ref[:, 0:1]
    bn = o_ref.shape[1]
    halves = [slice(0, bn // 2), slice(bn // 2, bn)] if bn % (2 * MXU_COLUMNS) == 0 else [slice(0, bn)]
    dots = [(jnp.dot(hb, wg_ref[:, c], preferred_element_type=F32),
             jnp.dot(pb, wu_ref[:, c], preferred_element_type=F32)) for c in halves]
    ssq = None
    for c, (pre, up) in zip(halves, dots):
        out = h_ref[:, c] + up * jax.nn.sigmoid(pre * rstd)
        o_ref[:, c] = out
        if rest:
            rest[0][:, c] = out.astype(rest[0].dtype)
            part = jnp.sum(out * out, axis=1, keepdims=True)
            ssq = part if ssq is None else ssq + part
    if rest:
        _emit_rstd(ssq, pl.program_id(1), nn - 1, width, *rest[1:])


def ple(hb, rstd, w_gate, p, w_up, h, *, emit_norm, bm=1024, bn=512):
    m, d = hb.shape
    pd = p.shape[1]
    n = w_gate.shape[1]
    bm, bn = _blk(m, bm), _blk(n, bn)
    nn = n // bn
    tile = pl.BlockSpec((bm, bn), lambda i, j: (i, j))
    rows = pl.BlockSpec((bm, LANES), lambda i, j: (i, 0))
    vmem = (2 * (bm * d * 2 + d * bn * 2 + bm * pd * 4 + pd * bn * 2 + 2 * bm * bn * 4 + bm * LANES * 4)
            + 3 * bm * bn * 4)
    out_specs, out_shape, scratch = [tile], [jax.ShapeDtypeStruct((m, n), F32)], []
    if emit_norm:
        out_specs += [tile, rows]
        out_shape += [jax.ShapeDtypeStruct((m, n), BF16), jax.ShapeDtypeStruct((m, LANES), F32)]
        scratch.append(pltpu.VMEM((bm, 1), F32))
        vmem += 2 * bm * bn * 2 + 3 * bm * LANES * 4
    outs = pl.pallas_call(
        functools.partial(_ple_kernel, nn=nn, width=n),
        grid=(m // bm, nn),
        in_specs=[pl.BlockSpec((bm, d), lambda i, j: (i, 0)),
                  pl.BlockSpec((d, bn), lambda i, j: (0, j)),
                  pl.BlockSpec((bm, pd), lambda i, j: (i, 0)),
                  pl.BlockSpec((pd, bn), lambda i, j: (0, j)),
                  tile, rows],
        out_specs=out_specs,
        out_shape=out_shape,
        scratch_shapes=scratch,
        compiler_params=_params(("parallel", "arbitrary" if emit_norm else "parallel"), vmem),
        name="ple",
    )(hb, w_gate, p, w_up, h, rstd)
    return outs if emit_norm else outs[0]


def _ple_final_kernel(hb_ref, wg_ref, p_ref, wu_ref, h_ref, rs_ref, g_ref, o_ref, ssq_ref, *, nn, width):
    j = pl.program_id(1)
    bn = h_ref.shape[1]
    pre = jnp.dot(hb_ref[...], wg_ref[...], preferred_element_type=F32)
    up = jnp.dot(p_ref[...].astype(BF16), wu_ref[...], preferred_element_type=F32)
    out = h_ref[...] + up * jax.nn.sigmoid(pre * rs_ref[:, 0:1])
    o_ref[:, pl.ds(pl.multiple_of(j * bn, bn), bn)] = out
    part = jnp.sum(out * out, axis=1, keepdims=True)

    @pl.when(j == 0)
    def _():
        ssq_ref[...] = part

    @pl.when(j > 0)
    def _():
        ssq_ref[...] += part

    @pl.when(j == nn - 1)
    def _():
        o_ref[...] = o_ref[...] * lax.rsqrt(ssq_ref[...] * (1.0 / width) + EPS) * g_ref[...]


def ple_final(hb, rstd, w_gate, p, w_up, h, gain, *, bm=512, bn=512):
    m, d = hb.shape
    pd = p.shape[1]
    n = w_gate.shape[1]
    bm, bn = _blk(m, bm), _blk(n, bn)
    nn = n // bn
    vmem = (2 * (bm * d * 2 + d * bn * 2 + bm * pd * 4 + pd * bn * 2 + bm * bn * 4 + bm * LANES * 4 + bm * n * 4)
            + 3 * bm * bn * 4 + bm * n * 4)
    return pl.pallas_call(
        functools.partial(_ple_final_kernel, nn=nn, width=n),
        grid=(m // bm, nn),
        in_specs=[pl.BlockSpec((bm, d), lambda i, j: (i, 0)),
                  pl.BlockSpec((d, bn), lambda i, j: (0, j)),
                  pl.BlockSpec((bm, pd), lambda i, j: (i, 0)),
                  pl.BlockSpec((pd, bn), lambda i, j: (0, j)),
                  pl.BlockSpec((bm, bn), lambda i, j: (i, j)),
                  pl.BlockSpec((bm, LANES), lambda i, j: (i, 0)),
                  pl.BlockSpec((1, n), lambda i, j: (0, 0))],
        out_specs=pl.BlockSpec((bm, n), lambda i, j: (i, 0)),
        out_shape=jax.ShapeDtypeStruct((m, n), F32),
        scratch_shapes=[pltpu.VMEM((bm, 1), F32)],
        compiler_params=_params(("parallel", "arbitrary"), vmem),
        name="ple_final",
    )(hb, w_gate, p, w_up, h, rstd, gain.reshape(1, n))


def _attn_kernel(q_ref, k_ref, v_ref, tri_ref, bias_ref, o_ref, acc_ref, gone_ref, *, tb, rs, group, seq):
    gone_limit = -EXP_ZERO_BELOW * LOG2E

    def scores(q, k_start, nk, bias):
        z = lax.dot_general(q, k_ref[pl.ds(k_start, nk), :], (((1,), (1,)), ((), ())),
                            preferred_element_type=F32)
        return z if bias is None else z + bias

    def split(z):
        sp = jnp.maximum(z, 0.0) + jnp.log(1.0 + jnp.exp2(-jnp.abs(z))) * LOG2E
        hi = sp.astype(BF16)
        return sp, hi, (sp - hi.astype(F32)).astype(BF16)

    def suffix_sum(hi, lo, nk):
        tri = tri_ref[0:nk, 0:nk]
        return jnp.dot(hi, tri, preferred_element_type=F32) + jnp.dot(lo, tri, preferred_element_type=F32)

    def weights(z, sp, suffix, gone):
        log2_w = z - sp - suffix
        walked = suffix[:, 0:1] + sp[:, 0:1]
        if gone is not None:
            log2_w, walked = log2_w - gone, walked + gone
        return jnp.exp2(log2_w).astype(BF16), walked

    def values(w, k_start, nk):
        return jnp.dot(w, v_ref[pl.ds(k_start, nk), :], preferred_element_type=F32)

    def tile(q, k_start, nk, gone, bias):
        z = scores(q, k_start, nk, bias)
        sp, hi, lo = split(z)
        w, gone = weights(z, sp, suffix_sum(hi, lo, nk), gone)
        return values(w, k_start, nk), gone

    def query_blocks(qis, first):
        q_starts = [pl.multiple_of(qi * tb, tb) for qi in qis]
        ns = tb // rs
        units = [(b, s, True) for b in range(len(qis)) for s in range(ns)]
        if not first:
            units += [(b, s, False) for b in range(len(qis)) for s in range(ns)]
        live = [None] * len(units)
        gone_of, pv_of = {}, {}

        def run(stage, u):
            b, s, diagonal = units[u]
            rows = slice(s * rs, (s + 1) * rs)
            nk = (s + 1) * rs if diagonal else tb
            k_start = q_starts[b] if diagonal else pl.multiple_of(q_starts[b] - tb, tb)
            if stage == 0:
                q = q_ref[pl.ds(q_starts[b] + s * rs, rs), :]
                live[u] = (scores(q, k_start, nk, bias_ref[rows, 0:nk] if diagonal else None),)
            elif stage == 1:
                (z,) = live[u]
                live[u] = (z,) + split(z)
            elif stage == 2:
                z, sp, hi, lo = live[u]
                live[u] = (z, sp, suffix_sum(hi, lo, nk))
            elif stage == 3:
                z, sp, suffix = live[u]
                w, gone_of[b, s] = weights(z, sp, suffix, gone_of.get((b, s)))
                live[u] = (w,)
            else:
                (w,) = live[u]
                live[u] = None
                pv = values(w, k_start, nk)
                if first:
                    o_ref[pl.ds(q_starts[b] + s * rs, rs), :] = pv.astype(o_ref.dtype)
                elif diagonal:
                    pv_of[b, s] = pv
                else:
                    acc_ref[b, rows, :] = pv_of.pop((b, s)) + pv
                    gone_ref[b, rows, :] = gone_of[b, s]

        n_stages = 5
        for step in range(len(units) + n_stages - 1):
            for stage in range(n_stages):
                if 0 <= step - stage < len(units):
                    run(stage, step - stage)
        if first:
            return

        for b, qi in enumerate(qis):
            def cond(carry):
                kb, alive = carry
                return jnp.logical_and(kb >= 0, alive > 0)

            def body(carry, b=b):
                kb, _ = carry
                pv_far, gone_far = tile(q_ref[pl.ds(q_starts[b], tb), :], pl.multiple_of(kb * tb, tb), tb,
                                        gone_ref[b], None)
                acc_ref[b] += pv_far
                gone_ref[b] = gone_far
                return kb - 1, (jnp.min(gone_far) < gone_limit).astype(jnp.int32)

            lax.while_loop(cond, body, (qi - 2, (jnp.min(gone_ref[b]) < gone_limit).astype(jnp.int32)))
            o_ref[pl.ds(q_starts[b], tb), :] = acc_ref[b].astype(o_ref.dtype)

    nq = seq // tb
    query_blocks([0], True)
    n_iter = (nq - 1) // group
    lead = nq - 1 - n_iter * group
    if lead:
        query_blocks(list(range(1, 1 + lead)), False)

    def step(it, carry):
        query_blocks([1 + lead + it * group + g for g in range(group)], False)
        return carry

    lax.fori_loop(0, n_iter, step, 0)


def stick_breaking_attention(q, kv, batch, seq, *, tb=256, rs=128, group=4):
    t, hd = q.shape
    n_heads = hd // HEAD_DIM
    tb = _blk(seq, tb)
    rows = lax.broadcasted_iota(jnp.int32, (tb, tb), 0)
    cols = lax.broadcasted_iota(jnp.int32, (tb, tb), 1)
    tri = (rows > cols).astype(BF16)
    bias = jnp.where(cols < rows, 0.0, MASKED_SCORE).astype(F32)
    vmem = 2 * (4 * seq * HEAD_DIM * 2 + tb * tb * 6) + 16 * tb * tb * 4
    head = lambda b, h: (b, h)
    return pl.pallas_call(
        functools.partial(_attn_kernel, tb=tb, rs=min(rs, tb), group=group, seq=seq),
        grid=(batch, n_heads),
        in_specs=[pl.BlockSpec((seq, HEAD_DIM), head),
                  pl.BlockSpec((seq, HEAD_DIM), head),
                  pl.BlockSpec((seq, HEAD_DIM), lambda b, h: (b, n_heads + h)),
                  pl.BlockSpec((tb, tb), lambda b, h: (0, 0)),
                  pl.BlockSpec((tb, tb), lambda b, h: (0, 0))],
        out_specs=pl.BlockSpec((seq, HEAD_DIM), head),
        out_shape=jax.ShapeDtypeStruct((t, hd), BF16),
        scratch_shapes=[pltpu.VMEM((group, tb, HEAD_DIM), F32), pltpu.VMEM((group, tb, 1), F32)],
        compiler_params=_params(("parallel", "parallel"), vmem),
        name="stick_breaking_attention",
    )(q, kv, kv, tri, bias)


INFO_IDX1, INFO_IDX2, INFO_RANK1, INFO_RANK2, INFO_W1, INFO_W2 = range(6)


def _pack_bf16_pairs(x):
    half = x.shape[1] // 2
    bits = lambda v: lax.bitcast_convert_type(v.astype(BF16).astype(F32), jnp.uint32)
    return (bits(x[:, :half]) >> 16) | (bits(x[:, half:]) & jnp.uint32(0xFFFF0000))


def _unpack_bf16_pairs(words):
    low = lax.bitcast_convert_type(words << 16, F32).astype(BF16)
    high = lax.bitcast_convert_type(words & jnp.uint32(0xFFFF0000), F32).astype(BF16)
    return low, high


def _router_kernel(h_ref, g_ref, wr_ref, br_ref, tri_ref, hn_ref, info_ref, cnt_ref, count_ref):
    @pl.when(pl.program_id(0) == 0)
    def _():
        count_ref[...] = jnp.zeros_like(count_ref)

    x = h_ref[...]
    hn = x * lax.rsqrt(jnp.mean(x * x, axis=-1, keepdims=True) + EPS) * g_ref[...]
    hn_ref[...] = _pack_bf16_pairs(hn)
    logits = jnp.dot(hn, wr_ref[...], precision=lax.Precision.HIGHEST,
                     preferred_element_type=F32) + br_ref[...]
    lane = lax.broadcasted_iota(jnp.int32, logits.shape, 1)
    m1 = jnp.max(logits, axis=1, keepdims=True)
    i1 = jnp.min(jnp.where(logits == m1, lane, LANES), axis=1, keepdims=True)
    rest = jnp.where(lane == i1, -jnp.inf, logits)
    m2 = jnp.max(rest, axis=1, keepdims=True)
    i2 = jnp.min(jnp.where(rest == m2, lane, LANES), axis=1, keepdims=True)
    e2 = jnp.exp(m2 - m1)
    w1 = 1.0 / (1.0 + e2)
    w2 = e2 / (1.0 + e2)
    sel1 = lane == i1
    sel2 = lane == i2
    sel = jnp.logical_or(sel1, sel2).astype(F32)
    before = jnp.dot(tri_ref[...], sel.astype(BF16), preferred_element_type=F32) + count_ref[0:1, :]
    rank1 = jnp.sum(jnp.where(sel1, before, 0.0), axis=1, keepdims=True)
    rank2 = jnp.sum(jnp.where(sel2, before, 0.0), axis=1, keepdims=True)
    count_ref[...] = count_ref[...] + jnp.sum(sel, axis=0, keepdims=True)
    cnt_ref[...] = count_ref[...]
    info = jnp.zeros(logits.shape, F32)
    for slot, val in ((INFO_IDX1, i1.astype(F32)), (INFO_IDX2, i2.astype(F32)), (INFO_RANK1, rank1),
                      (INFO_RANK2, rank2), (INFO_W1, w1), (INFO_W2, w2)):
        info = jnp.where(lane == slot, val, info)
    info_ref[...] = info


def route(h, gain, w_router, b_router, *, bt=512):
    t, d = h.shape
    n_exp = w_router.shape[1]
    bt = _blk(t, bt)
    wr = jnp.zeros((d, LANES), F32).at[:, :n_exp].set(w_router)
    br = jnp.full((1, LANES), -1e30, F32).at[0, :n_exp].set(b_router)
    tri = (lax.broadcasted_iota(jnp.int32, (bt, bt), 1)
           < lax.broadcasted_iota(jnp.int32, (bt, bt), 0)).astype(BF16)
    vmem = 2 * (2 * bt * d * 4 + d * LANES * 4 + bt * bt * 2 + bt * LANES * 4) + 4 * bt * d * 4
    hn, info, counts = pl.pallas_call(
        _router_kernel,
        grid=(t // bt,),
        in_specs=[pl.BlockSpec((bt, d), lambda i: (i, 0)),
                  pl.BlockSpec((1, d), lambda i: (0, 0)),
                  pl.BlockSpec((d, LANES), lambda i: (0, 0)),
                  pl.BlockSpec((1, LANES), lambda i: (0, 0)),
                  pl.BlockSpec((bt, bt), lambda i: (0, 0))],
        out_specs=[pl.BlockSpec((bt, d // 2), lambda i: (i, 0)),
                   pl.BlockSpec((bt, LANES), lambda i: (i, 0)),
                   pl.BlockSpec((8, LANES), lambda i: (0, 0))],
        out_shape=[jax.ShapeDtypeStruct((t, d // 2), jnp.uint32),
                   jax.ShapeDtypeStruct((t, LANES), F32),
                   jax.ShapeDtypeStruct((8, LANES), F32)],
        scratch_shapes=[pltpu.VMEM((8, LANES), F32)],
        compiler_params=_params(("arbitrary",), vmem),
        name="moe_router",
    )(h, gain, wr, br, tri)
    return hn, info, counts[0, :n_exp]


def _combine_kernel(pos_ref, pos_next_ref, y_ref, w_ref, h_ref, o_ref, hb_ref, rstd_ref, buf_ref, sems, *, tc):
    i = pl.program_id(0)
    slot = i % 2

    def copy(pos, slot, j, k):
        return pltpu.make_async_copy(y_ref.at[pl.ds(pos[k, j], 1)], buf_ref.at[slot, k, pl.ds(j, 1)],
                                     sems.at[slot])

    def issue(pos, slot):
        def token(j, _):
            copy(pos, slot, j, 0).start(priority=0)
            copy(pos, slot, j, 1).start(priority=1)
            return 0
        lax.fori_loop(0, tc, token, 0)

    @pl.when(i == 0)
    def _():
        issue(pos_ref, 0)

    @pl.when(i + 1 < pl.num_programs(0))
    def _():
        issue(pos_next_ref, 1 - slot)

    def wait(j, _):
        copy(pos_ref, slot, j, 0).wait()
        copy(pos_ref, slot, j, 1).wait()
        return 0

    lax.fori_loop(0, tc, wait, 0, unroll=4)
    w = w_ref[...]
    out = (h_ref[...] + w[:, INFO_W1:INFO_W1 + 1] * buf_ref[slot, 0]
           + w[:, INFO_W2:INFO_W2 + 1] * buf_ref[slot, 1])
    o_ref[...] = out
    hb_ref[...] = out.astype(hb_ref.dtype)
    rstd_ref[...] = jnp.broadcast_to(lax.rsqrt(jnp.mean(out * out, axis=1, keepdims=True) + EPS),
                                     rstd_ref.shape)


def combine(y_sorted, pos, info, h, *, tc=256):
    t, d = h.shape
    tc = _blk(t, tc)
    last = t // tc - 1
    vmem = 4 * tc * d * 4 + 2 * (2 * tc * d * 4 + 2 * tc * LANES * 4 + tc * d * 2) + 2 * tc * d * 4
    return pl.pallas_call(
        functools.partial(_combine_kernel, tc=tc),
        grid=(t // tc,),
        in_specs=[pl.BlockSpec((TOP_K, tc), lambda i: (0, i), memory_space=pltpu.SMEM),
                  pl.BlockSpec((TOP_K, tc), lambda i: (0, jnp.minimum(i + 1, last)), memory_space=pltpu.SMEM),
                  pl.BlockSpec(memory_space=pl.ANY),
                  pl.BlockSpec((tc, LANES), lambda i: (i, 0)),
                  pl.BlockSpec((tc, d), lambda i: (i, 0))],
        out_specs=[pl.BlockSpec((tc, d), lambda i: (i, 0)),
                   pl.BlockSpec((tc, d), lambda i: (i, 0)),
                   pl.BlockSpec((tc, LANES), lambda i: (i, 0))],
        out_shape=[jax.ShapeDtypeStruct((t, d), F32),
                   jax.ShapeDtypeStruct((t, d), BF16),
                   jax.ShapeDtypeStruct((t, LANES), F32)],
        scratch_shapes=[pltpu.VMEM((2, TOP_K, tc, d), F32), pltpu.SemaphoreType.DMA((2,))],
        compiler_params=_params(("arbitrary",), vmem),
        name="moe_combine",
    )(pos, pos, y_sorted, info, h)


GATHER_SLOTS = 3
ROW_PARTS = 4


def _expert_glu_kernel(be_ref, nvalid_ref, rows_ref, tok_ref, src_ref, wg_ref, wu_ref, o_ref, buf_ref, x_ref, sems, *,
                       bm):
    del be_ref
    i, j, nf = pl.program_id(0), pl.program_id(1), pl.num_programs(1)
    n_valid = nvalid_ref[0]
    share = bm // nf

    def copy(block, r):
        slot = block % GATHER_SLOTS
        return pltpu.make_async_copy(src_ref.at[pl.ds(tok_ref[block * bm + r], 1)],
                                     buf_ref.at[slot, pl.ds(r, 1)], sems.at[slot])

    @pl.when(jnp.logical_and(i == 0, j == 0))
    def _():
        def row(r, _):
            copy(0, r).start(priority=0)
            copy(1, r).start(priority=1)
            return 0
        lax.fori_loop(0, bm, row, 0)

    @pl.when(jnp.logical_and(j == 0, i < n_valid + GATHER_SLOTS - 1))
    def _():
        def row(r, _):
            copy(i, r).wait()
            return 0
        lax.fori_loop(0, bm, row, 0, unroll=8)
        half = buf_ref.shape[2]
        x_ref[:, :half], x_ref[:, half:] = _unpack_bf16_pairs(buf_ref[i % GATHER_SLOTS])

    part = bm // ROW_PARTS
    parts = (rows_ref[i] + part - 1) // part
    for q in range(1, ROW_PARTS + 1):
        @pl.when(parts == q)
        def _(n=q * part):
            for r in range(share):
                copy(i + GATHER_SLOTS - 1, j * share + r).start(priority=r % 2)
            x = x_ref[0:n, :]
            g = jnp.dot(x, wg_ref[...], preferred_element_type=F32)
            u = jnp.dot(x, wu_ref[...], preferred_element_type=F32)
            o_ref[0:n, :] = (g * jax.nn.sigmoid(g) * u).astype(o_ref.dtype)
            if n < bm:
                o_ref[n:, :] = jnp.zeros((bm - n, o_ref.shape[1]), o_ref.dtype)

    @pl.when(parts == 0)
    def _():
        o_ref[...] = jnp.zeros_like(o_ref)


def expert_glu(src, tok_of_row, w_gu, block_expert, n_valid, block_rows, *, bm, bn=512):
    r = tok_of_row.shape[0]
    d = src.shape[1] * 2
    f = w_gu.shape[2] // 2
    bn = _blk(f, bn)
    nf = f // bn
    assert bm % nf == 0
    vmem = (GATHER_SLOTS * bm * d * 2 + bm * d * 2 + 2 * (2 * d * bn * 2 + bm * bn * 2) + 3 * bm * bn * 4
            + bm * d * 4)
    col = lambda i, j, nv: jnp.where(i < nv[0], j, 0)
    return pl.pallas_call(
        functools.partial(_expert_glu_kernel, bm=bm),
        grid_spec=pltpu.PrefetchScalarGridSpec(
            num_scalar_prefetch=4,
            grid=(r // bm, nf),
            in_specs=[pl.BlockSpec(memory_space=pl.ANY),
                      pl.BlockSpec((None, d, bn), lambda i, j, be, nv, rows, tok: (be[i], 0, col(i, j, nv))),
                      pl.BlockSpec((None, d, bn), lambda i, j, be, nv, rows, tok: (be[i], 0, col(i, j, nv) + nf))],
            out_specs=pl.BlockSpec((bm, bn), lambda i, j, be, nv, rows, tok: (i, j)),
            scratch_shapes=[pltpu.VMEM((GATHER_SLOTS, bm, d // 2), jnp.uint32), pltpu.VMEM((bm, d), BF16),
                            pltpu.SemaphoreType.DMA((GATHER_SLOTS,))]),
        out_shape=jax.ShapeDtypeStruct((r, f), BF16),
        compiler_params=_params(("arbitrary", "arbitrary"), vmem),
        name="moe_expert_glu",
    )(block_expert, n_valid, block_rows, tok_of_row, src, w_gu, w_gu)


def _expert_down_kernel(be_ref, nvalid_ref, rows_ref, x_ref, w_ref, o_ref):
    del be_ref, nvalid_ref
    bm = o_ref.shape[0]
    part = bm // ROW_PARTS
    parts = (rows_ref[pl.program_id(0)] + part - 1) // part
    for q in range(1, ROW_PARTS + 1):
        @pl.when(parts == q)
        def _(n=q * part):
            o_ref[0:n, :] = jnp.dot(x_ref[0:n, :], w_ref[...], preferred_element_type=F32)
            if n < bm:
                o_ref[n:, :] = jnp.zeros((bm - n, o_ref.shape[1]), o_ref.dtype)

    @pl.when(parts == 0)
    def _():
        o_ref[...] = jnp.zeros_like(o_ref)


def expert_down(a_sorted, w_down, block_expert, n_valid, block_rows, *, bm, bn=1024):
    r, f = a_sorted.shape
    d = w_down.shape[2]
    bn = _blk(d, bn)
    vmem = 2 * (bm * f * 2 + f * bn * 2 + bm * bn * 4) + bm * bn * 4
    return pl.pallas_call(
        _expert_down_kernel,
        grid_spec=pltpu.PrefetchScalarGridSpec(
            num_scalar_prefetch=3,
            grid=(r // bm, d // bn),
            in_specs=[pl.BlockSpec((bm, f), lambda i, j, be, nv, rows: (i, 0)),
                      pl.BlockSpec((None, f, bn),
                                   lambda i, j, be, nv, rows: (be[i], 0, jnp.where(i < nv[0], j, 0)))],
            out_specs=pl.BlockSpec((bm, bn), lambda i, j, be, nv, rows: (i, j))),
        out_shape=jax.ShapeDtypeStruct((r, d), F32),
        compiler_params=_params(("arbitrary", "arbitrary"), vmem),
        name="moe_expert_down",
    )(block_expert, n_valid, block_rows, a_sorted, w_down)


def moe(h, gain, w_router, b_router, w_gu, w_down, *, bm=512):
    t, d = h.shape
    n_exp = w_router.shape[1]
    bm = _blk(t, bm)
    hn, info, counts = route(h, gain, w_router, b_router)
    counts = counts.astype(jnp.int32)
    blocks_per_expert = (counts + bm - 1) // bm
    block_end = jnp.cumsum(blocks_per_expert)
    row_offset = (block_end - blocks_per_expert) * bm
    n_blocks = TOP_K * t // bm + n_exp + GATHER_SLOTS - 1
    n_valid = block_end[-1:]
    all_blocks = jnp.arange(n_blocks, dtype=jnp.int32)
    block_ids = jnp.minimum(all_blocks, n_valid[0] - 1)
    block_expert = jnp.sum((block_ids[:, None] >= block_end[None, :]).astype(jnp.int32), axis=1)
    rows_before = (block_ids - (block_end - blocks_per_expert)[block_expert]) * bm
    block_rows = jnp.where(all_blocks < n_valid[0], jnp.clip(counts[block_expert] - rows_before, 0, bm), 0)
    idx = info[:, INFO_IDX1:INFO_IDX2 + 1].astype(jnp.int32)
    rank = info[:, INFO_RANK1:INFO_RANK2 + 1].astype(jnp.int32)
    pos = (row_offset[idx] + rank).T
    token_ids = jnp.broadcast_to(jnp.arange(t, dtype=jnp.int32), (TOP_K, t))
    tok_of_row = jnp.zeros((n_blocks * bm,), jnp.int32).at[pos.reshape(-1)].set(
        token_ids.reshape(-1), unique_indices=True)
    a_sorted = expert_glu(hn, tok_of_row, w_gu, block_expert, n_valid, block_rows, bm=bm)
    y_sorted = expert_down(a_sorted, w_down, block_expert, n_valid, block_rows, bm=bm)
    return combine(y_sorted, pos, info, h)


def kernel(x, p, a_norm, a_w_in, a_conv_w, a_w_out, kv_norm, w_kv, b_norm, b_w_q, b_w_o, ffn_norm, dense_w_gu, dense_w_down, moe_w_router, moe_b_router, moe_w_gu, moe_w_down, ple_norm, ple_w_up, ple_w_gate, final_norm):
    batch, seq, d = x.shape
    depth = p.shape[0]
    n_a = a_norm.shape[0]
    t = batch * seq
    p = p.reshape(depth, t, p.shape[-1])
    bf = lambda w: w.astype(BF16)

    h = x.reshape(t, d)
    hb, rstd = stream_operands(h)
    kv = None

    to_convert = [(name, m, w[m].reshape(-1, w.shape[-1]))
                  for m in range(moe_w_gu.shape[0]) for name, w in (("down", moe_w_down), ("gu", moe_w_gu))]
    converted = {}

    def hosting(*args, **kwargs):
        if not to_convert:
            return normed_matmul(*args, **kwargs)
        name, m, w2d = to_convert[0]
        out, wb = normed_matmul(*args, cast=w2d, **kwargs)
        if wb is not None:
            converted[name, m] = wb
            to_convert.pop(0)
        return out

    def expert_weights(name, w, m):
        to_convert[:] = [job for job in to_convert if job[:2] != (name, m)]
        wb = converted.get((name, m))
        return bf(w[m]) if wb is None else wb.reshape(w.shape[1:])
    for i in range(depth):
        if i < n_a:
            bcx = hosting(hb, rstd, a_norm[i], a_w_in[i])
            mixed, w_out = short_conv(bcx, a_conv_w[i], seq), a_w_out[i]
        else:
            j = i - n_a
            if j == 0:
                kv = hosting(hb, rstd, kv_norm, w_kv)
            q = hosting(hb, rstd, b_norm[j], b_w_q[j], scale=HEAD_DIM ** -0.5 * LOG2E)
            mixed, w_out = stick_breaking_attention(q, kv, batch, seq), b_w_o[j]
        if i % 2 == 0:
            h, hb, rstd = resid_matmul(mixed, bf(w_out), h, emit_norm=True)
            act = hosting(hb, rstd, ffn_norm[i], dense_w_gu[i // 2], glu=True, bn=256)
            h, hb, rstd = resid_matmul(act, bf(dense_w_down[i // 2]), h, emit_norm=True, bm=512)
        else:
            m = i // 2
            h = resid_matmul(mixed, bf(w_out), h, emit_norm=False)
            h, hb, rstd = moe(h, ffn_norm[i:i + 1], moe_w_router[m], moe_b_router[m],
                              expert_weights("gu", moe_w_gu, m), expert_weights("down", moe_w_down, m))
        w_gate = bf(ple_w_gate[i] * ple_norm[i][:, None])
        if i + 1 < depth:
            h, hb, rstd = ple(hb, rstd, w_gate, p[i], bf(ple_w_up[i]), h, emit_norm=True)
        else:
            h = ple_final(hb, rstd, w_gate, p[i], bf(ple_w_up[i]), h, final_norm)
    return h.reshape(batch, seq, d)
```

```python
import functools

import jax
import jax.numpy as jnp
from jax import lax
from jax.experimental import pallas as pl
from jax.experimental.pallas import tpu as pltpu

F32 = jnp.float32
BF16 = jnp.bfloat16

EPS = 1e-6
HEAD_DIM = 128
CONV_WIDTH = 3
TOP_K = 2
LANES = 128
BF16_SUBLANES = 16
MXU_COLUMNS = 256
VMEM_BUDGET_V7X = 60000 * 1024
VMEM_SLACK = 8 * 1024 * 1024
CAST_SLAB_BYTES = 4 * 1024 * 1024
LOG2E = 1.4426950408889634
EXP_ZERO_BELOW = -110.0
MASKED_SCORE = -1e30


def _params(semantics, block_bytes):
    limit = min(VMEM_BUDGET_V7X, int(block_bytes) + VMEM_SLACK)
    return pltpu.CompilerParams(dimension_semantics=semantics, vmem_limit_bytes=limit)


def _blk(dim, pref):
    b = min(dim, pref)
    while dim % b:
        b //= 2
    return b


def _rmsnorm_kernel(x_ref, g_ref, *o_refs):
    x = x_ref[...]
    y = x * lax.rsqrt(jnp.mean(x * x, axis=-1, keepdims=True) + EPS)
    for n, o_ref in enumerate(o_refs):
        o_ref[...] = (y * g_ref[n:n + 1, :]).astype(o_ref.dtype)


def rmsnorm(x, gains, out_dtype):
    t, d = x.shape
    g = gains.shape[0]
    bt = _blk(t, 256)
    out_bytes = jnp.dtype(out_dtype).itemsize
    outs = pl.pallas_call(
        _rmsnorm_kernel,
        grid=(t // bt,),
        in_specs=[pl.BlockSpec((bt, d), lambda i: (i, 0)),
                  pl.BlockSpec((g, d), lambda i: (0, 0))],
        out_specs=[pl.BlockSpec((bt, d), lambda i: (i, 0))] * g,
        out_shape=[jax.ShapeDtypeStruct((t, d), out_dtype)] * g,
        compiler_params=_params(("parallel",), 2 * bt * d * (4 + g * out_bytes)),
        name="rmsnorm",
    )(x, gains)
    return outs


def _stream_kernel(x_ref, xb_ref, rstd_ref):
    x = x_ref[...]
    xb_ref[...] = x.astype(xb_ref.dtype)
    rstd_ref[...] = jnp.broadcast_to(lax.rsqrt(jnp.mean(x * x, axis=-1, keepdims=True) + EPS), rstd_ref.shape)


def stream_operands(x):
    t, d = x.shape
    bt = _blk(t, 256)
    return pl.pallas_call(
        _stream_kernel,
        grid=(t // bt,),
        in_specs=[pl.BlockSpec((bt, d), lambda i: (i, 0))],
        out_specs=[pl.BlockSpec((bt, d), lambda i: (i, 0)), pl.BlockSpec((bt, LANES), lambda i: (i, 0))],
        out_shape=[jax.ShapeDtypeStruct((t, d), BF16), jax.ShapeDtypeStruct((t, LANES), F32)],
        compiler_params=_params(("parallel",), 2 * bt * (d * 6 + LANES * 4) + bt * d * 4),
        name="stream_operands",
    )(x)


def _emit_stream(h, j, last_j, width, hb_ref, rstd_ref, ssq_ref):
    hb_ref[...] = h.astype(hb_ref.dtype)
    _emit_rstd(jnp.sum(h * h, axis=1, keepdims=True), j, last_j, width, rstd_ref, ssq_ref)


def _emit_rstd(part, j, last_j, width, rstd_ref, ssq_ref):
    @pl.when(j == 0)
    def _():
        ssq_ref[...] = part

    @pl.when(j > 0)
    def _():
        ssq_ref[...] += part

    @pl.when(j == last_j)
    def _():
        rstd_ref[...] = jnp.broadcast_to(lax.rsqrt(ssq_ref[...] * (1.0 / width) + EPS), rstd_ref.shape)


WEIGHT_RING = 3


def _resid_mm_kernel(x_ref, w_ref, r_ref, o_ref, *rest, nn, width, n_steps):
    if n_steps:
        *rest, wbuf_ref, sems = rest
        s = pl.program_id(0) * nn + pl.program_id(1)
        bn = o_ref.shape[1]

        def fetch(step):
            col = pl.multiple_of((step % nn) * bn, bn)
            slot = step % WEIGHT_RING
            return pltpu.make_async_copy(w_ref.at[:, pl.ds(col, bn)], wbuf_ref.at[slot], sems.at[slot])

        @pl.when(s == 0)
        def _():
            for first in range(WEIGHT_RING - 1):
                fetch(jnp.int32(first)).start()

        @pl.when(s + WEIGHT_RING - 1 < n_steps)
        def _():
            fetch(s + WEIGHT_RING - 1).start()

        fetch(s).wait()
        w = wbuf_ref[s % WEIGHT_RING]
    else:
        w = w_ref[...]
    out = r_ref[...] + jnp.dot(x_ref[...], w, preferred_element_type=F32)
    o_ref[...] = out
    if rest:
        _emit_stream(out, pl.program_id(1), nn - 1, width, *rest)


def resid_matmul(x, w, resid, *, emit_norm, ring=False, bm=1024, bn=512):
    m, kdim = x.shape
    n = w.shape[1]
    bm, bn = _blk(m, bm), _blk(n, bn)
    nn = n // bn
    n_steps = (m // bm) * nn
    ring = ring and n_steps >= WEIGHT_RING
    tile = pl.BlockSpec((bm, bn), lambda i, j: (i, j))
    rows = pl.BlockSpec((bm, LANES), lambda i, j: (i, 0))
    vmem = (2 * bm * kdim + (WEIGHT_RING if ring else 2) * kdim * bn) * 2 + 4 * bm * bn * 4 + bm * bn * 4
    out_specs = [tile]
    out_shape = [jax.ShapeDtypeStruct((m, n), F32)]
    scratch = []
    if emit_norm:
        out_specs += [tile, rows]
        out_shape += [jax.ShapeDtypeStruct((m, n), BF16), jax.ShapeDtypeStruct((m, LANES), F32)]
        scratch.append(pltpu.VMEM((bm, 1), F32))
        vmem += 2 * bm * bn * 2 + 3 * bm * LANES * 4
    if ring:
        scratch += [pltpu.VMEM((WEIGHT_RING, kdim, bn), BF16), pltpu.SemaphoreType.DMA((WEIGHT_RING,))]
    weights = pl.BlockSpec(memory_space=pl.ANY) if ring else pl.BlockSpec((kdim, bn), lambda i, j: (0, j))
    outs = pl.pallas_call(
        functools.partial(_resid_mm_kernel, nn=nn, width=n, n_steps=n_steps if ring else 0),
        grid=(m // bm, nn),
        in_specs=[pl.BlockSpec((bm, kdim), lambda i, j: (i, 0)), weights, tile],
        out_specs=out_specs,
        out_shape=out_shape,
        scratch_shapes=scratch,
        compiler_params=_params(("arbitrary" if ring else "parallel",
                                 "arbitrary" if emit_norm or ring else "parallel"), vmem),
        name="resid_matmul",
    )(x, w, resid)
    return outs if emit_norm else outs[0]


def _normed_kernel(x_ref, rs_ref, g_ref, *refs, n_w, scale, has_cast):
    w_refs, refs = refs[:n_w], refs[n_w:]
    if has_cast:
        cast_ref, o_ref, cast_o_ref = refs[:3]
        cast_o_ref[...] = cast_ref[...].astype(cast_o_ref.dtype)
        wb_refs = refs[3:]
    else:
        o_ref, wb_refs = refs[0], refs[1:]

    @pl.when(pl.program_id(1) == 0)
    def _():
        gain = g_ref[...]
        for w_ref, wb_ref in zip(w_refs, wb_refs):
            wb_ref[...] = (w_ref[...] * gain).astype(BF16)

    x = x_ref[...]
    rstd = rs_ref[:, 0:1]
    outs = [jnp.dot(x, wb_ref[...], preferred_element_type=F32) * rstd for wb_ref in wb_refs]
    if n_w == 2:
        g, u = outs
        out = g * jax.nn.sigmoid(g) * u
    else:
        out = outs[0] if scale is None else outs[0] * scale
    o_ref[...] = out.astype(o_ref.dtype)


def normed_matmul(x, rstd, gain, w, *, glu=False, scale=None, cast=None, bm=1024, bn=512):
    m, kdim = x.shape
    n_w = 2 if glu else 1
    n = w.shape[1] // n_w
    bm, bn = _blk(m, bm), _blk(n, bn)
    nn, nm = n // bn, m // bm
    w_specs = [pl.BlockSpec((kdim, bn), lambda j, i, off=off: (0, j + off)) for off in range(0, n_w * nn, nn)]
    vmem = (2 * bm * kdim * 2 + n_w * kdim * bn * (2 * 4 + 2) + 2 * bm * bn * 2 + 2 * kdim * LANES * 4
            + 2 * bm * LANES * 4 + (n_w + 1) * bm * bn * 4)
    in_specs = [pl.BlockSpec((bm, kdim), lambda j, i: (i, 0)),
                pl.BlockSpec((bm, LANES), lambda j, i: (i, 0)),
                pl.BlockSpec((kdim, 1), lambda j, i: (0, 0))] + w_specs
    args = [x, rstd, gain.reshape(kdim, 1)] + [w] * n_w
    out_specs = [pl.BlockSpec((bm, bn), lambda j, i: (i, j))]
    out_shape = [jax.ShapeDtypeStruct((m, n), BF16)]
    slab = _cast_slab(cast, nn * nm) if cast is not None else None
    if slab is not None:
        rows, cols = cast.shape
        last = rows // slab - 1
        side = pl.BlockSpec((slab, cols), lambda j, i: (jnp.minimum(j * nm + i, last), 0))
        in_specs.append(side)
        args.append(cast)
        out_specs.append(side)
        out_shape.append(jax.ShapeDtypeStruct((rows, cols), BF16))
        vmem += 2 * slab * cols * (4 + 2)
    outs = pl.pallas_call(
        functools.partial(_normed_kernel, n_w=n_w, scale=scale, has_cast=slab is not None),
        grid=(nn, nm),
        in_specs=in_specs,
        out_specs=out_specs,
        out_shape=out_shape,
        scratch_shapes=[pltpu.VMEM((kdim, bn), BF16)] * n_w,
        compiler_params=_params(("arbitrary", "arbitrary"), vmem),
        name="normed_glu" if glu else "normed_matmul",
    )(*args)
    return (outs[0], outs[1] if slab is not None else None) if cast is not None else outs[0]


def _cast_slab(a, steps):
    rows = a.shape[0]
    slab = BF16_SUBLANES
    while slab <= rows and (rows % slab or rows // slab > steps):
        slab *= 2
    return slab if slab <= rows and slab * a.shape[1] * 4 <= CAST_SLAB_BYTES else None


def _conv_kernel(b_ref, c_ref, u_ref, ch_ref, uh_ref, w_ref, o_ref, *, blocks_per_seq):
    ts = c_ref.shape[0]
    cu = c_ref[...].astype(F32) * u_ref[...].astype(F32)
    seq_start = (pl.program_id(0) % blocks_per_seq) == 0
    halo = ch_ref[...].astype(F32) * uh_ref[...].astype(F32)
    halo = jnp.where(seq_start, 0.0, halo)
    row = lax.broadcasted_iota(jnp.int32, cu.shape, 0)
    h1 = halo[BF16_SUBLANES - 1:BF16_SUBLANES, :]
    h2 = halo[BF16_SUBLANES - 2:BF16_SUBLANES - 1, :]
    prev1 = jnp.where(row == 0, h1, pltpu.roll(cu, 1, 0))
    prev2 = jnp.where(row == 0, h2, jnp.where(row == 1, h1, pltpu.roll(cu, 2, 0)))
    y = w_ref[0:1, :] * prev2 + w_ref[1:2, :] * prev1 + w_ref[2:3, :] * cu
    o_ref[...] = (b_ref[...].astype(F32) * y).astype(o_ref.dtype)


def short_conv(bcx, conv_w, seq):
    t, d3 = bcx.shape
    d = d3 // 3
    ts = _blk(seq, 1024)
    dblk = _blk(d, 1024)
    nd = d // dblk
    hb = ts // BF16_SUBLANES

    def halo_map(off):
        return lambda i, j: (jnp.maximum(i * hb - 1, 0), j + off)

    vmem = 2 * (4 * ts * dblk * 2 + 2 * BF16_SUBLANES * dblk * 2) + 6 * ts * dblk * 4
    return pl.pallas_call(
        functools.partial(_conv_kernel, blocks_per_seq=seq // ts),
        grid=(t // ts, nd),
        in_specs=[pl.BlockSpec((ts, dblk), lambda i, j: (i, j)),
                  pl.BlockSpec((ts, dblk), lambda i, j: (i, j + nd)),
                  pl.BlockSpec((ts, dblk), lambda i, j: (i, j + 2 * nd)),
                  pl.BlockSpec((BF16_SUBLANES, dblk), halo_map(nd)),
                  pl.BlockSpec((BF16_SUBLANES, dblk), halo_map(2 * nd)),
                  pl.BlockSpec((CONV_WIDTH, dblk), lambda i, j: (0, j))],
        out_specs=pl.BlockSpec((ts, dblk), lambda i, j: (i, j)),
        out_shape=jax.ShapeDtypeStruct((t, d), BF16),
        compiler_params=_params(("parallel", "parallel"), vmem),
        name="short_conv",
    )(bcx, bcx, bcx, bcx, bcx, conv_w)


def _ple_kernel(hb_ref, wg_ref, p_ref, wu_ref, h_ref, rs_ref, o_ref, *rest, nn, width):
    hb, pb, rstd = hb_ref[...], p_ref[...].astype(BF16), rs_ref[:, 0:1]
    bn = o_ref.shape[1]
    halves = [slice(0, bn // 2), slice(bn // 2, bn)] if bn % (2 * MXU_COLUMNS) == 0 else [slice(0, bn)]
    dots = [(jnp.dot(hb, wg_ref[:, c], preferred_element_type=F32),
             jnp.dot(pb, wu_ref[:, c], preferred_element_type=F32)) for c in halves]
    ssq = None
    for c, (pre, up) in zip(halves, dots):
        out = h_ref[:, c] + up * jax.nn.sigmoid(pre * rstd)
        o_ref[:, c] = out
        if rest:
            rest[0][:, c] = out.astype(rest[0].dtype)
            part = jnp.sum(out * out, axis=1, keepdims=True)
            ssq = part if ssq is None else ssq + part
    if rest:
        _emit_rstd(ssq, pl.program_id(1), nn - 1, width, *rest[1:])


def ple(hb, rstd, w_gate, p, w_up, h, *, emit_norm, bm=1024, bn=512):
    m, d = hb.shape
    pd = p.shape[1]
    n = w_gate.shape[1]
    bm, bn = _blk(m, bm), _blk(n, bn)
    nn = n // bn
    tile = pl.BlockSpec((bm, bn), lambda i, j: (i, j))
    rows = pl.BlockSpec((bm, LANES), lambda i, j: (i, 0))
    vmem = (2 * (bm * d * 2 + d * bn * 2 + bm * pd * 4 + pd * bn * 2 + 2 * bm * bn * 4 + bm * LANES * 4)
            + 3 * bm * bn * 4)
    out_specs, out_shape, scratch = [tile], [jax.ShapeDtypeStruct((m, n), F32)], []
    if emit_norm:
        out_specs += [tile, rows]
        out_shape += [jax.ShapeDtypeStruct((m, n), BF16), jax.ShapeDtypeStruct((m, LANES), F32)]
        scratch.append(pltpu.VMEM((bm, 1), F32))
        vmem += 2 * bm * bn * 2 + 3 * bm * LANES * 4
    outs = pl.pallas_call(
        functools.partial(_ple_kernel, nn=nn, width=n),
        grid=(m // bm, nn),
        in_specs=[pl.BlockSpec((bm, d), lambda i, j: (i, 0)),
                  pl.BlockSpec((d, bn), lambda i, j: (0, j)),
                  pl.BlockSpec((bm, pd), lambda i, j: (i, 0)),
                  pl.BlockSpec((pd, bn), lambda i, j: (0, j)),
                  tile, rows],
        out_specs=out_specs,
        out_shape=out_shape,
        scratch_shapes=scratch,
        compiler_params=_params(("parallel", "arbitrary" if emit_norm else "parallel"), vmem),
        name="ple",
    )(hb, w_gate, p, w_up, h, rstd)
    return outs if emit_norm else outs[0]


def _attn_kernel(q_ref, k_ref, v_ref, tri_ref, bias_ref, o_ref, acc_ref, gone_ref, *, tb, rs, group, seq):
    gone_limit = -EXP_ZERO_BELOW * LOG2E

    def scores(q, k_start, nk, bias):
        z = lax.dot_general(q, k_ref[pl.ds(k_start, nk), :], (((1,), (1,)), ((), ())),
                            preferred_element_type=F32)
        return z if bias is None else z + bias

    def split(z):
        sp = jnp.maximum(z, 0.0) + jnp.log(1.0 + jnp.exp2(-jnp.abs(z))) * LOG2E
        hi = sp.astype(BF16)
        return sp, hi, (sp - hi.astype(F32)).astype(BF16)

    def suffix_sum(hi, lo, nk):
        tri = tri_ref[0:nk, 0:nk]
        return jnp.dot(hi, tri, preferred_element_type=F32) + jnp.dot(lo, tri, preferred_element_type=F32)

    def weights(z, sp, suffix, gone):
        log2_w = z - sp - suffix
        walked = suffix[:, 0:1] + sp[:, 0:1]
        if gone is not None:
            log2_w, walked = log2_w - gone, walked + gone
        return jnp.exp2(log2_w).astype(BF16), walked

    def values(w, k_start, nk):
        return jnp.dot(w, v_ref[pl.ds(k_start, nk), :], preferred_element_type=F32)

    def tile(q, k_start, nk, gone, bias):
        z = scores(q, k_start, nk, bias)
        sp, hi, lo = split(z)
        w, gone = weights(z, sp, suffix_sum(hi, lo, nk), gone)
        return values(w, k_start, nk), gone

    def query_blocks(qis, first):
        q_starts = [pl.multiple_of(qi * tb, tb) for qi in qis]
        ns = tb // rs
        units = [(b, s, True) for b in range(len(qis)) for s in range(ns)]
        if not first:
            units += [(b, s, False) for b in range(len(qis)) for s in range(ns)]
        live = [None] * len(units)
        gone_of, pv_of = {}, {}

        def run(stage, u):
            b, s, diagonal = units[u]
            rows = slice(s * rs, (s + 1) * rs)
            nk = (s + 1) * rs if diagonal else tb
            k_start = q_starts[b] if diagonal else pl.multiple_of(q_starts[b] - tb, tb)
            if stage == 0:
                q = q_ref[pl.ds(q_starts[b] + s * rs, rs), :]
                live[u] = (scores(q, k_start, nk, bias_ref[rows, 0:nk] if diagonal else None),)
            elif stage == 1:
                (z,) = live[u]
                live[u] = (z,) + split(z)
            elif stage == 2:
                z, sp, hi, lo = live[u]
                live[u] = (z, sp, suffix_sum(hi, lo, nk))
            elif stage == 3:
                z, sp, suffix = live[u]
                w, gone_of[b, s] = weights(z, sp, suffix, gone_of.get((b, s)))
                live[u] = (w,)
            else:
                (w,) = live[u]
                live[u] = None
                pv = values(w, k_start, nk)
                if first:
                    o_ref[pl.ds(q_starts[b] + s * rs, rs), :] = pv.astype(o_ref.dtype)
                elif diagonal:
                    pv_of[b, s] = pv
                else:
                    acc_ref[b, rows, :] = pv_of.pop((b, s)) + pv
                    gone_ref[b, rows, :] = gone_of[b, s]

        n_stages = 5
        for step in range(len(units) + n_stages - 1):
            for stage in range(n_stages):
                if 0 <= step - stage < len(units):
                    run(stage, step - stage)
        if first:
            return

        for b, qi in enumerate(qis):
            def cond(carry):
                kb, alive = carry
                return jnp.logical_and(kb >= 0, alive > 0)

            def body(carry, b=b):
                kb, _ = carry
                pv_far, gone_far = tile(q_ref[pl.ds(q_starts[b], tb), :], pl.multiple_of(kb * tb, tb), tb,
                                        gone_ref[b], None)
                acc_ref[b] += pv_far
                gone_ref[b] = gone_far
                return kb - 1, (jnp.min(gone_far) < gone_limit).astype(jnp.int32)

            lax.while_loop(cond, body, (qi - 2, (jnp.min(gone_ref[b]) < gone_limit).astype(jnp.int32)))
            o_ref[pl.ds(q_starts[b], tb), :] = acc_ref[b].astype(o_ref.dtype)

    nq = seq // tb
    query_blocks([0], True)
    n_iter = (nq - 1) // group
    lead = nq - 1 - n_iter * group
    if lead:
        query_blocks(list(range(1, 1 + lead)), False)

    def step(it, carry):
        query_blocks([1 + lead + it * group + g for g in range(group)], False)
        return carry

    lax.fori_loop(0, n_iter, step, 0)


def stick_breaking_attention(q, kv, batch, seq, *, tb=256, rs=128, group=4):
    t, hd = q.shape
    n_heads = hd // HEAD_DIM
    tb = _blk(seq, tb)
    rows = lax.broadcasted_iota(jnp.int32, (tb, tb), 0)
    cols = lax.broadcasted_iota(jnp.int32, (tb, tb), 1)
    tri = (rows > cols).astype(BF16)
    bias = jnp.where(cols < rows, 0.0, MASKED_SCORE).astype(F32)
    vmem = 2 * (4 * seq * HEAD_DIM * 2 + tb * tb * 6) + 16 * tb * tb * 4
    head = lambda b, h: (b, h)
    return pl.pallas_call(
        functools.partial(_attn_kernel, tb=tb, rs=min(rs, tb), group=group, seq=seq),
        grid=(batch, n_heads),
        in_specs=[pl.BlockSpec((seq, HEAD_DIM), head),
                  pl.BlockSpec((seq, HEAD_DIM), head),
                  pl.BlockSpec((seq, HEAD_DIM), lambda b, h: (b, n_heads + h)),
                  pl.BlockSpec((tb, tb), lambda b, h: (0, 0)),
                  pl.BlockSpec((tb, tb), lambda b, h: (0, 0))],
        out_specs=pl.BlockSpec((seq, HEAD_DIM), head),
        out_shape=jax.ShapeDtypeStruct((t, hd), BF16),
        scratch_shapes=[pltpu.VMEM((group, tb, HEAD_DIM), F32), pltpu.VMEM((group, tb, 1), F32)],
        compiler_params=_params(("parallel", "parallel"), vmem),
        name="stick_breaking_attention",
    )(q, kv, kv, tri, bias)


INFO_IDX1, INFO_IDX2, INFO_RANK1, INFO_RANK2, INFO_W1, INFO_W2 = range(6)


def _pack_bf16_pairs(x):
    half = x.shape[1] // 2
    bits = lambda v: lax.bitcast_convert_type(v.astype(BF16).astype(F32), jnp.uint32)
    return (bits(x[:, :half]) >> 16) | (bits(x[:, half:]) & jnp.uint32(0xFFFF0000))


def _unpack_bf16_pairs(words):
    low = lax.bitcast_convert_type(words << 16, F32).astype(BF16)
    high = lax.bitcast_convert_type(words & jnp.uint32(0xFFFF0000), F32).astype(BF16)
    return low, high


def _router_kernel(h_ref, g_ref, wr_ref, br_ref, tri_ref, hn_ref, info_ref, cnt_ref, count_ref):
    @pl.when(pl.program_id(0) == 0)
    def _():
        count_ref[...] = jnp.zeros_like(count_ref)

    x = h_ref[...]
    hn = x * lax.rsqrt(jnp.mean(x * x, axis=-1, keepdims=True) + EPS) * g_ref[...]
    hn_ref[...] = _pack_bf16_pairs(hn)
    logits = jnp.dot(hn, wr_ref[...], precision=lax.Precision.HIGHEST,
                     preferred_element_type=F32) + br_ref[...]
    lane = lax.broadcasted_iota(jnp.int32, logits.shape, 1)
    m1 = jnp.max(logits, axis=1, keepdims=True)
    i1 = jnp.min(jnp.where(logits == m1, lane, LANES), axis=1, keepdims=True)
    rest = jnp.where(lane == i1, -jnp.inf, logits)
    m2 = jnp.max(rest, axis=1, keepdims=True)
    i2 = jnp.min(jnp.where(rest == m2, lane, LANES), axis=1, keepdims=True)
    e2 = jnp.exp(m2 - m1)
    w1 = 1.0 / (1.0 + e2)
    w2 = e2 / (1.0 + e2)
    sel1 = lane == i1
    sel2 = lane == i2
    sel = jnp.logical_or(sel1, sel2).astype(F32)
    before = jnp.dot(tri_ref[...], sel.astype(BF16), preferred_element_type=F32) + count_ref[0:1, :]
    rank1 = jnp.sum(jnp.where(sel1, before, 0.0), axis=1, keepdims=True)
    rank2 = jnp.sum(jnp.where(sel2, before, 0.0), axis=1, keepdims=True)
    count_ref[...] = count_ref[...] + jnp.sum(sel, axis=0, keepdims=True)
    cnt_ref[...] = count_ref[...]
    info = jnp.zeros(logits.shape, F32)
    for slot, val in ((INFO_IDX1, i1.astype(F32)), (INFO_IDX2, i2.astype(F32)), (INFO_RANK1, rank1),
                      (INFO_RANK2, rank2), (INFO_W1, w1), (INFO_W2, w2)):
        info = jnp.where(lane == slot, val, info)
    info_ref[...] = info


def route(h, gain, w_router, b_router, *, bt=512):
    t, d = h.shape
    n_exp = w_router.shape[1]
    bt = _blk(t, bt)
    wr = jnp.zeros((d, LANES), F32).at[:, :n_exp].set(w_router)
    br = jnp.full((1, LANES), -1e30, F32).at[0, :n_exp].set(b_router)
    tri = (lax.broadcasted_iota(jnp.int32, (bt, bt), 1)
           < lax.broadcasted_iota(jnp.int32, (bt, bt), 0)).astype(BF16)
    vmem = 2 * (2 * bt * d * 4 + d * LANES * 4 + bt * bt * 2 + bt * LANES * 4) + 4 * bt * d * 4
    hn, info, counts = pl.pallas_call(
        _router_kernel,
        grid=(t // bt,),
        in_specs=[pl.BlockSpec((bt, d), lambda i: (i, 0)),
                  pl.BlockSpec((1, d), lambda i: (0, 0)),
                  pl.BlockSpec((d, LANES), lambda i: (0, 0)),
                  pl.BlockSpec((1, LANES), lambda i: (0, 0)),
                  pl.BlockSpec((bt, bt), lambda i: (0, 0))],
        out_specs=[pl.BlockSpec((bt, d // 2), lambda i: (i, 0)),
                   pl.BlockSpec((bt, LANES), lambda i: (i, 0)),
                   pl.BlockSpec((8, LANES), lambda i: (0, 0))],
        out_shape=[jax.ShapeDtypeStruct((t, d // 2), jnp.uint32),
                   jax.ShapeDtypeStruct((t, LANES), F32),
                   jax.ShapeDtypeStruct((8, LANES), F32)],
        scratch_shapes=[pltpu.VMEM((8, LANES), F32)],
        compiler_params=_params(("arbitrary",), vmem),
        name="moe_router",
    )(h, gain, wr, br, tri)
    return hn, info, counts[0, :n_exp]


def _combine_kernel(pos_ref, pos_next_ref, y_ref, w_ref, h_ref, o_ref, hb_ref, rstd_ref, buf_ref, sems, *, tc):
    i = pl.program_id(0)
    slot = i % 2

    def copy(pos, slot, j, k):
        return pltpu.make_async_copy(y_ref.at[pl.ds(pos[k, j], 1)], buf_ref.at[slot, k, pl.ds(j, 1)],
                                     sems.at[slot])

    def issue(pos, slot):
        def token(j, _):
            copy(pos, slot, j, 0).start(priority=0)
            copy(pos, slot, j, 1).start(priority=1)
            return 0
        lax.fori_loop(0, tc, token, 0)

    @pl.when(i == 0)
    def _():
        issue(pos_ref, 0)

    @pl.when(i + 1 < pl.num_programs(0))
    def _():
        issue(pos_next_ref, 1 - slot)

    def wait(j, _):
        copy(pos_ref, slot, j, 0).wait()
        copy(pos_ref, slot, j, 1).wait()
        return 0

    lax.fori_loop(0, tc, wait, 0, unroll=4)
    w = w_ref[...]
    out = (h_ref[...] + w[:, INFO_W1:INFO_W1 + 1] * buf_ref[slot, 0]
           + w[:, INFO_W2:INFO_W2 + 1] * buf_ref[slot, 1])
    o_ref[...] = out
    hb_ref[...] = out.astype(hb_ref.dtype)
    rstd_ref[...] = jnp.broadcast_to(lax.rsqrt(jnp.mean(out * out, axis=1, keepdims=True) + EPS),
                                     rstd_ref.shape)


def combine(y_sorted, pos, info, h, *, tc=256):
    t, d = h.shape
    tc = _blk(t, tc)
    last = t // tc - 1
    vmem = 4 * tc * d * 4 + 2 * (2 * tc * d * 4 + 2 * tc * LANES * 4 + tc * d * 2) + 2 * tc * d * 4
    return pl.pallas_call(
        functools.partial(_combine_kernel, tc=tc),
        grid=(t // tc,),
        in_specs=[pl.BlockSpec((TOP_K, tc), lambda i: (0, i), memory_space=pltpu.SMEM),
                  pl.BlockSpec((TOP_K, tc), lambda i: (0, jnp.minimum(i + 1, last)), memory_space=pltpu.SMEM),
                  pl.BlockSpec(memory_space=pl.ANY),
                  pl.BlockSpec((tc, LANES), lambda i: (i, 0)),
                  pl.BlockSpec((tc, d), lambda i: (i, 0))],
        out_specs=[pl.BlockSpec((tc, d), lambda i: (i, 0)),
                   pl.BlockSpec((tc, d), lambda i: (i, 0)),
                   pl.BlockSpec((tc, LANES), lambda i: (i, 0))],
        out_shape=[jax.ShapeDtypeStruct((t, d), F32),
                   jax.ShapeDtypeStruct((t, d), BF16),
                   jax.ShapeDtypeStruct((t, LANES), F32)],
        scratch_shapes=[pltpu.VMEM((2, TOP_K, tc, d), F32), pltpu.SemaphoreType.DMA((2,))],
        compiler_params=_params(("arbitrary",), vmem),
        name="moe_combine",
    )(pos, pos, y_sorted, info, h)


GATHER_SLOTS = 3
ROW_PARTS = 4


def _expert_glu_kernel(be_ref, nvalid_ref, rows_ref, tok_ref, src_ref, wg_ref, wu_ref, o_ref, buf_ref, x_ref, sems, *,
                       bm):
    del be_ref
    i, j, nf = pl.program_id(0), pl.program_id(1), pl.num_programs(1)
    n_valid = nvalid_ref[0]
    share = bm // nf

    def copy(block, r):
        slot = block % GATHER_SLOTS
        return pltpu.make_async_copy(src_ref.at[pl.ds(tok_ref[block * bm + r], 1)],
                                     buf_ref.at[slot, pl.ds(r, 1)], sems.at[slot])

    @pl.when(jnp.logical_and(i == 0, j == 0))
    def _():
        def row(r, _):
            copy(0, r).start(priority=0)
            copy(1, r).start(priority=1)
            return 0
        lax.fori_loop(0, bm, row, 0)

    @pl.when(jnp.logical_and(j == 0, i < n_valid + GATHER_SLOTS - 1))
    def _():
        def row(r, _):
            copy(i, r).wait()
            return 0
        lax.fori_loop(0, bm, row, 0, unroll=8)
        half = buf_ref.shape[2]
        x_ref[:, :half], x_ref[:, half:] = _unpack_bf16_pairs(buf_ref[i % GATHER_SLOTS])

    part = bm // ROW_PARTS
    parts = (rows_ref[i] + part - 1) // part
    for q in range(1, ROW_PARTS + 1):
        @pl.when(parts == q)
        def _(n=q * part):
            for r in range(share):
                copy(i + GATHER_SLOTS - 1, j * share + r).start(priority=r % 2)
            x = x_ref[0:n, :]
            g = jnp.dot(x, wg_ref[...], preferred_element_type=F32)
            u = jnp.dot(x, wu_ref[...], preferred_element_type=F32)
            o_ref[0:n, :] = (g * jax.nn.sigmoid(g) * u).astype(o_ref.dtype)
            if n < bm:
                o_ref[n:, :] = jnp.zeros((bm - n, o_ref.shape[1]), o_ref.dtype)

    @pl.when(parts == 0)
    def _():
        o_ref[...] = jnp.zeros_like(o_ref)


def expert_glu(src, tok_of_row, w_gu, block_expert, n_valid, block_rows, *, bm, bn=512):
    r = tok_of_row.shape[0]
    d = src.shape[1] * 2
    f = w_gu.shape[2] // 2
    bn = _blk(f, bn)
    nf = f // bn
    assert bm % nf == 0
    vmem = (GATHER_SLOTS * bm * d * 2 + bm * d * 2 + 2 * (2 * d * bn * 2 + bm * bn * 2) + 3 * bm * bn * 4
            + bm * d * 4)
    col = lambda i, j, nv: jnp.where(i < nv[0], j, 0)
    return pl.pallas_call(
        functools.partial(_expert_glu_kernel, bm=bm),
        grid_spec=pltpu.PrefetchScalarGridSpec(
            num_scalar_prefetch=4,
            grid=(r // bm, nf),
            in_specs=[pl.BlockSpec(memory_space=pl.ANY),
                      pl.BlockSpec((None, d, bn), lambda i, j, be, nv, rows, tok: (be[i], 0, col(i, j, nv))),
                      pl.BlockSpec((None, d, bn), lambda i, j, be, nv, rows, tok: (be[i], 0, col(i, j, nv) + nf))],
            out_specs=pl.BlockSpec((bm, bn), lambda i, j, be, nv, rows, tok: (i, j)),
            scratch_shapes=[pltpu.VMEM((GATHER_SLOTS, bm, d // 2), jnp.uint32), pltpu.VMEM((bm, d), BF16),
                            pltpu.SemaphoreType.DMA((GATHER_SLOTS,))]),
        out_shape=jax.ShapeDtypeStruct((r, f), BF16),
        compiler_params=_params(("arbitrary", "arbitrary"), vmem),
        name="moe_expert_glu",
    )(block_expert, n_valid, block_rows, tok_of_row, src, w_gu, w_gu)


def _expert_down_kernel(be_ref, nvalid_ref, rows_ref, x_ref, w_ref, o_ref):
    del be_ref, nvalid_ref
    bm = o_ref.shape[0]
    part = bm // ROW_PARTS
    parts = (rows_ref[pl.program_id(0)] + part - 1) // part
    for q in range(1, ROW_PARTS + 1):
        @pl.when(parts == q)
        def _(n=q * part):
            o_ref[0:n, :] = jnp.dot(x_ref[0:n, :], w_ref[...], preferred_element_type=F32)
            if n < bm:
                o_ref[n:, :] = jnp.zeros((bm - n, o_ref.shape[1]), o_ref.dtype)

    @pl.when(parts == 0)
    def _():
        o_ref[...] = jnp.zeros_like(o_ref)


def expert_down(a_sorted, w_down, block_expert, n_valid, block_rows, *, bm, bn=1024):
    r, f = a_sorted.shape
    d = w_down.shape[2]
    bn = _blk(d, bn)
    vmem = 2 * (bm * f * 2 + f * bn * 2 + bm * bn * 4) + bm * bn * 4
    return pl.pallas_call(
        _expert_down_kernel,
        grid_spec=pltpu.PrefetchScalarGridSpec(
            num_scalar_prefetch=3,
            grid=(r // bm, d // bn),
            in_specs=[pl.BlockSpec((bm, f), lambda i, j, be, nv, rows: (i, 0)),
                      pl.BlockSpec((None, f, bn),
                                   lambda i, j, be, nv, rows: (be[i], 0, jnp.where(i < nv[0], j, 0)))],
            out_specs=pl.BlockSpec((bm, bn), lambda i, j, be, nv, rows: (i, j))),
        out_shape=jax.ShapeDtypeStruct((r, d), F32),
        compiler_params=_params(("arbitrary", "arbitrary"), vmem),
        name="moe_expert_down",
    )(block_expert, n_valid, block_rows, a_sorted, w_down)


def moe(h, gain, w_router, b_router, w_gu, w_down, *, bm=512):
    t, d = h.shape
    n_exp = w_router.shape[1]
    bm = _blk(t, bm)
    hn, info, counts = route(h, gain, w_router, b_router)
    counts = counts.astype(jnp.int32)
    blocks_per_expert = (counts + bm - 1) // bm
    block_end = jnp.cumsum(blocks_per_expert)
    row_offset = (block_end - blocks_per_expert) * bm
    n_blocks = TOP_K * t // bm + n_exp + GATHER_SLOTS - 1
    n_valid = block_end[-1:]
    all_blocks = jnp.arange(n_blocks, dtype=jnp.int32)
    block_ids = jnp.minimum(all_blocks, n_valid[0] - 1)
    block_expert = jnp.sum((block_ids[:, None] >= block_end[None, :]).astype(jnp.int32), axis=1)
    rows_before = (block_ids - (block_end - blocks_per_expert)[block_expert]) * bm
    block_rows = jnp.where(all_blocks < n_valid[0], jnp.clip(counts[block_expert] - rows_before, 0, bm), 0)
    idx = info[:, INFO_IDX1:INFO_IDX2 + 1].astype(jnp.int32)
    rank = info[:, INFO_RANK1:INFO_RANK2 + 1].astype(jnp.int32)
    pos = (row_offset[idx] + rank).T
    token_ids = jnp.broadcast_to(jnp.arange(t, dtype=jnp.int32), (TOP_K, t))
    tok_of_row = jnp.zeros((n_blocks * bm,), jnp.int32).at[pos.reshape(-1)].set(
        token_ids.reshape(-1), unique_indices=True)
    a_sorted = expert_glu(hn, tok_of_row, w_gu, block_expert, n_valid, block_rows, bm=bm)
    y_sorted = expert_down(a_sorted, w_down, block_expert, n_valid, block_rows, bm=bm)
    return combine(y_sorted, pos, info, h)


def kernel(x, p, a_norm, a_w_in, a_conv_w, a_w_out, kv_norm, w_kv, b_norm, b_w_q, b_w_o, ffn_norm, dense_w_gu, dense_w_down, moe_w_router, moe_b_router, moe_w_gu, moe_w_down, ple_norm, ple_w_up, ple_w_gate, final_norm):
    batch, seq, d = x.shape
    depth = p.shape[0]
    n_a = a_norm.shape[0]
    t = batch * seq
    p = p.reshape(depth, t, p.shape[-1])
    bf = lambda w: w.astype(BF16)

    h = x.reshape(t, d)
    hb, rstd = stream_operands(h)
    kv = None

    to_convert = [(name, m, w[m].reshape(-1, w.shape[-1]))
                  for m in range(moe_w_gu.shape[0]) for name, w in (("down", moe_w_down), ("gu", moe_w_gu))]
    converted = {}

    def hosting(*args, **kwargs):
        if not to_convert:
            return normed_matmul(*args, **kwargs)
        name, m, w2d = to_convert[0]
        out, wb = normed_matmul(*args, cast=w2d, **kwargs)
        if wb is not None:
            converted[name, m] = wb
            to_convert.pop(0)
        return out

    def expert_weights(name, w, m):
        to_convert[:] = [job for job in to_convert if job[:2] != (name, m)]
        wb = converted.get((name, m))
        return bf(w[m]) if wb is None else wb.reshape(w.shape[1:])
    for i in range(depth):
        if i < n_a:
            bcx = hosting(hb, rstd, a_norm[i], a_w_in[i])
            mixed, w_out = short_conv(bcx, a_conv_w[i], seq), a_w_out[i]
        else:
            j = i - n_a
            if j == 0:
                kv = hosting(hb, rstd, kv_norm, w_kv)
            q = hosting(hb, rstd, b_norm[j], b_w_q[j], scale=HEAD_DIM ** -0.5 * LOG2E)
            mixed, w_out = stick_breaking_attention(q, kv, batch, seq), b_w_o[j]
        if i % 2 == 0:
            h, hb, rstd = resid_matmul(mixed, bf(w_out), h, emit_norm=True, ring=True)
            act = hosting(hb, rstd, ffn_norm[i], dense_w_gu[i // 2], glu=True, bn=256)
            h, hb, rstd = resid_matmul(act, bf(dense_w_down[i // 2]), h, emit_norm=True, bm=512)
        else:
            m = i // 2
            h = resid_matmul(mixed, bf(w_out), h, emit_norm=False, ring=True)
            h, hb, rstd = moe(h, ffn_norm[i:i + 1], moe_w_router[m], moe_b_router[m],
                              expert_weights("gu", moe_w_gu, m), expert_weights("down", moe_w_down, m))
        w_gate = bf(ple_w_gate[i] * ple_norm[i][:, None])
        if i + 1 < depth:
            h, hb, rstd = ple(hb, rstd, w_gate, p[i], bf(ple_w_up[i]), h, emit_norm=True)
        else:
            h = ple(hb, rstd, w_gate, p[i], bf(ple_w_up[i]), h, emit_norm=False)
    (out,) = rmsnorm(h, final_norm[None, :], F32)
    return out.reshape(batch, seq, d)
```

```python
import functools

import jax
import jax.numpy as jnp
from jax import lax
from jax.experimental import pallas as pl
from jax.experimental.pallas import tpu as pltpu

F32 = jnp.float32
BF16 = jnp.bfloat16

EPS = 1e-6
HEAD_DIM = 128
CONV_WIDTH = 3
TOP_K = 2
LANES = 128
BF16_SUBLANES = 16
MXU_COLUMNS = 256
VMEM_BUDGET_V7X = 60000 * 1024
VMEM_SLACK = 8 * 1024 * 1024
CAST_SLAB_BYTES = 4 * 1024 * 1024
LOG2E = 1.4426950408889634
EXP_ZERO_BELOW = -110.0
MASKED_SCORE = -1e30


def _params(semantics, block_bytes):
    limit = min(VMEM_BUDGET_V7X, int(block_bytes) + VMEM_SLACK)
    return pltpu.CompilerParams(dimension_semantics=semantics, vmem_limit_bytes=limit)


def _blk(dim, pref):
    b = min(dim, pref)
    while dim % b:
        b //= 2
    return b


def _rmsnorm_kernel(x_ref, g_ref, *o_refs):
    x = x_ref[...]
    y = x * lax.rsqrt(jnp.mean(x * x, axis=-1, keepdims=True) + EPS)
    for n, o_ref in enumerate(o_refs):
        o_ref[...] = (y * g_ref[n:n + 1, :]).astype(o_ref.dtype)


def rmsnorm(x, gains, out_dtype):
    t, d = x.shape
    g = gains.shape[0]
    bt = _blk(t, 256)
    out_bytes = jnp.dtype(out_dtype).itemsize
    outs = pl.pallas_call(
        _rmsnorm_kernel,
        grid=(t // bt,),
        in_specs=[pl.BlockSpec((bt, d), lambda i: (i, 0)),
                  pl.BlockSpec((g, d), lambda i: (0, 0))],
        out_specs=[pl.BlockSpec((bt, d), lambda i: (i, 0))] * g,
        out_shape=[jax.ShapeDtypeStruct((t, d), out_dtype)] * g,
        compiler_params=_params(("parallel",), 2 * bt * d * (4 + g * out_bytes)),
        name="rmsnorm",
    )(x, gains)
    return outs


def _stream_kernel(x_ref, xb_ref, rstd_ref):
    x = x_ref[...]
    xb_ref[...] = x.astype(xb_ref.dtype)
    rstd_ref[...] = jnp.broadcast_to(lax.rsqrt(jnp.mean(x * x, axis=-1, keepdims=True) + EPS), rstd_ref.shape)


def stream_operands(x):
    t, d = x.shape
    bt = _blk(t, 256)
    return pl.pallas_call(
        _stream_kernel,
        grid=(t // bt,),
        in_specs=[pl.BlockSpec((bt, d), lambda i: (i, 0))],
        out_specs=[pl.BlockSpec((bt, d), lambda i: (i, 0)), pl.BlockSpec((bt, LANES), lambda i: (i, 0))],
        out_shape=[jax.ShapeDtypeStruct((t, d), BF16), jax.ShapeDtypeStruct((t, LANES), F32)],
        compiler_params=_params(("parallel",), 2 * bt * (d * 6 + LANES * 4) + bt * d * 4),
        name="stream_operands",
    )(x)


def _emit_stream(h, j, last_j, width, hb_ref, rstd_ref, ssq_ref):
    hb_ref[...] = h.astype(hb_ref.dtype)
    _emit_rstd(jnp.sum(h * h, axis=1, keepdims=True), j, last_j, width, rstd_ref, ssq_ref)


def _emit_rstd(part, j, last_j, width, rstd_ref, ssq_ref):
    @pl.when(j == 0)
    def _():
        ssq_ref[...] = part

    @pl.when(j > 0)
    def _():
        ssq_ref[...] += part

    @pl.when(j == last_j)
    def _():
        rstd_ref[...] = jnp.broadcast_to(lax.rsqrt(ssq_ref[...] * (1.0 / width) + EPS), rstd_ref.shape)


WEIGHT_RING = 3


def _resid_mm_kernel(x_ref, w_ref, r_ref, o_ref, *rest, nn, width, n_steps):
    if n_steps:
        *rest, wbuf_ref, sems = rest
        s = pl.program_id(0) * nn + pl.program_id(1)
        bn = o_ref.shape[1]

        def fetch(step):
            col = pl.multiple_of((step % nn) * bn, bn)
            slot = step % WEIGHT_RING
            return pltpu.make_async_copy(w_ref.at[:, pl.ds(col, bn)], wbuf_ref.at[slot], sems.at[slot])

        @pl.when(s == 0)
        def _():
            for first in range(WEIGHT_RING - 1):
                fetch(jnp.int32(first)).start()

        @pl.when(s + WEIGHT_RING - 1 < n_steps)
        def _():
            fetch(s + WEIGHT_RING - 1).start()

        fetch(s).wait()
        w = wbuf_ref[s % WEIGHT_RING]
    else:
        w = w_ref[...]
    out = r_ref[...] + jnp.dot(x_ref[...], w, preferred_element_type=F32)
    o_ref[...] = out
    if rest:
        _emit_stream(out, pl.program_id(1), nn - 1, width, *rest)


def resid_matmul(x, w, resid, *, emit_norm, ring=False, bm=1024, bn=512):
    m, kdim = x.shape
    n = w.shape[1]
    bm, bn = _blk(m, bm), _blk(n, bn)
    nn = n // bn
    n_steps = (m // bm) * nn
    ring = ring and n_steps >= WEIGHT_RING
    tile = pl.BlockSpec((bm, bn), lambda i, j: (i, j))
    rows = pl.BlockSpec((bm, LANES), lambda i, j: (i, 0))
    vmem = (2 * bm * kdim + (WEIGHT_RING if ring else 2) * kdim * bn) * 2 + 4 * bm * bn * 4 + bm * bn * 4
    out_specs = [tile]
    out_shape = [jax.ShapeDtypeStruct((m, n), F32)]
    scratch = []
    if emit_norm:
        out_specs += [tile, rows]
        out_shape += [jax.ShapeDtypeStruct((m, n), BF16), jax.ShapeDtypeStruct((m, LANES), F32)]
        scratch.append(pltpu.VMEM((bm, 1), F32))
        vmem += 2 * bm * bn * 2 + 3 * bm * LANES * 4
    if ring:
        scratch += [pltpu.VMEM((WEIGHT_RING, kdim, bn), BF16), pltpu.SemaphoreType.DMA((WEIGHT_RING,))]
    weights = pl.BlockSpec(memory_space=pl.ANY) if ring else pl.BlockSpec((kdim, bn), lambda i, j: (0, j))
    outs = pl.pallas_call(
        functools.partial(_resid_mm_kernel, nn=nn, width=n, n_steps=n_steps if ring else 0),
        grid=(m // bm, nn),
        in_specs=[pl.BlockSpec((bm, kdim), lambda i, j: (i, 0)), weights, tile],
        out_specs=out_specs,
        out_shape=out_shape,
        scratch_shapes=scratch,
        compiler_params=_params(("arbitrary" if ring else "parallel",
                                 "arbitrary" if emit_norm or ring else "parallel"), vmem),
        name="resid_matmul",
    )(x, w, resid)
    return outs if emit_norm else outs[0]


def _normed_kernel(x_ref, rs_ref, g_ref, *refs, n_w, scale, has_cast):
    w_refs, refs = refs[:n_w], refs[n_w:]
    if has_cast:
        cast_ref, o_ref, cast_o_ref = refs[:3]
        cast_o_ref[...] = cast_ref[...].astype(cast_o_ref.dtype)
        wb_refs = refs[3:]
    else:
        o_ref, wb_refs = refs[0], refs[1:]

    @pl.when(pl.program_id(1) == 0)
    def _():
        gain = g_ref[...]
        for w_ref, wb_ref in zip(w_refs, wb_refs):
            wb_ref[...] = (w_ref[...] * gain).astype(BF16)

    x = x_ref[...]
    rstd = rs_ref[:, 0:1]
    outs = [jnp.dot(x, wb_ref[...], preferred_element_type=F32) * rstd for wb_ref in wb_refs]
    if n_w == 2:
        g, u = outs
        out = g * jax.nn.sigmoid(g) * u
    else:
        out = outs[0] if scale is None else outs[0] * scale
    o_ref[...] = out.astype(o_ref.dtype)


def normed_matmul(x, rstd, gain, w, *, glu=False, scale=None, cast=None, bm=1024, bn=512):
    m, kdim = x.shape
    n_w = 2 if glu else 1
    n = w.shape[1] // n_w
    bm, bn = _blk(m, bm), _blk(n, bn)
    nn, nm = n // bn, m // bm
    w_specs = [pl.BlockSpec((kdim, bn), lambda j, i, off=off: (0, j + off)) for off in range(0, n_w * nn, nn)]
    vmem = (2 * bm * kdim * 2 + n_w * kdim * bn * (2 * 4 + 2) + 2 * bm * bn * 2 + 2 * kdim * LANES * 4
            + 2 * bm * LANES * 4 + (n_w + 1) * bm * bn * 4)
    in_specs = [pl.BlockSpec((bm, kdim), lambda j, i: (i, 0)),
                pl.BlockSpec((bm, LANES), lambda j, i: (i, 0)),
                pl.BlockSpec((kdim, 1), lambda j, i: (0, 0))] + w_specs
    args = [x, rstd, gain.reshape(kdim, 1)] + [w] * n_w
    out_specs = [pl.BlockSpec((bm, bn), lambda j, i: (i, j))]
    out_shape = [jax.ShapeDtypeStruct((m, n), BF16)]
    slab = _cast_slab(cast, nn * nm) if cast is not None else None
    if slab is not None:
        rows, cols = cast.shape
        last = rows // slab - 1
        side = pl.BlockSpec((slab, cols), lambda j, i: (jnp.minimum(j * nm + i, last), 0))
        in_specs.append(side)
        args.append(cast)
        out_specs.append(side)
        out_shape.append(jax.ShapeDtypeStruct((rows, cols), BF16))
        vmem += 2 * slab * cols * (4 + 2)
    outs = pl.pallas_call(
        functools.partial(_normed_kernel, n_w=n_w, scale=scale, has_cast=slab is not None),
        grid=(nn, nm),
        in_specs=in_specs,
        out_specs=out_specs,
        out_shape=out_shape,
        scratch_shapes=[pltpu.VMEM((kdim, bn), BF16)] * n_w,
        compiler_params=_params(("arbitrary", "arbitrary"), vmem),
        name="normed_glu" if glu else "normed_matmul",
    )(*args)
    return (outs[0], outs[1] if slab is not None else None) if cast is not None else outs[0]


def _cast_slab(a, steps):
    rows = a.shape[0]
    slab = BF16_SUBLANES
    while slab <= rows and (rows % slab or rows // slab > steps):
        slab *= 2
    return slab if slab <= rows and slab * a.shape[1] * 4 <= CAST_SLAB_BYTES else None


def _conv_kernel(b_ref, c_ref, u_ref, ch_ref, uh_ref, w_ref, o_ref, *, blocks_per_seq):
    ts = c_ref.shape[0]
    cu = c_ref[...].astype(F32) * u_ref[...].astype(F32)
    seq_start = (pl.program_id(0) % blocks_per_seq) == 0
    halo = ch_ref[...].astype(F32) * uh_ref[...].astype(F32)
    halo = jnp.where(seq_start, 0.0, halo)
    row = lax.broadcasted_iota(jnp.int32, cu.shape, 0)
    h1 = halo[BF16_SUBLANES - 1:BF16_SUBLANES, :]
    h2 = halo[BF16_SUBLANES - 2:BF16_SUBLANES - 1, :]
    prev1 = jnp.where(row == 0, h1, pltpu.roll(cu, 1, 0))
    prev2 = jnp.where(row == 0, h2, jnp.where(row == 1, h1, pltpu.roll(cu, 2, 0)))
    y = w_ref[0:1, :] * prev2 + w_ref[1:2, :] * prev1 + w_ref[2:3, :] * cu
    o_ref[...] = (b_ref[...].astype(F32) * y).astype(o_ref.dtype)


def short_conv(bcx, conv_w, seq):
    t, d3 = bcx.shape
    d = d3 // 3
    ts = _blk(seq, 1024)
    dblk = _blk(d, 1024)
    nd = d // dblk
    hb = ts // BF16_SUBLANES

    def halo_map(off):
        return lambda i, j: (jnp.maximum(i * hb - 1, 0), j + off)

    vmem = 2 * (4 * ts * dblk * 2 + 2 * BF16_SUBLANES * dblk * 2) + 6 * ts * dblk * 4
    return pl.pallas_call(
        functools.partial(_conv_kernel, blocks_per_seq=seq // ts),
        grid=(t // ts, nd),
        in_specs=[pl.BlockSpec((ts, dblk), lambda i, j: (i, j)),
                  pl.BlockSpec((ts, dblk), lambda i, j: (i, j + nd)),
                  pl.BlockSpec((ts, dblk), lambda i, j: (i, j + 2 * nd)),
                  pl.BlockSpec((BF16_SUBLANES, dblk), halo_map(nd)),
                  pl.BlockSpec((BF16_SUBLANES, dblk), halo_map(2 * nd)),
                  pl.BlockSpec((CONV_WIDTH, dblk), lambda i, j: (0, j))],
        out_specs=pl.BlockSpec((ts, dblk), lambda i, j: (i, j)),
        out_shape=jax.ShapeDtypeStruct((t, d), BF16),
        compiler_params=_params(("parallel", "parallel"), vmem),
        name="short_conv",
    )(bcx, bcx, bcx, bcx, bcx, conv_w)


def _ple_kernel(hb_ref, wg_ref, p_ref, wu_ref, h_ref, rs_ref, o_ref, *rest, nn, width):
    hb, pb, rstd = hb_ref[...], p_ref[...].astype(BF16), rs_ref[:, 0:1]
    bn = o_ref.shape[1]
    halves = [slice(0, bn // 2), slice(bn // 2, bn)] if bn % (2 * MXU_COLUMNS) == 0 else [slice(0, bn)]
    dots = [(jnp.dot(hb, wg_ref[:, c], preferred_element_type=F32),
             jnp.dot(pb, wu_ref[:, c], preferred_element_type=F32)) for c in halves]
    ssq = None
    for c, (pre, up) in zip(halves, dots):
        out = h_ref[:, c] + up * jax.nn.sigmoid(pre * rstd)
        o_ref[:, c] = out
        if rest:
            rest[0][:, c] = out.astype(rest[0].dtype)
            part = jnp.sum(out * out, axis=1, keepdims=True)
            ssq = part if ssq is None else ssq + part
    if rest:
        _emit_rstd(ssq, pl.program_id(1), nn - 1, width, *rest[1:])


def ple(hb, rstd, w_gate, p, w_up, h, *, emit_norm, bm=1024, bn=512):
    m, d = hb.shape
    pd = p.shape[1]
    n = w_gate.shape[1]
    bm, bn = _blk(m, bm), _blk(n, bn)
    nn = n // bn
    tile = pl.BlockSpec((bm, bn), lambda i, j: (i, j))
    rows = pl.BlockSpec((bm, LANES), lambda i, j: (i, 0))
    vmem = (2 * (bm * d * 2 + d * bn * 2 + bm * pd * 4 + pd * bn * 2 + 2 * bm * bn * 4 + bm * LANES * 4)
            + 3 * bm * bn * 4)
    out_specs, out_shape, scratch = [tile], [jax.ShapeDtypeStruct((m, n), F32)], []
    if emit_norm:
        out_specs += [tile, rows]
        out_shape += [jax.ShapeDtypeStruct((m, n), BF16), jax.ShapeDtypeStruct((m, LANES), F32)]
        scratch.append(pltpu.VMEM((bm, 1), F32))
        vmem += 2 * bm * bn * 2 + 3 * bm * LANES * 4
    outs = pl.pallas_call(
        functools.partial(_ple_kernel, nn=nn, width=n),
        grid=(m // bm, nn),
        in_specs=[pl.BlockSpec((bm, d), lambda i, j: (i, 0)),
                  pl.BlockSpec((d, bn), lambda i, j: (0, j)),
                  pl.BlockSpec((bm, pd), lambda i, j: (i, 0)),
                  pl.BlockSpec((pd, bn), lambda i, j: (0, j)),
                  tile, rows],
        out_specs=out_specs,
        out_shape=out_shape,
        scratch_shapes=scratch,
        compiler_params=_params(("parallel", "arbitrary" if emit_norm else "parallel"), vmem),
        name="ple",
    )(hb, w_gate, p, w_up, h, rstd)
    return outs if emit_norm else outs[0]


def _attn_kernel(q_ref, k_ref, v_ref, tri_ref, bias_ref, o_ref, acc_ref, gone_ref, *, tb, rs, group, seq):
    gone_limit = -EXP_ZERO_BELOW * LOG2E

    def scores(q, k_start, nk, bias):
        z = lax.dot_general(q, k_ref[pl.ds(k_start, nk), :], (((1,), (1,)), ((), ())),
                            preferred_element_type=F32)
        return z if bias is None else z + bias

    def split(z):
        sp = jnp.maximum(z, 0.0) + jnp.log(1.0 + jnp.exp2(-jnp.abs(z))) * LOG2E
        hi = sp.astype(BF16)
        return sp, hi, (sp - hi.astype(F32)).astype(BF16)

    def suffix_sum(hi, lo, nk):
        tri = tri_ref[0:nk, 0:nk]
        return jnp.dot(hi, tri, preferred_element_type=F32) + jnp.dot(lo, tri, preferred_element_type=F32)

    def weights(z, sp, suffix, gone):
        log2_w = z - sp - suffix
        walked = suffix[:, 0:1] + sp[:, 0:1]
        if gone is not None:
            log2_w, walked = log2_w - gone, walked + gone
        return jnp.exp2(log2_w).astype(BF16), walked

    def values(w, k_start, nk):
        return jnp.dot(w, v_ref[pl.ds(k_start, nk), :], preferred_element_type=F32)

    def tile(q, k_start, nk, gone, bias):
        z = scores(q, k_start, nk, bias)
        sp, hi, lo = split(z)
        w, gone = weights(z, sp, suffix_sum(hi, lo, nk), gone)
        return values(w, k_start, nk), gone

    def query_blocks(qis, first):
        q_starts = [pl.multiple_of(qi * tb, tb) for qi in qis]
        ns = tb // rs
        units = [(b, s, True) for b in range(len(qis)) for s in range(ns)]
        if not first:
            units += [(b, s, False) for b in range(len(qis)) for s in range(ns)]
        live = [None] * len(units)
        gone_of, pv_of = {}, {}

        def run(stage, u):
            b, s, diagonal = units[u]
            rows = slice(s * rs, (s + 1) * rs)
            nk = (s + 1) * rs if diagonal else tb
            k_start = q_starts[b] if diagonal else pl.multiple_of(q_starts[b] - tb, tb)
            if stage == 0:
                q = q_ref[pl.ds(q_starts[b] + s * rs, rs), :]
                live[u] = (scores(q, k_start, nk, bias_ref[rows, 0:nk] if diagonal else None),)
            elif stage == 1:
                (z,) = live[u]
                live[u] = (z,) + split(z)
            elif stage == 2:
                z, sp, hi, lo = live[u]
                live[u] = (z, sp, suffix_sum(hi, lo, nk))
            elif stage == 3:
                z, sp, suffix = live[u]
                w, gone_of[b, s] = weights(z, sp, suffix, gone_of.get((b, s)))
                live[u] = (w,)
            else:
                (w,) = live[u]
                live[u] = None
                pv = values(w, k_start, nk)
                if first:
                    o_ref[pl.ds(q_starts[b] + s * rs, rs), :] = pv.astype(o_ref.dtype)
                elif diagonal:
                    pv_of[b, s] = pv
                else:
                    acc_ref[b, rows, :] = pv_of.pop((b, s)) + pv
                    gone_ref[b, rows, :] = gone_of[b, s]

        n_stages = 5
        for step in range(len(units) + n_stages - 1):
            for stage in range(n_stages):
                if 0 <= step - stage < len(units):
                    run(stage, step - stage)
        if first:
            return

        for b, qi in enumerate(qis):
            def cond(carry):
                kb, alive = carry
                return jnp.logical_and(kb >= 0, alive > 0)

            def body(carry, b=b):
                kb, _ = carry
                pv_far, gone_far = tile(q_ref[pl.ds(q_starts[b], tb), :], pl.multiple_of(kb * tb, tb), tb,
                                        gone_ref[b], None)
                acc_ref[b] += pv_far
                gone_ref[b] = gone_far
                return kb - 1, (jnp.min(gone_far) < gone_limit).astype(jnp.int32)

            lax.while_loop(cond, body, (qi - 2, (jnp.min(gone_ref[b]) < gone_limit).astype(jnp.int32)))
            o_ref[pl.ds(q_starts[b], tb), :] = acc_ref[b].astype(o_ref.dtype)

    nq = seq // tb
    query_blocks([0], True)
    n_iter = (nq - 1) // group
    lead = nq - 1 - n_iter * group
    if lead:
        query_blocks(list(range(1, 1 + lead)), False)

    def step(it, carry):
        query_blocks([1 + lead + it * group + g for g in range(group)], False)
        return carry

    lax.fori_loop(0, n_iter, step, 0)


def stick_breaking_attention(q, kv, batch, seq, *, tb=256, rs=128, group=4):
    t, hd = q.shape
    n_heads = hd // HEAD_DIM
    tb = _blk(seq, tb)
    rows = lax.broadcasted_iota(jnp.int32, (tb, tb), 0)
    cols = lax.broadcasted_iota(jnp.int32, (tb, tb), 1)
    tri = (rows > cols).astype(BF16)
    bias = jnp.where(cols < rows, 0.0, MASKED_SCORE).astype(F32)
    vmem = 2 * (4 * seq * HEAD_DIM * 2 + tb * tb * 6) + 16 * tb * tb * 4
    head = lambda b, h: (b, h)
    return pl.pallas_call(
        functools.partial(_attn_kernel, tb=tb, rs=min(rs, tb), group=group, seq=seq),
        grid=(batch, n_heads),
        in_specs=[pl.BlockSpec((seq, HEAD_DIM), head),
                  pl.BlockSpec((seq, HEAD_DIM), head),
                  pl.BlockSpec((seq, HEAD_DIM), lambda b, h: (b, n_heads + h)),
                  pl.BlockSpec((tb, tb), lambda b, h: (0, 0)),
                  pl.BlockSpec((tb, tb), lambda b, h: (0, 0))],
        out_specs=pl.BlockSpec((seq, HEAD_DIM), head),
        out_shape=jax.ShapeDtypeStruct((t, hd), BF16),
        scratch_shapes=[pltpu.VMEM((group, tb, HEAD_DIM), F32), pltpu.VMEM((group, tb, 1), F32)],
        compiler_params=_params(("parallel", "parallel"), vmem),
        name="stick_breaking_attention",
    )(q, kv, kv, tri, bias)


INFO_IDX1, INFO_IDX2, INFO_RANK1, INFO_RANK2, INFO_W1, INFO_W2 = range(6)


def _pack_bf16_pairs(x):
    half = x.shape[1] // 2
    bits = lambda v: lax.bitcast_convert_type(v.astype(BF16).astype(F32), jnp.uint32)
    return (bits(x[:, :half]) >> 16) | (bits(x[:, half:]) & jnp.uint32(0xFFFF0000))


def _unpack_bf16_pairs(words):
    low = lax.bitcast_convert_type(words << 16, F32).astype(BF16)
    high = lax.bitcast_convert_type(words & jnp.uint32(0xFFFF0000), F32).astype(BF16)
    return low, high


def _router_kernel(h_ref, g_ref, wr_ref, br_ref, tri_ref, hn_ref, info_ref, cnt_ref, count_ref):
    @pl.when(pl.program_id(0) == 0)
    def _():
        count_ref[...] = jnp.zeros_like(count_ref)

    x = h_ref[...]
    hn = x * lax.rsqrt(jnp.mean(x * x, axis=-1, keepdims=True) + EPS) * g_ref[...]
    hn_ref[...] = _pack_bf16_pairs(hn)
    wr = wr_ref[...]
    wr_hi = wr.astype(BF16)
    wr_lo = (wr - wr_hi.astype(F32)).astype(BF16)
    hn_hi = hn.astype(BF16)
    hn_lo = (hn - hn_hi.astype(F32)).astype(BF16)
    logits = (jnp.dot(hn_hi, wr_hi, preferred_element_type=F32)
              + jnp.dot(hn_hi, wr_lo, preferred_element_type=F32)
              + jnp.dot(hn_lo, wr_hi, preferred_element_type=F32)) + br_ref[...]
    lane = lax.broadcasted_iota(jnp.int32, logits.shape, 1)
    m1 = jnp.max(logits, axis=1, keepdims=True)
    i1 = jnp.min(jnp.where(logits == m1, lane, LANES), axis=1, keepdims=True)
    rest = jnp.where(lane == i1, -jnp.inf, logits)
    m2 = jnp.max(rest, axis=1, keepdims=True)
    i2 = jnp.min(jnp.where(rest == m2, lane, LANES), axis=1, keepdims=True)
    e2 = jnp.exp(m2 - m1)
    w1 = 1.0 / (1.0 + e2)
    w2 = e2 / (1.0 + e2)
    sel1 = lane == i1
    sel2 = lane == i2
    sel = jnp.logical_or(sel1, sel2).astype(F32)
    before = jnp.dot(tri_ref[...], sel.astype(BF16), preferred_element_type=F32) + count_ref[0:1, :]
    rank1 = jnp.sum(jnp.where(sel1, before, 0.0), axis=1, keepdims=True)
    rank2 = jnp.sum(jnp.where(sel2, before, 0.0), axis=1, keepdims=True)
    count_ref[...] = count_ref[...] + jnp.sum(sel, axis=0, keepdims=True)
    cnt_ref[...] = count_ref[...]
    info = jnp.zeros(logits.shape, F32)
    for slot, val in ((INFO_IDX1, i1.astype(F32)), (INFO_IDX2, i2.astype(F32)), (INFO_RANK1, rank1),
                      (INFO_RANK2, rank2), (INFO_W1, w1), (INFO_W2, w2)):
        info = jnp.where(lane == slot, val, info)
    info_ref[...] = info


def route(h, gain, w_router, b_router, *, bt=512):
    t, d = h.shape
    n_exp = w_router.shape[1]
    bt = _blk(t, bt)
    wr = jnp.zeros((d, LANES), F32).at[:, :n_exp].set(w_router)
    br = jnp.full((1, LANES), -1e30, F32).at[0, :n_exp].set(b_router)
    tri = (lax.broadcasted_iota(jnp.int32, (bt, bt), 1)
           < lax.broadcasted_iota(jnp.int32, (bt, bt), 0)).astype(BF16)
    vmem = 2 * (2 * bt * d * 4 + d * LANES * 4 + bt * bt * 2 + bt * LANES * 4) + 4 * bt * d * 4
    hn, info, counts = pl.pallas_call(
        _router_kernel,
        grid=(t // bt,),
        in_specs=[pl.BlockSpec((bt, d), lambda i: (i, 0)),
                  pl.BlockSpec((1, d), lambda i: (0, 0)),
                  pl.BlockSpec((d, LANES), lambda i: (0, 0)),
                  pl.BlockSpec((1, LANES), lambda i: (0, 0)),
                  pl.BlockSpec((bt, bt), lambda i: (0, 0))],
        out_specs=[pl.BlockSpec((bt, d // 2), lambda i: (i, 0)),
                   pl.BlockSpec((bt, LANES), lambda i: (i, 0)),
                   pl.BlockSpec((8, LANES), lambda i: (0, 0))],
        out_shape=[jax.ShapeDtypeStruct((t, d // 2), jnp.uint32),
                   jax.ShapeDtypeStruct((t, LANES), F32),
                   jax.ShapeDtypeStruct((8, LANES), F32)],
        scratch_shapes=[pltpu.VMEM((8, LANES), F32)],
        compiler_params=_params(("arbitrary",), vmem),
        name="moe_router",
    )(h, gain, wr, br, tri)
    return hn, info, counts[0, :n_exp]


def _combine_kernel(pos_ref, pos_next_ref, y_ref, w_ref, h_ref, o_ref, hb_ref, rstd_ref, buf_ref, sems, *, tc):
    i = pl.program_id(0)
    slot = i % 2

    def copy(pos, slot, j, k):
        return pltpu.make_async_copy(y_ref.at[pl.ds(pos[k, j], 1)], buf_ref.at[slot, k, pl.ds(j, 1)],
                                     sems.at[slot])

    def issue(pos, slot):
        def token(j, _):
            copy(pos, slot, j, 0).start(priority=0)
            copy(pos, slot, j, 1).start(priority=1)
            return 0
        lax.fori_loop(0, tc, token, 0)

    @pl.when(i == 0)
    def _():
        issue(pos_ref, 0)

    @pl.when(i + 1 < pl.num_programs(0))
    def _():
        issue(pos_next_ref, 1 - slot)

    def wait(j, _):
        copy(pos_ref, slot, j, 0).wait()
        copy(pos_ref, slot, j, 1).wait()
        return 0

    lax.fori_loop(0, tc, wait, 0, unroll=4)
    w = w_ref[...]
    out = (h_ref[...] + w[:, INFO_W1:INFO_W1 + 1] * buf_ref[slot, 0]
           + w[:, INFO_W2:INFO_W2 + 1] * buf_ref[slot, 1])
    o_ref[...] = out
    hb_ref[...] = out.astype(hb_ref.dtype)
    rstd_ref[...] = jnp.broadcast_to(lax.rsqrt(jnp.mean(out * out, axis=1, keepdims=True) + EPS),
                                     rstd_ref.shape)


def combine(y_sorted, pos, info, h, *, tc=256):
    t, d = h.shape
    tc = _blk(t, tc)
    last = t // tc - 1
    vmem = 4 * tc * d * 4 + 2 * (2 * tc * d * 4 + 2 * tc * LANES * 4 + tc * d * 2) + 2 * tc * d * 4
    return pl.pallas_call(
        functools.partial(_combine_kernel, tc=tc),
        grid=(t // tc,),
        in_specs=[pl.BlockSpec((TOP_K, tc), lambda i: (0, i), memory_space=pltpu.SMEM),
                  pl.BlockSpec((TOP_K, tc), lambda i: (0, jnp.minimum(i + 1, last)), memory_space=pltpu.SMEM),
                  pl.BlockSpec(memory_space=pl.ANY),
                  pl.BlockSpec((tc, LANES), lambda i: (i, 0)),
                  pl.BlockSpec((tc, d), lambda i: (i, 0))],
        out_specs=[pl.BlockSpec((tc, d), lambda i: (i, 0)),
                   pl.BlockSpec((tc, d), lambda i: (i, 0)),
                   pl.BlockSpec((tc, LANES), lambda i: (i, 0))],
        out_shape=[jax.ShapeDtypeStruct((t, d), F32),
                   jax.ShapeDtypeStruct((t, d), BF16),
                   jax.ShapeDtypeStruct((t, LANES), F32)],
        scratch_shapes=[pltpu.VMEM((2, TOP_K, tc, d), F32), pltpu.SemaphoreType.DMA((2,))],
        compiler_params=_params(("arbitrary",), vmem),
        name="moe_combine",
    )(pos, pos, y_sorted, info, h)


GATHER_SLOTS = 3
ROW_PARTS = 4


def _expert_glu_kernel(be_ref, nvalid_ref, rows_ref, tok_ref, src_ref, wg_ref, wu_ref, o_ref, buf_ref, x_ref, sems, *,
                       bm):
    del be_ref
    i, j, nf = pl.program_id(0), pl.program_id(1), pl.num_programs(1)
    n_valid = nvalid_ref[0]
    share = bm // nf

    def copy(block, r):
        slot = block % GATHER_SLOTS
        return pltpu.make_async_copy(src_ref.at[pl.ds(tok_ref[block * bm + r], 1)],
                                     buf_ref.at[slot, pl.ds(r, 1)], sems.at[slot])

    @pl.when(jnp.logical_and(i == 0, j == 0))
    def _():
        def row(r, _):
            copy(0, r).start(priority=0)
            copy(1, r).start(priority=1)
            return 0
        lax.fori_loop(0, bm, row, 0)

    @pl.when(jnp.logical_and(j == 0, i < n_valid + GATHER_SLOTS - 1))
    def _():
        def row(r, _):
            copy(i, r).wait()
            return 0
        lax.fori_loop(0, bm, row, 0, unroll=8)
        half = buf_ref.shape[2]
        x_ref[:, :half], x_ref[:, half:] = _unpack_bf16_pairs(buf_ref[i % GATHER_SLOTS])

    part = bm // ROW_PARTS
    parts = (rows_ref[i] + part - 1) // part
    for q in range(1, ROW_PARTS + 1):
        @pl.when(parts == q)
        def _(n=q * part):
            for r in range(share):
                copy(i + GATHER_SLOTS - 1, j * share + r).start(priority=r % 2)
            x = x_ref[0:n, :]
            g = jnp.dot(x, wg_ref[...], preferred_element_type=F32)
            u = jnp.dot(x, wu_ref[...], preferred_element_type=F32)
            o_ref[0:n, :] = (g * jax.nn.sigmoid(g) * u).astype(o_ref.dtype)
            if n < bm:
                o_ref[n:, :] = jnp.zeros((bm - n, o_ref.shape[1]), o_ref.dtype)

    @pl.when(parts == 0)
    def _():
        o_ref[...] = jnp.zeros_like(o_ref)


def expert_glu(src, tok_of_row, w_gu, block_expert, n_valid, block_rows, *, bm, bn=512):
    r = tok_of_row.shape[0]
    d = src.shape[1] * 2
    f = w_gu.shape[2] // 2
    bn = _blk(f, bn)
    nf = f // bn
    assert bm % nf == 0
    vmem = (GATHER_SLOTS * bm * d * 2 + bm * d * 2 + 2 * (2 * d * bn * 2 + bm * bn * 2) + 3 * bm * bn * 4
            + bm * d * 4)
    col = lambda i, j, nv: jnp.where(i < nv[0], j, 0)
    return pl.pallas_call(
        functools.partial(_expert_glu_kernel, bm=bm),
        grid_spec=pltpu.PrefetchScalarGridSpec(
            num_scalar_prefetch=4,
            grid=(r // bm, nf),
            in_specs=[pl.BlockSpec(memory_space=pl.ANY),
                      pl.BlockSpec((None, d, bn), lambda i, j, be, nv, rows, tok: (be[i], 0, col(i, j, nv))),
                      pl.BlockSpec((None, d, bn), lambda i, j, be, nv, rows, tok: (be[i], 0, col(i, j, nv) + nf))],
            out_specs=pl.BlockSpec((bm, bn), lambda i, j, be, nv, rows, tok: (i, j)),
            scratch_shapes=[pltpu.VMEM((GATHER_SLOTS, bm, d // 2), jnp.uint32), pltpu.VMEM((bm, d), BF16),
                            pltpu.SemaphoreType.DMA((GATHER_SLOTS,))]),
        out_shape=jax.ShapeDtypeStruct((r, f), BF16),
        compiler_params=_params(("arbitrary", "arbitrary"), vmem),
        name="moe_expert_glu",
    )(block_expert, n_valid, block_rows, tok_of_row, src, w_gu, w_gu)


def _expert_down_kernel(be_ref, nvalid_ref, rows_ref, x_ref, w_ref, o_ref):
    del be_ref, nvalid_ref
    bm = o_ref.shape[0]
    part = bm // ROW_PARTS
    parts = (rows_ref[pl.program_id(0)] + part - 1) // part
    for q in range(1, ROW_PARTS + 1):
        @pl.when(parts == q)
        def _(n=q * part):
            o_ref[0:n, :] = jnp.dot(x_ref[0:n, :], w_ref[...], preferred_element_type=F32)
            if n < bm:
                o_ref[n:, :] = jnp.zeros((bm - n, o_ref.shape[1]), o_ref.dtype)

    @pl.when(parts == 0)
    def _():
        o_ref[...] = jnp.zeros_like(o_ref)


def expert_down(a_sorted, w_down, block_expert, n_valid, block_rows, *, bm, bn=1024):
    r, f = a_sorted.shape
    d = w_down.shape[2]
    bn = _blk(d, bn)
    vmem = 2 * (bm * f * 2 + f * bn * 2 + bm * bn * 4) + bm * bn * 4
    return pl.pallas_call(
        _expert_down_kernel,
        grid_spec=pltpu.PrefetchScalarGridSpec(
            num_scalar_prefetch=3,
            grid=(r // bm, d // bn),
            in_specs=[pl.BlockSpec((bm, f), lambda i, j, be, nv, rows: (i, 0)),
                      pl.BlockSpec((None, f, bn),
                                   lambda i, j, be, nv, rows: (be[i], 0, jnp.where(i < nv[0], j, 0)))],
            out_specs=pl.BlockSpec((bm, bn), lambda i, j, be, nv, rows: (i, j))),
        out_shape=jax.ShapeDtypeStruct((r, d), F32),
        compiler_params=_params(("arbitrary", "arbitrary"), vmem),
        name="moe_expert_down",
    )(block_expert, n_valid, block_rows, a_sorted, w_down)


def moe(h, gain, w_router, b_router, w_gu, w_down, *, bm=512):
    t, d = h.shape
    n_exp = w_router.shape[1]
    bm = _blk(t, bm)
    hn, info, counts = route(h, gain, w_router, b_router)
    counts = counts.astype(jnp.int32)
    blocks_per_expert = (counts + bm - 1) // bm
    block_end = jnp.cumsum(blocks_per_expert)
    row_offset = (block_end - blocks_per_expert) * bm
    n_blocks = TOP_K * t // bm + n_exp + GATHER_SLOTS - 1
    n_valid = block_end[-1:]
    all_blocks = jnp.arange(n_blocks, dtype=jnp.int32)
    block_ids = jnp.minimum(all_blocks, n_valid[0] - 1)
    block_expert = jnp.sum((block_ids[:, None] >= block_end[None, :]).astype(jnp.int32), axis=1)
    rows_before = (block_ids - (block_end - blocks_per_expert)[block_expert]) * bm
    block_rows = jnp.where(all_blocks < n_valid[0], jnp.clip(counts[block_expert] - rows_before, 0, bm), 0)
    idx = info[:, INFO_IDX1:INFO_IDX2 + 1].astype(jnp.int32)
    rank = info[:, INFO_RANK1:INFO_RANK2 + 1].astype(jnp.int32)
    pos = (row_offset[idx] + rank).T
    token_ids = jnp.broadcast_to(jnp.arange(t, dtype=jnp.int32), (TOP_K, t))
    tok_of_row = jnp.zeros((n_blocks * bm,), jnp.int32).at[pos.reshape(-1)].set(
        token_ids.reshape(-1), unique_indices=True)
    a_sorted = expert_glu(hn, tok_of_row, w_gu, block_expert, n_valid, block_rows, bm=bm)
    y_sorted = expert_down(a_sorted, w_down, block_expert, n_valid, block_rows, bm=bm)
    return combine(y_sorted, pos, info, h)


def kernel(x, p, a_norm, a_w_in, a_conv_w, a_w_out, kv_norm, w_kv, b_norm, b_w_q, b_w_o, ffn_norm, dense_w_gu, dense_w_down, moe_w_router, moe_b_router, moe_w_gu, moe_w_down, ple_norm, ple_w_up, ple_w_gate, final_norm):
    batch, seq, d = x.shape
    depth = p.shape[0]
    n_a = a_norm.shape[0]
    t = batch * seq
    p = p.reshape(depth, t, p.shape[-1])
    bf = lambda w: w.astype(BF16)

    h = x.reshape(t, d)
    hb, rstd = stream_operands(h)
    kv = None

    to_convert = [(name, m, w[m].reshape(-1, w.shape[-1]))
                  for m in range(moe_w_gu.shape[0]) for name, w in (("down", moe_w_down), ("gu", moe_w_gu))]
    converted = {}

    def hosting(*args, **kwargs):
        if not to_convert:
            return normed_matmul(*args, **kwargs)
        name, m, w2d = to_convert[0]
        out, wb = normed_matmul(*args, cast=w2d, **kwargs)
        if wb is not None:
            converted[name, m] = wb
            to_convert.pop(0)
        return out

    def expert_weights(name, w, m):
        to_convert[:] = [job for job in to_convert if job[:2] != (name, m)]
        wb = converted.get((name, m))
        return bf(w[m]) if wb is None else wb.reshape(w.shape[1:])
    for i in range(depth):
        if i < n_a:
            bcx = hosting(hb, rstd, a_norm[i], a_w_in[i])
            mixed, w_out = short_conv(bcx, a_conv_w[i], seq), a_w_out[i]
        else:
            j = i - n_a
            if j == 0:
                kv = hosting(hb, rstd, kv_norm, w_kv)
            q = hosting(hb, rstd, b_norm[j], b_w_q[j], scale=HEAD_DIM ** -0.5 * LOG2E)
            mixed, w_out = stick_breaking_attention(q, kv, batch, seq), b_w_o[j]
        if i % 2 == 0:
            h, hb, rstd = resid_matmul(mixed, bf(w_out), h, emit_norm=True, ring=True)
            act = hosting(hb, rstd, ffn_norm[i], dense_w_gu[i // 2], glu=True, bn=256)
            h, hb, rstd = resid_matmul(act, bf(dense_w_down[i // 2]), h, emit_norm=True, bm=512)
        else:
            m = i // 2
            h = resid_matmul(mixed, bf(w_out), h, emit_norm=False, ring=True)
            h, hb, rstd = moe(h, ffn_norm[i:i + 1], moe_w_router[m], moe_b_router[m],
                              expert_weights("gu", moe_w_gu, m), expert_weights("down", moe_w_down, m))
        w_gate = bf(ple_w_gate[i] * ple_norm[i][:, None])
        if i + 1 < depth:
            h, hb, rstd = ple(hb, rstd, w_gate, p[i], bf(ple_w_up[i]), h, emit_norm=True)
        else:
            h = ple(hb, rstd, w_gate, p[i], bf(ple_w_up[i]), h, emit_norm=False)
    (out,) = rmsnorm(h, final_norm[None, :], F32)
    return out.reshape(batch, seq, d)
```
